```python
import math, functools
import jax, jax.numpy as jnp
from jax import lax
import numpy as np

D_MODEL = 1024
BATCH = 8
SEQ = 8192
DEPTH = 2

GRID_W = 64
CTX_LEN = 256
CHUNK = 64
CONV_W = 5
EPS = 1e-6
MIN_LOWER = 1e-30

SSD_HEADS = 8
SSD_HEAD_DIM = 64
SSD_WIDTH = SSD_HEADS * SSD_HEAD_DIM
SSD_GROUPS = 2
SSD_STATE = 128
SSD_CONV_DIM = SSD_WIDTH + 2 * SSD_GROUPS * SSD_STATE

HGRN_HEADS = 4
HGRN_KEY_DIM = 64
HGRN_HEAD_DIM = 64
HGRN_KEY_WIDTH = HGRN_HEADS * HGRN_KEY_DIM
HGRN_WIDTH = HGRN_HEADS * HGRN_HEAD_DIM

GDN_HEADS = 4
GDN_KEY_DIM = 64
GDN_HEAD_DIM = 64
GDN_KEY_WIDTH = GDN_HEADS * GDN_KEY_DIM
GDN_WIDTH = GDN_HEADS * GDN_HEAD_DIM
GDN_CONV_DIM = 2 * GDN_KEY_WIDTH + GDN_WIDTH

MIX_WIDTH = SSD_WIDTH + HGRN_WIDTH + GDN_WIDTH
SSD_IN = SSD_WIDTH + SSD_CONV_DIM + 2 * SSD_HEADS
HGRN_IN = 3 * HGRN_KEY_WIDTH + 2 * HGRN_WIDTH
GDN_IN = GDN_CONV_DIM + 4 * GDN_HEADS + GDN_WIDTH
IN_COLS = SSD_IN + HGRN_IN + GDN_IN

N_EXPERTS = 64
TOP_K = 8
EXPERT_DIM = 256
SHARED_DIM = 256
ROUTED_SCALE = 2.5
EXPERT_BLOCK = 128

kernel_name = 'hybrid_ssd_hgrn2_gdn_moe_diffusion_prefix'


def _split(u, sizes):
    return jnp.split(u, [int(s) for s in np.cumsum(sizes)[:-1]], axis=-1)


def _masked_exp(diff, mask):
    return jnp.where(mask, jnp.exp(jnp.where(mask, diff, 0.0)), 0.0)


def _rmsnorm(x, w):
    x32 = x.astype(jnp.float32)
    y = x32 * lax.rsqrt(jnp.mean(x32 * x32, axis=-1, keepdims=True) + EPS)
    return (y * w.astype(jnp.float32)).astype(x.dtype)


def _group_rms(t, group, w):
    shp = t.shape
    tg = t.reshape(*shp[:-1], shp[-1] // group, group)
    tg = tg * lax.rsqrt(jnp.mean(tg * tg, axis=-1, keepdims=True) + EPS) * w
    return tg.reshape(shp)


def _l2norm(t):
    return t * lax.rsqrt(jnp.sum(t * t, axis=-1, keepdims=True) + 1e-6)


def _dwconv(u, w):
    half = CONV_W // 2
    n = u.shape[-2]
    up = jnp.pad(u, [(0, 0)] * (u.ndim - 2) + [(half, half), (0, 0)])
    return sum(up[..., j:j + n, :] * w[j] for j in range(CONV_W))


def _grid_conv(u, w, rows):
    bsz, n, ch = u.shape
    return _dwconv(u.reshape(bsz, rows, GRID_W, ch), w).reshape(bsz, n, ch)


def _two_way(scan_fn, ctx_fwd, ctx_bwd, lat_fwd, lat_bwd, state0):
    rev = lambda args: tuple(jnp.flip(a, axis=1) for a in args)
    yc_f, sc_f = scan_fn(*ctx_fwd, state0)
    yc_b, sc_b = scan_fn(*rev(ctx_bwd), state0)
    yl_f, _ = scan_fn(*lat_fwd, sc_f)
    yl_b, _ = scan_fn(*rev(lat_bwd), sc_b)
    return yc_f + jnp.flip(yc_b, axis=1), yl_f + jnp.flip(yl_b, axis=1)


def _ssd_scan(xdt, da, bm, cm, h0):
    bsz, n, nh, p = xdt.shape
    g, ns = bm.shape[2], bm.shape[3]
    r = nh // g
    nc = n // CHUNK
    xc = xdt.reshape(bsz, nc, CHUNK, g, r, p)
    acs = jnp.cumsum(da.reshape(bsz, nc, CHUNK, g, r), axis=2)
    bc = bm.reshape(bsz, nc, CHUNK, g, ns)
    cc = cm.reshape(bsz, nc, CHUNK, g, ns)
    causal = jnp.tril(jnp.ones((CHUNK, CHUNK), dtype=bool))[:, :, None, None]
    decay = _masked_exp(acs[:, :, :, None] - acs[:, :, None], causal)
    cb = jnp.einsum('bctgn,bcsgn->bctsg', cc, bc)
    y_diag = jnp.einsum('bctsg,bctsgr,bcsgrp->bctgrp', cb, decay, xc)
    states = jnp.einsum('bcsgn,bcsgr,bcsgrp->bcgrpn', bc, jnp.exp(acs[:, :, -1:] - acs), xc)
    chunk_decay = jnp.exp(acs[:, :, -1])

    def step(h, inp):
        st, dec = inp
        return dec[..., None, None] * h + st, h

    h_fin, h_in = lax.scan(step, h0.reshape(bsz, g, r, p, ns),
                           (jnp.moveaxis(states, 1, 0), jnp.moveaxis(chunk_decay, 1, 0)))
    y_off = jnp.einsum('bctgn,bcgrpn,bctgr->bctgrp', cc, jnp.moveaxis(h_in, 0, 1), jnp.exp(acs))
    return (y_diag + y_off).reshape(bsz, n, nh, p), h_fin.reshape(bsz, nh, p, ns)


def _ssd_mixer(u_ctx, u_lat, lat_conv, conv_w, conv_b, dt_bias, a_log, d_skip, norm_w):
    a = -jnp.exp(a_log.astype(jnp.float32))

    def prep(u, conv):
        z, xbc, dt_f, dt_b = _split(u, [SSD_WIDTH, SSD_CONV_DIM, SSD_HEADS, SSD_HEADS])
        xbc = jax.nn.silu(conv(xbc, conv_w) + conv_b)
        xs, bm, cm = _split(xbc, [SSD_WIDTH, SSD_GROUPS * SSD_STATE, SSD_GROUPS * SSD_STATE])
        bsz, n = u.shape[:2]
        xs = xs.reshape(bsz, n, SSD_HEADS, SSD_HEAD_DIM)
        bm = bm.reshape(bsz, n, SSD_GROUPS, SSD_STATE)
        cm = cm.reshape(bsz, n, SSD_GROUPS, SSD_STATE)
        dtf = jax.nn.softplus(dt_f + dt_bias[0])
        dtb = jax.nn.softplus(dt_b + dt_bias[1])
        fwd = (xs * dtf[..., None], dtf * a[0], bm, cm)
        bwd = (xs * dtb[..., None], dtb * a[1], bm, cm)
        return z, xs, fwd, bwd

    zc, xc, cf, cb = prep(u_ctx, _dwconv)
    zl, xl, lf, lb = prep(u_lat, lat_conv)
    h0 = jnp.zeros((u_lat.shape[0], SSD_HEADS, SSD_HEAD_DIM, SSD_STATE), jnp.float32)
    yc, yl = _two_way(_ssd_scan, cf, cb, lf, lb, h0)

    def out(y, xs, z):
        y = (y + d_skip[:, None] * xs).reshape(z.shape)
        return _group_rms(y * jax.nn.silu(z), SSD_WIDTH // SSD_GROUPS, norm_w.reshape(SSD_GROUPS, -1))

    return out(yc, xc, zc), out(yl, xl, zl)


def _hgrn2_scan(q, logf, k, v, s0):
    bsz, n, nh, dk = q.shape
    dv = v.shape[-1]
    nc = n // CHUNK
    chunks = lambda t: jnp.moveaxis(t.reshape(bsz, nc, CHUNK, *t.shape[2:]), 1, 0)
    causal = jnp.tril(jnp.ones((CHUNK, CHUNK), dtype=bool))[None, :, :, None, None]

    def step(s, inp):
        qc, gc, kc, vc = inp
        b = jnp.cumsum(gc, axis=1)
        w = _masked_exp(b[:, :, None] - b[:, None, :], causal)
        att = jnp.einsum('bthk,bshk,btshk->bhts', qc, kc, w)
        o = jnp.einsum('bhts,bshv->bthv', att, vc) + jnp.einsum('bthk,bhkv->bthv', qc * jnp.exp(b), s)
        b_last = b[:, -1]
        s = jnp.exp(b_last)[..., None] * s + jnp.einsum('bshk,bshv->bhkv', kc * jnp.exp(b_last[:, None] - b), vc)
        return s, o

    s_fin, o = lax.scan(step, s0, (chunks(q), chunks(logf), chunks(k), chunks(v)))
    return jnp.moveaxis(o, 0, 1).reshape(bsz, n, nh, dv), s_fin


def _hgrn_mixer(u_ctx, u_lat, lower, norm_w):
    log_lower = jnp.log(jnp.maximum(lower, MIN_LOWER))

    def gate(fr):
        logf = jnp.logaddexp(jax.nn.log_sigmoid(fr), log_lower + jax.nn.log_sigmoid(-fr))
        return logf, (1.0 - lower) * jax.nn.sigmoid(-fr)

    def prep(u):
        q, f_f, f_b, i, g = _split(u, [HGRN_KEY_WIDTH, HGRN_KEY_WIDTH, HGRN_KEY_WIDTH, HGRN_WIDTH, HGRN_WIDTH])
        bsz, n = u.shape[:2]
        heads = lambda t, d: t.reshape(bsz, n, HGRN_HEADS, d)
        q = heads(jax.nn.silu(q), HGRN_KEY_DIM)
        i = heads(i, HGRN_HEAD_DIM)
        lf, kf = gate(f_f)
        lbw, kb = gate(f_b)
        fwd = (q, heads(lf, HGRN_KEY_DIM), heads(kf, HGRN_KEY_DIM), i)
        bwd = (q, heads(lbw, HGRN_KEY_DIM), heads(kb, HGRN_KEY_DIM), i)
        return g, fwd, bwd

    gc, cf, cb = prep(u_ctx)
    gl, lf, lb = prep(u_lat)
    s0 = jnp.zeros((u_lat.shape[0], HGRN_HEADS, HGRN_KEY_DIM, HGRN_HEAD_DIM), jnp.float32)
    oc, ol = _two_way(_hgrn2_scan, cf, cb, lf, lb, s0)
    out = lambda o, g: _group_rms(o.reshape(g.shape), HGRN_HEAD_DIM, norm_w) * jax.nn.silu(g)
    return out(oc, gc), out(ol, gl)


def _gdn_scan(q, k, v, g, beta, s0):
    bsz, n, nh, dk = q.shape
    dv = v.shape[-1]
    nc = n // CHUNK

    def to_chunks(t):
        return jnp.moveaxis(t.reshape(bsz, nc, CHUNK, nh, *t.shape[3:]), 3, 2)

    qc, kc, vc, bc = to_chunks(q), to_chunks(k), to_chunks(v), to_chunks(beta)
    gam = jnp.cumsum(to_chunks(g), axis=-1)
    incl = jnp.tril(jnp.ones((CHUNK, CHUNK), dtype=bool))
    strict = jnp.tril(jnp.ones((CHUNK, CHUNK), dtype=bool), k=-1)
    decay = _masked_exp(gam[..., :, None] - gam[..., None, :], incl)
    kb = kc * bc[..., None]
    a_strict = jnp.where(strict, jnp.einsum('bchtk,bchsk->bchts', kb, kc) * decay, 0.0)
    lhs = a_strict + jnp.eye(CHUNK, dtype=a_strict.dtype)
    rhs = jnp.concatenate([vc * bc[..., None], kb * jnp.exp(gam)[..., None]], axis=-1)
    sol = lax.linalg.triangular_solve(lhs, rhs, left_side=True, lower=True, unit_diagonal=True)
    u, w = sol[..., :dv], sol[..., dv:]
    qk = jnp.einsum('bchtk,bchsk->bchts', qc, kc) * decay
    qg = qc * jnp.exp(gam)[..., None]
    kend = kc * jnp.exp(gam[..., -1:] - gam)[..., None]
    dec = jnp.exp(gam[..., -1])

    def step(s, inp):
        u_c, w_c, qk_c, qg_c, ke_c, d_c = inp
        v_new = u_c - jnp.einsum('bhtk,bhkv->bhtv', w_c, s)
        o = jnp.einsum('bhtk,bhkv->bhtv', qg_c, s) + jnp.einsum('bhts,bhsv->bhtv', qk_c, v_new)
        s = d_c[..., None, None] * s + jnp.einsum('bhsk,bhsv->bhkv', ke_c, v_new)
        return s, o

    mv = lambda t: jnp.moveaxis(t, 1, 0)
    s_fin, o = lax.scan(step, s0, (mv(u), mv(w), mv(qk), mv(qg), mv(kend), mv(dec)))
    o = jnp.moveaxis(jnp.moveaxis(o, 0, 1), 2, 3).reshape(bsz, n, nh, dv)
    return o, s_fin


def _gdn_mixer(u_ctx, u_lat, lat_conv, conv_w, dt_bias, a_log, norm_w):
    a = jnp.exp(a_log.astype(jnp.float32))

    def prep(u, conv):
        qkv, a_f, a_b, b_f, b_b, g = _split(u, [GDN_CONV_DIM, GDN_HEADS, GDN_HEADS, GDN_HEADS, GDN_HEADS, GDN_WIDTH])
        q, k, v = _split(jax.nn.silu(conv(qkv, conv_w)), [GDN_KEY_WIDTH, GDN_KEY_WIDTH, GDN_WIDTH])
        bsz, n = u.shape[:2]
        q = _l2norm(q.reshape(bsz, n, GDN_HEADS, GDN_KEY_DIM)) * GDN_KEY_DIM ** -0.5
        k = _l2norm(k.reshape(bsz, n, GDN_HEADS, GDN_KEY_DIM))
        v = v.reshape(bsz, n, GDN_HEADS, GDN_HEAD_DIM)
        g_f = -a[0] * jax.nn.softplus(a_f + dt_bias[0])
        g_b = -a[1] * jax.nn.softplus(a_b + dt_bias[1])
        return g, (q, k, v, g_f, jax.nn.sigmoid(b_f)), (q, k, v, g_b, jax.nn.sigmoid(b_b))

    gc, cf, cb = prep(u_ctx, _dwconv)
    gl, lf, lb = prep(u_lat, lat_conv)
    s0 = jnp.zeros((u_lat.shape[0], GDN_HEADS, GDN_KEY_DIM, GDN_HEAD_DIM), jnp.float32)
    oc, ol = _two_way(_gdn_scan, cf, cb, lf, lb, s0)
    out = lambda o, g: _group_rms(o.reshape(g.shape), GDN_HEAD_DIM, norm_w) * jax.nn.silu(g)
    return out(oc, gc), out(ol, gl)


def _routed_experts(h, router_w, router_bias, w_gate, w_up, w_down):
    n_tok, dm = h.shape
    n_assign = n_tok * TOP_K
    scores = jax.nn.sigmoid(jnp.dot(h.astype(jnp.float32), router_w.astype(jnp.float32)))
    _, idx = lax.top_k(scores + router_bias.astype(jnp.float32), TOP_K)
    gate = jnp.take_along_axis(scores, idx, axis=-1)
    gate = gate / jnp.sum(gate, axis=-1, keepdims=True) * ROUTED_SCALE
    flat_e = idx.reshape(-1)
    order = jnp.argsort(flat_e)
    e_sorted = flat_e[order]
    token = order // TOP_K
    counts = jnp.bincount(flat_e, length=N_EXPERTS)
    padded = (counts + EXPERT_BLOCK - 1) // EXPERT_BLOCK * EXPERT_BLOCK
    pad_end = jnp.cumsum(padded)
    slot = (pad_end - padded)[e_sorted] + jnp.arange(n_assign) - (jnp.cumsum(counts) - counts)[e_sorted]
    n_blocks = -(-n_assign // EXPERT_BLOCK) + N_EXPERTS
    buf = jnp.zeros((n_blocks * EXPERT_BLOCK, dm), h.dtype).at[slot].set(h[token])
    block_expert = jnp.minimum(
        jnp.searchsorted(pad_end, jnp.arange(n_blocks) * EXPERT_BLOCK, side='right'), N_EXPERTS - 1)

    def expert_block(args):
        xb, e = args
        return (jax.nn.silu(xb @ w_gate[e]) * (xb @ w_up[e])) @ w_down[e]

    out = lax.map(expert_block, (buf.reshape(n_blocks, EXPERT_BLOCK, dm), block_expert))
    out = out.reshape(n_blocks * EXPERT_BLOCK, dm)
    contrib = out[slot] * gate.reshape(-1)[order][:, None].astype(out.dtype)
    return jnp.zeros_like(h).at[token].add(contrib)


def _moe(h, router_w, router_bias, w_gate, w_up, w_down, sh_gate, sh_up, sh_down):
    routed = lax.map(lambda hs: _routed_experts(hs, router_w, router_bias, w_gate, w_up, w_down), h)
    shared = (jax.nn.silu(h @ sh_gate) * (h @ sh_up)) @ sh_down
    return routed + shared


def setup_inputs(seed: int = 0) -> dict:
    key = jax.random.key(seed)
    ks = jax.random.split(key, 31)
    nrm = lambda k, shape, scale: jax.random.normal(k, shape, jnp.float32) * scale

    def dt_bias(k, shape):
        dt = jnp.exp(jax.random.uniform(k, shape, jnp.float32) * (math.log(0.1) - math.log(1e-3)) + math.log(1e-3))
        return dt + jnp.log(-jnp.expm1(-dt))

    def a_log(k, shape):
        return jnp.log(jax.random.uniform(k, shape, jnp.float32, minval=1.0, maxval=16.0))

    L = DEPTH
    return {
        'x': nrm(ks[0], (BATCH, SEQ, D_MODEL), 1.0),
        'c': nrm(ks[1], (BATCH, D_MODEL), 1.0),
        'ctx': nrm(ks[2], (BATCH, CTX_LEN, D_MODEL), 1.0),
        'c_ctx': nrm(ks[3], (D_MODEL,), 1.0),
        'w_mod': nrm(ks[4], (L, D_MODEL, 6 * D_MODEL), 0.5 * D_MODEL ** -0.5),
        'b_mod': nrm(ks[5], (L, 6 * D_MODEL), 0.02),
        'norm_mix': 1.0 + nrm(ks[6], (L, D_MODEL), 0.02),
        'w_in': nrm(ks[7], (L, D_MODEL, IN_COLS), D_MODEL ** -0.5),
        'ssd_conv_w': nrm(ks[8], (L, CONV_W, SSD_CONV_DIM), CONV_W ** -0.5),
        'ssd_conv_b': nrm(ks[9], (L, SSD_CONV_DIM), 0.02),
        'ssd_dt_bias': dt_bias(ks[10], (L, 2, SSD_HEADS)),
        'ssd_a_log': a_log(ks[11], (L, 2, SSD_HEADS)),
        'ssd_d': 1.0 + nrm(ks[12], (L, SSD_HEADS), 0.02),
        'ssd_norm': 1.0 + nrm(ks[13], (L, SSD_WIDTH), 0.02),
        'hgrn_lb': nrm(ks[14], (L, HGRN_KEY_WIDTH), 0.1),
        'hgrn_norm': 1.0 + nrm(ks[15], (L, HGRN_HEAD_DIM), 0.02),
        'gdn_conv_w': nrm(ks[16], (L, CONV_W, GDN_CONV_DIM), CONV_W ** -0.5),
        'gdn_dt_bias': dt_bias(ks[17], (L, 2, GDN_HEADS)),
        'gdn_a_log': a_log(ks[18], (L, 2, GDN_HEADS)),
        'gdn_norm': 1.0 + nrm(ks[19], (L, GDN_HEAD_DIM), 0.02),
        'w_out': nrm(ks[20], (L, MIX_WIDTH, D_MODEL), MIX_WIDTH ** -0.5),
        'norm_ffn': 1.0 + nrm(ks[21], (L, D_MODEL), 0.02),
        'router_w': nrm(ks[22], (L, D_MODEL, N_EXPERTS), D_MODEL ** -0.5),
        'router_bias': nrm(ks[23], (L, N_EXPERTS), 0.01),
        'exp_gate': nrm(ks[24], (L, N_EXPERTS, D_MODEL, EXPERT_DIM), D_MODEL ** -0.5),
        'exp_up': nrm(ks[25], (L, N_EXPERTS, D_MODEL, EXPERT_DIM), D_MODEL ** -0.5),
        'exp_down': nrm(ks[26], (L, N_EXPERTS, EXPERT_DIM, D_MODEL), EXPERT_DIM ** -0.5),
        'sh_gate': nrm(ks[27], (L, D_MODEL, SHARED_DIM), D_MODEL ** -0.5),
        'sh_up': nrm(ks[28], (L, D_MODEL, SHARED_DIM), D_MODEL ** -0.5),
        'sh_down': nrm(ks[29], (L, SHARED_DIM, D_MODEL), SHARED_DIM ** -0.5),
        'norm_final': 1.0 + nrm(ks[30], (D_MODEL,), 0.02),
    }


def reference(x, c, ctx, c_ctx, w_mod, b_mod, norm_mix, w_in, ssd_conv_w, ssd_conv_b, ssd_dt_bias,
              ssd_a_log, ssd_d, ssd_norm, hgrn_lb, hgrn_norm, gdn_conv_w, gdn_dt_bias, gdn_a_log, gdn_norm,
              w_out, norm_ffn, router_w, router_bias, exp_gate, exp_up, exp_down, sh_gate, sh_up, sh_down,
              norm_final):
    rows = x.shape[1] // GRID_W
    lat_conv = functools.partial(_grid_conv, rows=rows)
    p_lb = jax.nn.softmax(hgrn_lb.astype(jnp.float32), axis=0)
    lower_all = jnp.cumsum(p_lb, axis=0) - p_lb[0]
    s_lat = jax.nn.silu(c)
    s_ctx = jax.nn.silu(c_ctx)[None]
    h, hc = x, ctx
    for l in range(DEPTH):
        mod_l = (s_lat @ w_mod[l] + b_mod[l])[:, None, :]
        mod_c = (s_ctx @ w_mod[l] + b_mod[l])[:, None, :]
        sh_m, sc_m, g_m, sh_f, sc_f, g_f = jnp.split(mod_l, 6, axis=-1)
        csh_m, csc_m, cg_m, csh_f, csc_f, cg_f = jnp.split(mod_c, 6, axis=-1)

        a_l = _rmsnorm(h, norm_mix[l]) * (1.0 + sc_m) + sh_m
        a_c = _rmsnorm(hc, norm_mix[l]) * (1.0 + csc_m) + csh_m
        u_l = (a_l @ w_in[l]).astype(jnp.float32)
        u_c = (a_c @ w_in[l]).astype(jnp.float32)
        ssd_c, hg_c, gd_c = _split(u_c, [SSD_IN, HGRN_IN, GDN_IN])
        ssd_l, hg_l, gd_l = _split(u_l, [SSD_IN, HGRN_IN, GDN_IN])
        ys_c, ys_l = _ssd_mixer(ssd_c, ssd_l, lat_conv, ssd_conv_w[l], ssd_conv_b[l], ssd_dt_bias[l],
                                ssd_a_log[l], ssd_d[l], ssd_norm[l])
        yh_c, yh_l = _hgrn_mixer(hg_c, hg_l, lower_all[l], hgrn_norm[l])
        yg_c, yg_l = _gdn_mixer(gd_c, gd_l, lat_conv, gdn_conv_w[l], gdn_dt_bias[l], gdn_a_log[l], gdn_norm[l])
        y_l = jnp.concatenate([ys_l, yh_l, yg_l], axis=-1).astype(h.dtype)
        h = h + g_m * (y_l @ w_out[l])

        f_l = _rmsnorm(h, norm_ffn[l]) * (1.0 + sc_f) + sh_f
        h = h + g_f * _moe(f_l, router_w[l], router_bias[l], exp_gate[l], exp_up[l], exp_down[l],
                           sh_gate[l], sh_up[l], sh_down[l])

        if l < DEPTH - 1:
            y_c = jnp.concatenate([ys_c, yh_c, yg_c], axis=-1).astype(hc.dtype)
            hc = hc + cg_m * (y_c @ w_out[l])
            f_c = _rmsnorm(hc, norm_ffn[l]) * (1.0 + csc_f) + csh_f
            hc = hc + cg_f * _moe(f_c, router_w[l], router_bias[l], exp_gate[l], exp_up[l], exp_down[l],
                                  sh_gate[l], sh_up[l], sh_down[l])
    return _rmsnorm(h, norm_final)
```

```python
import functools

import numpy as np
import jax
import jax.numpy as jnp
from jax import lax
from jax.experimental import pallas as pl
from jax.experimental.pallas import tpu as pltpu

F32 = jnp.float32
BF16 = jnp.bfloat16

D_MODEL = 1024
GRID_W = 64
CONV_W = 5
EPS = 1e-6
MIN_LOWER = 1e-30
HEAD = 64
SSD_HEADS = 8
SSD_WIDTH = 512
SSD_STATE = 128
SSD_GROUPS = 2
HG_WIDTH = 256
GD_WIDTH = 256
GD_HEADS = 4
N_EXPERTS = 64
TOP_K = 8
EXPERT_DIM = 256
ROUTED_SCALE = 2.5

LANES = 128
SCAN_CHUNK = 128
BASE = 16
TOKEN_TILE = 256
EXPERT_ROWS = 256
VMEM_LIMIT = 56 * 1024 * 1024

Z_OFF, X_OFF, BC_OFF = 0, 512, 1024
HQ_OFF, HFF_OFF, HFB_OFF, HI_OFF, HGATE_OFF = 1536, 1792, 2048, 2304, 2560
GQ_OFF, GK_OFF, GV_OFF, GGATE_OFF = 2816, 3072, 3328, 3584
SM_OFF = 3840
NCOLS = 3968
SSD_CONV = 1024
GDN_CONV = 768
SM_DT = (0, 8)
SM_A = (16, 20)
SM_B = (24, 28)


def _src_columns():
    src = np.full((NCOLS,), -1, np.int64)
    def put(dst, s0, n):
        src[dst:dst + n] = np.arange(s0, s0 + n)
    put(Z_OFF, 0, 512)
    put(X_OFF, 512, 1024)
    put(SM_OFF + 0, 1536, 16)
    hb = 1552
    put(HQ_OFF, hb, 1280)
    gb = 2832
    put(GQ_OFF, gb, 768)
    put(SM_OFF + 16, gb + 768, 16)
    put(GGATE_OFF, gb + 784, 256)
    return src


def _dot(a, b, prec=None):
    return lax.dot_general(a, b, (((1,), (0,)), ((), ())), precision=prec, preferred_element_type=F32)


def _dot_nt(a, b, prec=None):
    return lax.dot_general(a, b, (((1,), (1,)), ((), ())), precision=prec, preferred_element_type=F32)


def _split(a):
    hi = a.astype(BF16)
    lo = (a - hi.astype(F32)).astype(BF16)
    return hi, lo


def _dot3(a, b):
    ah, al = _split(a)
    bh, bl = _split(b)
    return _dot(ah, bh) + (_dot(ah, bl) + _dot(al, bh))


def _dot3_exact_rhs(a, b_bf16):
    ah, al = _split(a)
    return _dot(ah, b_bf16) + _dot(al, b_bf16)


def _silu(x):
    return x * jax.nn.sigmoid(x)


def _softplus(x):
    return jnp.maximum(x, 0.0) + jnp.log1p(jnp.exp(-jnp.abs(x)))


def _log_sigmoid(x):
    return jnp.minimum(x, 0.0) - jnp.log1p(jnp.exp(-jnp.abs(x)))


def _params(sem):
    return pltpu.CompilerParams(dimension_semantics=sem, vmem_limit_bytes=VMEM_LIMIT)


def _iota(shape, dim):
    return lax.broadcasted_iota(jnp.int32, shape, dim)


def _ones_where(mask):
    return jnp.where(mask, 1.0, 0.0).astype(BF16)


def _shr(x, div):
    return jnp.right_shift(x, int(np.log2(div)))


def _scan_mask(n, rev):
    r, c = _iota((n, n), 0), _iota((n, n), 1)
    return (c >= r) if rev else (c <= r)


def _expand_matrix(lane0, heads, width):
    r, c = _iota((LANES, width), 0), _iota((LANES, width), 1)
    return _ones_where(r == lane0 + _shr(c, HEAD))


def _head_blocks(n):
    r, c = _iota((n, n), 0), _iota((n, n), 1)
    return _shr(r, HEAD) == _shr(c, HEAD)


def _mod_kernel(s_ref, w_ref, b_ref, o_ref):
    s = _silu(s_ref[...])
    o_ref[...] = _dot(s.astype(BF16), w_ref[...].astype(BF16)) + b_ref[...]


def _modulation(cond, w, b):
    rows, d = cond.shape
    n = w.shape[1]
    bn = d
    return pl.pallas_call(
        _mod_kernel,
        grid=(n // bn,),
        in_specs=[pl.BlockSpec((rows, d), lambda j: (0, 0)),
                  pl.BlockSpec((d, bn), lambda j: (0, j)),
                  pl.BlockSpec((1, bn), lambda j: (0, j))],
        out_specs=pl.BlockSpec((rows, bn), lambda j: (0, j)),
        out_shape=jax.ShapeDtypeStruct((rows, n), F32),
        compiler_params=_params(("parallel",)),
        name="modulation",
    )(cond, w, b.reshape(1, n))


def _inproj_kernel(h_ref, sc_ref, sh_ref, nw_ref, w_ref, cws_ref, cbs_ref, cwg_ref, u_ref, *, ctx_tiles):
    tm = h_ref.shape[1]
    t = pl.program_id(1)
    h = h_ref[0]
    a = h * lax.rsqrt(jnp.mean(h * h, axis=-1, keepdims=True) + EPS) * nw_ref[...]
    a = a * (1.0 + sc_ref[0]) + sh_ref[0]
    ab = a.astype(BF16)
    seg = jnp.where(t < ctx_tiles, tm, GRID_W)
    pos = _iota((tm, 1), 0) & (seg - 1)
    half = CONV_W // 2
    masks = {o: (pos + o >= 0) & (pos + o < seg) for o in range(-half, half + 1) if o}

    def conv(x, cw_ref, c0, wd, cb_ref):
        acc = x * cw_ref[half:half + 1, c0:c0 + wd]
        for j in range(CONV_W):
            o = j - half
            if o == 0:
                continue
            shifted = pltpu.roll(x, (-o) % tm, 0)
            acc = acc + jnp.where(masks[o], shifted, 0.0) * cw_ref[j:j + 1, c0:c0 + wd]
        if cb_ref is not None:
            acc = acc + cb_ref[:, c0:c0 + wd]
        return _silu(acc)

    step = 2 * LANES
    for c0 in range(0, NCOLS, step):
        wd = min(step, NCOLS - c0)
        u = _dot(ab, w_ref[:, c0:c0 + wd])
        if X_OFF <= c0 < X_OFF + SSD_CONV:
            u = conv(u, cws_ref, c0 - X_OFF, wd, cbs_ref)
        elif GQ_OFF <= c0 < GQ_OFF + GDN_CONV:
            u = conv(u, cwg_ref, c0 - GQ_OFF, wd, None)
        u_ref[0, :, c0:c0 + wd] = u


def _inproj(h, mod, norm_w, w_p, conv_s, bias_s, conv_g, *, ctx):
    bsz, t_all, d = h.shape
    tm = TOKEN_TILE
    ctx_tiles = ctx // tm
    rows = mod.shape[0] // 6

    def mrow(k):
        return lambda b, t: (jnp.where(t < ctx_tiles, rows - 1, b) * 6 + k, 0, 0)

    return pl.pallas_call(
        functools.partial(_inproj_kernel, ctx_tiles=ctx_tiles),
        grid=(bsz, t_all // tm),
        in_specs=[pl.BlockSpec((1, tm, d), lambda b, t: (b, t, 0)),
                  pl.BlockSpec((1, 1, d), mrow(1)),
                  pl.BlockSpec((1, 1, d), mrow(0)),
                  pl.BlockSpec((1, d), lambda b, t: (0, 0)),
                  pl.BlockSpec((d, NCOLS), lambda b, t: (0, 0)),
                  pl.BlockSpec((CONV_W, SSD_CONV), lambda b, t: (0, 0)),
                  pl.BlockSpec((1, SSD_CONV), lambda b, t: (0, 0)),
                  pl.BlockSpec((CONV_W, GDN_CONV), lambda b, t: (0, 0))],
        out_specs=pl.BlockSpec((1, tm, NCOLS), lambda b, t: (b, t, 0)),
        out_shape=jax.ShapeDtypeStruct((bsz, t_all, NCOLS), F32),
        compiler_params=_params(("parallel", "parallel")),
        name="in_projection",
    )(h, mod, mod, norm_w.reshape(1, d), w_p, conv_s, bias_s.reshape(1, -1), conv_g)


def _chunk_index(rev, n_ctx, n_all):
    if not rev:
        return lambda c: c
    return lambda c: jnp.where(c < n_ctx, n_ctx - 1 - c, n_all + n_ctx - 1 - c)


def _ssd_kernel(x_ref, bc_ref, sm_ref, bias_ref, arow_ref, drow_ref, y_ref, st_ref, *, rev):
    cn = x_ref.shape[1]

    @pl.when(pl.program_id(1) == 0)
    def _():
        st_ref[...] = jnp.zeros_like(st_ref)

    xs = x_ref[0]
    bc = bc_ref[0]
    lane0 = SM_DT[1] if rev else SM_DT[0]
    dt = _softplus(sm_ref[0] + bias_ref[...])
    da = dt * arow_ref[...]
    mask = _scan_mask(cn, rev)
    acs = _cumsum_rows(da, rev)
    acs_r = acs.T
    ex = _expand_matrix(lane0, SSD_HEADS, SSD_WIDTH)
    acs_e = _dot3_exact_rhs(acs, ex)
    dt_e = _dot3_exact_rhs(dt, ex)
    last = 0 if rev else cn - 1
    tot_e = acs_e[last:last + 1]
    xdt = xs * dt_e
    xdt_b = xdt.astype(BF16)
    wst = (xdt * jnp.exp(tot_e - acs_e)).astype(BF16)
    eacs = jnp.exp(acs_e)
    etot = jnp.exp(tot_e)
    gw = SSD_WIDTH // SSD_GROUPS
    hpg = SSD_HEADS // SSD_GROUPS
    lane_head = _shr(_iota((1, gw), 1), HEAD)
    for g in range(SSD_GROUPS):
        bm = bc[:, g * SSD_STATE:(g + 1) * SSD_STATE]
        cm = bc[:, (SSD_GROUPS + g) * SSD_STATE:(SSD_GROUPS + g + 1) * SSD_STATE]
        bmb, cmb = bm.astype(BF16), cm.astype(BF16)
        cb = _dot_nt(cmb, bmb)
        xg = xdt_b[:, g * gw:(g + 1) * gw]
        yd = jnp.zeros((cn, gw), F32)
        for r in range(hpg):
            ln = lane0 + g * hpg + r
            diff = acs[:, ln:ln + 1] - acs_r[ln:ln + 1, :]
            dec = jnp.where(mask, jnp.exp(jnp.where(mask, diff, 0.0)), 0.0)
            yh = _dot((cb * dec).astype(BF16), xg)
            yd = jnp.where(lane_head == r, yh, yd)
        st = st_ref[g]
        yo = _dot(cmb, st.astype(BF16)) * eacs[:, g * gw:(g + 1) * gw]
        y = yd + yo
        if drow_ref is not None:
            y = y + drow_ref[:, g * gw:(g + 1) * gw] * xs[:, g * gw:(g + 1) * gw]
        y_ref[0, :, g * gw:(g + 1) * gw] = y
        st_ref[g] = st * etot[:, g * gw:(g + 1) * gw] + _dot(bm.T.astype(BF16), wst[:, g * gw:(g + 1) * gw])


def _cumsum_rows(x, rev):
    tri = _ones_where(_scan_mask(x.shape[0], rev))
    xh, xl = _split(x)
    return _dot(tri, xh) + _dot(tri, xl)


def _ssd_scan(u, dt_bias, a_log, d_skip, *, rev, n_ctx):
    bsz, t_all, _ = u.shape
    cn = SCAN_CHUNK
    n_all = t_all // cn
    cidx = _chunk_index(rev, n_ctx, n_all)
    di = 1 if rev else 0
    lane0 = SM_DT[di]
    bias = jnp.zeros((1, LANES), F32).at[0, lane0:lane0 + SSD_HEADS].set(dt_bias[di])
    arow = jnp.zeros((1, LANES), F32).at[0, lane0:lane0 + SSD_HEADS].set(-jnp.exp(a_log[di].astype(F32)))
    in_specs = [pl.BlockSpec((1, cn, SSD_WIDTH), lambda b, c: (b, cidx(c), X_OFF // SSD_WIDTH)),
                pl.BlockSpec((1, cn, 512), lambda b, c: (b, cidx(c), BC_OFF // 512)),
                pl.BlockSpec((1, cn, LANES), lambda b, c: (b, cidx(c), SM_OFF // LANES)),
                pl.BlockSpec((1, LANES), lambda b, c: (0, 0)),
                pl.BlockSpec((1, LANES), lambda b, c: (0, 0))]
    args = [u, u, u, bias, arow]
    if not rev:
        in_specs.append(pl.BlockSpec((1, SSD_WIDTH), lambda b, c: (0, 0)))
        args.append(jnp.repeat(d_skip.astype(F32), HEAD).reshape(1, SSD_WIDTH))
        body = functools.partial(_ssd_kernel, rev=rev)
    else:
        body = lambda x, bc, sm, bi, ar, y, st: _ssd_kernel(x, bc, sm, bi, ar, None, y, st, rev=rev)
    return pl.pallas_call(
        body,
        grid=(bsz, n_all),
        in_specs=in_specs,
        out_specs=pl.BlockSpec((1, cn, SSD_WIDTH), lambda b, c: (b, cidx(c), 0)),
        out_shape=jax.ShapeDtypeStruct((bsz, t_all, SSD_WIDTH), F32),
        scratch_shapes=[pltpu.VMEM((SSD_GROUPS, SSD_STATE, SSD_WIDTH // SSD_GROUPS), F32)],
        compiler_params=_params(("parallel", "arbitrary")),
        name="ssd_scan_bwd" if rev else "ssd_scan_fwd",
    )(*args)


def _hgrn_kernel(q_ref, f_ref, i_ref, low_ref, y_ref, st_ref, *, rev):
    cn = q_ref.shape[1]
    wdt = HG_WIDTH

    @pl.when(pl.program_id(1) == 0)
    def _():
        st_ref[...] = jnp.zeros_like(st_ref)

    lower = low_ref[0:1]
    log_lower = low_ref[1:2]
    fr = f_ref[0]
    qs = _silu(q_ref[0])
    v = i_ref[0]
    ls_p, ls_n = _log_sigmoid(fr), _log_sigmoid(-fr)
    a1, a2 = ls_p, log_lower + ls_n
    logf = jnp.maximum(a1, a2) + jnp.log1p(jnp.exp(-jnp.abs(a1 - a2)))
    kg = (1.0 - lower) * jax.nn.sigmoid(-fr)
    b = _cumsum_rows(logf, rev)
    vb = v.astype(BF16)
    hb = _head_blocks(wdt)
    lane_head = _shr(_iota((1, wdt), 1), HEAD)

    st = st_ref[...]
    y_ref[0] = _dot_nt((qs * jnp.exp(b)).astype(BF16), st.astype(BF16))
    last = 0 if rev else cn - 1
    b_last = b[last:last + 1]
    kend = (kg * jnp.exp(b_last - b)).astype(BF16)
    upd = _dot(v.T.astype(BF16), kend)
    st_ref[...] = st * jnp.exp(b_last) + jnp.where(hb, upd, 0.0)

    heads = wdt // HEAD

    def offdiag(t0, t1, s0, s1, r):
        br = b[r:r + 1]
        qp = qs[t0:t1] * jnp.exp(b[t0:t1] - br)
        kp = (kg[s0:s1] * jnp.exp(br - b[s0:s1])).astype(BF16)
        nt = t1 - t0
        qstack = jnp.concatenate([jnp.where(lane_head == h, qp, 0.0) for h in range(heads)], axis=0).astype(BF16)
        att = _dot_nt(qstack, kp)
        res = _dot(att.astype(BF16), vb[s0:s1])
        out = jnp.zeros((nt, wdt), F32)
        for h in range(heads):
            out = jnp.where(lane_head == h, res[h * nt:(h + 1) * nt], out)
        y_ref[0, t0:t1, :] += out

    bd = _ones_where(hb)

    def diag(t0, t1):
        n = t1 - t0
        bt = b[t0:t1]
        ti, si = _iota((n, n, wdt), 0), _iota((n, n, wdt), 1)
        m3 = (si >= ti) if rev else (si <= ti)
        diff = bt[:, None, :] - bt[None, :, :]
        w = jnp.where(m3, jnp.exp(jnp.where(m3, diff, 0.0)), 0.0)
        p = w * qs[t0:t1][:, None, :] * kg[t0:t1][None, :, :]
        r2 = _dot(p.reshape(n * n, wdt).astype(BF16), bd)
        o = jnp.sum(r2.reshape(n, n, wdt) * v[t0:t1][None, :, :], axis=1)
        y_ref[0, t0:t1, :] += o

    def block(lo, hi):
        if hi - lo <= BASE:
            diag(lo, hi)
            return
        mid = (lo + hi) // 2
        if rev:
            offdiag(lo, mid, mid, hi, mid)
        else:
            offdiag(mid, hi, lo, mid, mid - 1)
        block(lo, mid)
        block(mid, hi)

    block(0, cn)


def _hgrn_scan(u, lower, *, rev, n_ctx):
    bsz, t_all, _ = u.shape
    cn = SCAN_CHUNK
    n_all = t_all // cn
    cidx = _chunk_index(rev, n_ctx, n_all)
    f_off = HFB_OFF if rev else HFF_OFF
    low = jnp.stack([lower, jnp.log(jnp.maximum(lower, MIN_LOWER))]).astype(F32)
    low = jnp.concatenate([low, jnp.zeros((6, HG_WIDTH), F32)], axis=0)
    blk = lambda off: pl.BlockSpec((1, cn, HG_WIDTH), lambda b, c: (b, cidx(c), off // HG_WIDTH))
    return pl.pallas_call(
        functools.partial(_hgrn_kernel, rev=rev),
        grid=(bsz, n_all),
        in_specs=[blk(HQ_OFF), blk(f_off), blk(HI_OFF), pl.BlockSpec((8, HG_WIDTH), lambda b, c: (0, 0))],
        out_specs=pl.BlockSpec((1, cn, HG_WIDTH), lambda b, c: (b, cidx(c), 0)),
        out_shape=jax.ShapeDtypeStruct((bsz, t_all, HG_WIDTH), F32),
        scratch_shapes=[pltpu.VMEM((HG_WIDTH, HG_WIDTH), F32)],
        compiler_params=_params(("parallel", "arbitrary")),
        name="hgrn_scan_bwd" if rev else "hgrn_scan_fwd",
    )(u, u, u, low)


def _unit_inverse(a, rev):
    n = a.shape[0]
    r, c = _iota((n, n), 0), _iota((n, n), 1)
    eye = jnp.where(r == c, 1.0, 0.0)
    d = jnp.where(_shr(r, BASE) == _shr(c, BASE), a, 0.0)
    t = eye - d
    p = d
    k = 1
    while k < BASE // 2:
        p = _dot3(p, p)
        t = _dot3(t, eye + p)
        k *= 2
    size = BASE
    while size < n:
        big = 2 * size
        off = (_shr(r, big) == _shr(c, big)) & (_shr(r, size) != _shr(c, size))
        a_off = jnp.where(off, a, 0.0)
        t = t - _dot3(_dot3(t, a_off), t)
        size = big
    return t


def _gdn_kernel(q_ref, k_ref, v_ref, sm_ref, bias_ref, arow_ref, y_ref, st_ref, *, rev):
    cn = q_ref.shape[1]
    wdt = GD_WIDTH

    @pl.when(pl.program_id(1) == 0)
    def _():
        st_ref[...] = jnp.zeros_like(st_ref)

    hb = _head_blocks(wdt)
    bd = _ones_where(hb)
    lane_head = _shr(_iota((1, wdt), 1), HEAD)
    q, k, v = q_ref[0], k_ref[0], v_ref[0]
    q = q * lax.rsqrt(_dot3_exact_rhs(q * q, bd) + 1e-6) * (HEAD ** -0.5)
    k = k * lax.rsqrt(_dot3_exact_rhs(k * k, bd) + 1e-6)
    sm = sm_ref[0]
    la = SM_A[1] if rev else SM_A[0]
    lb = SM_B[1] if rev else SM_B[0]
    g = arow_ref[...] * _softplus(sm + bias_ref[...])
    beta = jax.nn.sigmoid(sm)
    gam = _cumsum_rows(g, rev)
    gam_r = gam.T
    gam_e = _dot3_exact_rhs(gam, _expand_matrix(la, GD_HEADS, wdt))
    beta_e = _dot3_exact_rhs(beta, _expand_matrix(lb, GD_HEADS, wdt))
    last = 0 if rev else cn - 1
    tot_e = gam_e[last:last + 1]
    egam = jnp.exp(gam_e)
    kb = k * beta_e
    kbf = k.astype(BF16)
    rhs = jnp.concatenate([v * beta_e, kb * egam], axis=1)
    mask = _scan_mask(cn, rev)
    r_i, c_i = _iota((cn, cn), 0), _iota((cn, cn), 1)
    strict = mask & (r_i != c_i)
    u_all = jnp.zeros((cn, wdt), F32)
    w_all = jnp.zeros((cn, wdt), F32)
    qks = []
    for h in range(GD_HEADS):
        ln = la + h
        diff = gam[:, ln:ln + 1] - gam_r[ln:ln + 1, :]
        dec = jnp.where(mask, jnp.exp(jnp.where(mask, diff, 0.0)), 0.0)
        hm = lane_head == h
        kk = _dot_nt(jnp.where(hm, kb, 0.0).astype(BF16), kbf)
        a = jnp.where(strict, kk * dec, 0.0)
        t = _unit_inverse(a, rev)
        sol = _dot3(t, rhs)
        u_all = jnp.where(hm, sol[:, :wdt], u_all)
        w_all = jnp.where(hm, sol[:, wdt:], w_all)
        qks.append((_dot_nt(jnp.where(hm, q, 0.0).astype(BF16), kbf) * dec).astype(BF16))
    st = st_ref[...]
    stb = st.astype(BF16)
    v_new = u_all - _dot(w_all.astype(BF16), stb)
    vnb = v_new.astype(BF16)
    o = _dot((q * egam).astype(BF16), stb)
    for h in range(GD_HEADS):
        o = o + jnp.where(lane_head == h, _dot(qks[h], vnb), 0.0)
    y_ref[0] = o
    kend = k * jnp.exp(tot_e - gam_e)
    st_ref[...] = st * jnp.exp(tot_e) + jnp.where(hb, _dot(kend.T.astype(BF16), vnb), 0.0)


def _gdn_scan(u, dt_bias, a_log, *, rev, n_ctx):
    bsz, t_all, _ = u.shape
    cn = SCAN_CHUNK
    n_all = t_all // cn
    cidx = _chunk_index(rev, n_ctx, n_all)
    di = 1 if rev else 0
    la = SM_A[di]
    bias = jnp.zeros((1, LANES), F32).at[0, la:la + GD_HEADS].set(dt_bias[di])
    arow = jnp.zeros((1, LANES), F32).at[0, la:la + GD_HEADS].set(-jnp.exp(a_log[di].astype(F32)))
    blk = lambda off: pl.BlockSpec((1, cn, GD_WIDTH), lambda b, c: (b, cidx(c), off // GD_WIDTH))
    return pl.pallas_call(
        functools.partial(_gdn_kernel, rev=rev),
        grid=(bsz, n_all),
        in_specs=[blk(GQ_OFF), blk(GK_OFF), blk(GV_OFF),
                  pl.BlockSpec((1, cn, LANES), lambda b, c: (b, cidx(c), SM_OFF // LANES)),
                  pl.BlockSpec((1, LANES), lambda b, c: (0, 0)),
                  pl.BlockSpec((1, LANES), lambda b, c: (0, 0))],
        out_specs=pl.BlockSpec((1, cn, GD_WIDTH), lambda b, c: (b, cidx(c), 0)),
        out_shape=jax.ShapeDtypeStruct((bsz, t_all, GD_WIDTH), F32),
        scratch_shapes=[pltpu.VMEM((GD_WIDTH, GD_WIDTH), F32)],
        compiler_params=_params(("parallel", "arbitrary")),
        name="gdn_scan_bwd" if rev else "gdn_scan_fwd",
    )(u, u, u, u, bias, arow)


def _group_rms(y, group, bd):
    ms = _dot3_exact_rhs(y * y, bd) * (1.0 / group)
    return y * lax.rsqrt(ms + EPS)


def _outproj_kernel(ysf, ysb, yhf, yhb, ygf, ygb, z_ref, hg_ref, gg_ref, h_ref, gm_ref, scf_ref, shf_ref, gf_ref,
                    nrm_ref, wout_ref, nf_ref, rw_ref, rb_ref, sg_ref, su_ref, sd_ref,
                    h2_ref, xf_ref, idx_ref, gate_ref):
    tm = h_ref.shape[1]
    gw = SSD_WIDTH // SSD_GROUPS
    r, c = _iota((SSD_WIDTH, SSD_WIDTH), 0), _iota((SSD_WIDTH, SSD_WIDTH), 1)
    bd_s = _ones_where(_shr(r, gw) == _shr(c, gw))
    bd_h = _ones_where(_head_blocks(HG_WIDTH))
    ys = (ysf[0] + ysb[0]) * _silu(z_ref[0])
    ys = _group_rms(ys, gw, bd_s) * nrm_ref[:, 0:SSD_WIDTH]
    yh = _group_rms(yhf[0] + yhb[0], HEAD, bd_h) * nrm_ref[:, SSD_WIDTH:SSD_WIDTH + HG_WIDTH] * _silu(hg_ref[0])
    yg = _group_rms(ygf[0] + ygb[0], HEAD, bd_h) * nrm_ref[:, SSD_WIDTH + HG_WIDTH:] * _silu(gg_ref[0])
    y = jnp.concatenate([ys, yh, yg], axis=1).astype(BF16)
    h1 = h_ref[0] + gm_ref[0] * _dot(y, wout_ref[...])
    xf = h1 * lax.rsqrt(jnp.mean(h1 * h1, axis=-1, keepdims=True) + EPS) * nf_ref[...]
    xf = xf * (1.0 + scf_ref[0]) + shf_ref[0]
    xb = xf.astype(BF16)
    xf_ref[0] = xb
    hid = _silu(_dot(xb, sg_ref[...])) * _dot(xb, su_ref[...])
    h2_ref[0] = h1 + gf_ref[0] * _dot(hid.astype(BF16), sd_ref[...])
    scores = jax.nn.sigmoid(_dot3(xf, rw_ref[...]))
    lane = _iota((tm, LANES), 1)
    valid = lane < N_EXPERTS
    sel = jnp.where(valid, scores + rb_ref[...], -jnp.inf)
    idx_out = jnp.zeros((tm, LANES), jnp.int32)
    gate_out = jnp.zeros((tm, LANES), F32)
    gsum = jnp.zeros((tm, 1), F32)
    for j in range(TOP_K):
        m = jnp.max(sel, axis=-1, keepdims=True)
        pick = jnp.min(jnp.where(sel == m, lane, LANES), axis=-1, keepdims=True)
        hit = lane == pick
        gv = jnp.sum(jnp.where(hit, scores, 0.0), axis=-1, keepdims=True)
        idx_out = jnp.where(lane == j, pick, idx_out)
        gate_out = jnp.where(lane == j, gv, gate_out)
        gsum = gsum + gv
        sel = jnp.where(hit, -jnp.inf, sel)
    idx_ref[0] = idx_out
    gate_ref[0] = gate_out / gsum * ROUTED_SCALE


def _outproj(ys, yh, yg, u, h, mod, nrm, w_out, norm_ffn, router_wp, router_bp, sg, su, sd, *, ctx):
    bsz, t_all, d = h.shape
    tm = TOKEN_TILE
    ctx_tiles = ctx // tm
    rows = mod.shape[0] // 6

    def mrow(k):
        return pl.BlockSpec((1, 1, d), lambda b, t: (jnp.where(t < ctx_tiles, rows - 1, b) * 6 + k, 0, 0))

    tok = lambda w, j=0: pl.BlockSpec((1, tm, w), lambda b, t: (b, t, j))
    full = lambda a: pl.BlockSpec(a.shape, lambda b, t: (0,) * a.ndim)
    in_specs = [tok(SSD_WIDTH), tok(SSD_WIDTH), tok(HG_WIDTH), tok(HG_WIDTH), tok(GD_WIDTH), tok(GD_WIDTH),
                tok(SSD_WIDTH, Z_OFF // SSD_WIDTH), tok(HG_WIDTH, HGATE_OFF // HG_WIDTH), tok(GD_WIDTH, GGATE_OFF // GD_WIDTH),
                tok(d), mrow(2), mrow(4), mrow(3), mrow(5),
                full(nrm), full(w_out), full(norm_ffn), full(router_wp), full(router_bp), full(sg), full(su), full(sd)]
    out_shape = (jax.ShapeDtypeStruct((bsz, t_all, d), F32), jax.ShapeDtypeStruct((bsz, t_all, d), BF16),
                 jax.ShapeDtypeStruct((bsz, t_all, LANES), jnp.int32), jax.ShapeDtypeStruct((bsz, t_all, LANES), F32))
    out_specs = (tok(d), tok(d), tok(LANES), tok(LANES))
    return pl.pallas_call(
        _outproj_kernel,
        grid=(bsz, t_all // tm),
        in_specs=in_specs,
        out_specs=out_specs,
        out_shape=out_shape,
        compiler_params=_params(("parallel", "parallel")),
        name="out_projection_router",
    )(ys[0], ys[1], yh[0], yh[1], yg[0], yg[1], u, u, u, h, mod, mod, mod, mod,
      nrm, w_out, norm_ffn, router_wp, router_bp, sg, su, sd)


def _expert_kernel(be_ref, x_ref, wg_ref, wu_ref, wd_ref, o_ref):
    x = x_ref[...]
    hid = _silu(_dot(x, wg_ref[0].astype(BF16))) * _dot(x, wu_ref[0].astype(BF16))
    o_ref[...] = _dot(hid.astype(BF16), wd_ref[0].astype(BF16))


def _expert_blocks(xs, block_expert, w_gate, w_up, w_down):
    n_rows, d = xs.shape
    bm = EXPERT_ROWS
    n_blocks = n_rows // bm
    e_dim = w_gate.shape[-1]
    grid_spec = pltpu.PrefetchScalarGridSpec(
        num_scalar_prefetch=1,
        grid=(n_blocks,),
        in_specs=[pl.BlockSpec((bm, d), lambda i, be: (i, 0)),
                  pl.BlockSpec((1, d, e_dim), lambda i, be: (be[i], 0, 0)),
                  pl.BlockSpec((1, d, e_dim), lambda i, be: (be[i], 0, 0)),
                  pl.BlockSpec((1, e_dim, d), lambda i, be: (be[i], 0, 0))],
        out_specs=pl.BlockSpec((bm, d), lambda i, be: (i, 0)),
    )
    return pl.pallas_call(
        _expert_kernel,
        grid_spec=grid_spec,
        out_shape=jax.ShapeDtypeStruct((n_rows, d), F32),
        compiler_params=_params(("arbitrary",)),
        name="routed_experts",
    )(block_expert, xs, w_gate, w_up, w_down)


def _combine_kernel(h_ref, r_ref, gate_ref, gf_ref, o_ref):
    acc = jnp.zeros(h_ref.shape[1:], F32)
    gate = gate_ref[0]
    for j in range(TOP_K):
        acc = acc + r_ref[0, j] * gate[:, j:j + 1]
    o_ref[0] = h_ref[0] + gf_ref[0] * acc


def _combine(h2, routed, gate, mod, *, ctx):
    bsz, t_all, d = h2.shape
    tm = TOKEN_TILE
    ctx_tiles = ctx // tm
    rows = mod.shape[0] // 6
    return pl.pallas_call(
        _combine_kernel,
        grid=(bsz, t_all // tm),
        in_specs=[pl.BlockSpec((1, tm, d), lambda b, t: (b, t, 0)),
                  pl.BlockSpec((1, TOP_K, tm, d), lambda b, t: (b, 0, t, 0)),
                  pl.BlockSpec((1, tm, LANES), lambda b, t: (b, t, 0)),
                  pl.BlockSpec((1, 1, d), lambda b, t: (jnp.where(t < ctx_tiles, rows - 1, b) * 6 + 5, 0, 0))],
        out_specs=pl.BlockSpec((1, tm, d), lambda b, t: (b, t, 0)),
        out_shape=jax.ShapeDtypeStruct((bsz, t_all, d), F32),
        compiler_params=_params(("parallel", "parallel")),
        name="moe_combine",
    )(h2, routed, gate, mod)


def _routed(xf, idx, gate, w_gate, w_up, w_down):
    bsz, t_all, d = xf.shape
    n_tok = bsz * t_all
    n_assign = n_tok * TOP_K
    bm = EXPERT_ROWS
    flat_e = idx.reshape(-1)
    order = jnp.argsort(flat_e)
    e_sorted = flat_e[order]
    token = order // TOP_K
    counts = jnp.bincount(flat_e, length=N_EXPERTS)
    padded = (counts + bm - 1) // bm * bm
    pad_end = jnp.cumsum(padded)
    slot = (pad_end - padded)[e_sorted] + jnp.arange(n_assign) - (jnp.cumsum(counts) - counts)[e_sorted]
    n_blocks = -(-n_assign // bm) + N_EXPERTS
    src = jnp.zeros((n_blocks * bm,), jnp.int32).at[slot].set(token.astype(jnp.int32))
    block_expert = jnp.minimum(
        jnp.searchsorted(pad_end, jnp.arange(n_blocks) * bm, side='right'), N_EXPERTS - 1).astype(jnp.int32)
    xs = jnp.take(xf.reshape(n_tok, d), src, axis=0)
    out = _expert_blocks(xs, block_expert, w_gate, w_up, w_down)
    inv = jnp.zeros((n_assign,), jnp.int32).at[order].set(slot.astype(jnp.int32))
    per = jnp.take(out, inv, axis=0).reshape(bsz, t_all, TOP_K, d)
    return jnp.transpose(per, (0, 2, 1, 3))


def _final_kernel(h_ref, w_ref, o_ref):
    h = h_ref[0]
    o_ref[0] = h * lax.rsqrt(jnp.mean(h * h, axis=-1, keepdims=True) + EPS) * w_ref[...]


def _final_norm(h, w, *, ctx):
    bsz, t_all, d = h.shape
    tm = TOKEN_TILE
    seq = t_all - ctx
    ctx_tiles = ctx // tm
    return pl.pallas_call(
        _final_kernel,
        grid=(bsz, seq // tm),
        in_specs=[pl.BlockSpec((1, tm, d), lambda b, t: (b, t + ctx_tiles, 0)),
                  pl.BlockSpec((1, d), lambda b, t: (0, 0))],
        out_specs=pl.BlockSpec((1, tm, d), lambda b, t: (b, t, 0)),
        out_shape=jax.ShapeDtypeStruct((bsz, seq, d), F32),
        compiler_params=_params(("parallel", "parallel")),
        name="final_norm",
    )(h, w.reshape(1, d))


def kernel(x, c, ctx, c_ctx, w_mod, b_mod, norm_mix, w_in, ssd_conv_w, ssd_conv_b, ssd_dt_bias, ssd_a_log, ssd_d,
           ssd_norm, hgrn_lb, hgrn_norm, gdn_conv_w, gdn_dt_bias, gdn_a_log, gdn_norm, w_out, norm_ffn, router_w,
           router_bias, exp_gate, exp_up, exp_down, sh_gate, sh_up, sh_down, norm_final):
    bsz, seq, d = x.shape
    n_ctx_tok = ctx.shape[1]
    depth = w_in.shape[0]
    assert d == D_MODEL and n_ctx_tok == TOKEN_TILE and seq % TOKEN_TILE == 0 and TOKEN_TILE % SCAN_CHUNK == 0
    assert TOKEN_TILE % GRID_W == 0 and seq % GRID_W == 0
    n_ctx = n_ctx_tok // SCAN_CHUNK

    src = _src_columns()
    valid = jnp.asarray(src >= 0)
    src_c = jnp.asarray(np.maximum(src, 0))
    p_lb = jax.nn.softmax(hgrn_lb.astype(F32), axis=0)
    lower_all = jnp.cumsum(p_lb, axis=0) - p_lb[0]

    rows = -(-(bsz + 1) // 8) * 8
    cond = jnp.zeros((rows, d), F32).at[:bsz].set(c).at[rows - 1].set(c_ctx)

    h = jnp.concatenate([ctx, x], axis=1)
    for l in range(depth):
        mod = _modulation(cond, w_mod[l], b_mod[l]).reshape(rows * 6, 1, d)
        w_p = jnp.where(valid[None, :], jnp.take(w_in[l], src_c, axis=1), 0.0).astype(BF16)
        u = _inproj(h, mod, norm_mix[l], w_p, ssd_conv_w[l], ssd_conv_b[l], gdn_conv_w[l], ctx=n_ctx_tok)
        ys = [_ssd_scan(u, ssd_dt_bias[l], ssd_a_log[l], ssd_d[l], rev=r, n_ctx=n_ctx) for r in (False, True)]
        yh = [_hgrn_scan(u, lower_all[l], rev=r, n_ctx=n_ctx) for r in (False, True)]
        yg = [_gdn_scan(u, gdn_dt_bias[l], gdn_a_log[l], rev=r, n_ctx=n_ctx) for r in (False, True)]
        nrm = jnp.concatenate([ssd_norm[l], jnp.tile(hgrn_norm[l], HG_WIDTH // HEAD),
                               jnp.tile(gdn_norm[l], GD_WIDTH // HEAD)]).astype(F32).reshape(1, d)
        rwp = jnp.zeros((d, LANES), F32).at[:, :N_EXPERTS].set(router_w[l])
        rbp = jnp.zeros((1, LANES), F32).at[0, :N_EXPERTS].set(router_bias[l])
        h2, xf, idx, gate = _outproj(ys, yh, yg, u, h, mod, nrm, w_out[l].astype(BF16), norm_ffn[l].reshape(1, d),
                                     rwp, rbp, sh_gate[l].astype(BF16), sh_up[l].astype(BF16),
                                     sh_down[l].astype(BF16), ctx=n_ctx_tok)
        routed = _routed(xf, idx[..., :TOP_K], gate, exp_gate[l], exp_up[l], exp_down[l])
        h = _combine(h2, routed, gate, mod, ctx=n_ctx_tok)
    return _final_norm(h, norm_final, ctx=n_ctx_tok)
```

```python
import functools

import numpy as np
import jax
import jax.numpy as jnp
from jax import lax
from jax.experimental import pallas as pl
from jax.experimental.pallas import tpu as pltpu

F32 = jnp.float32
BF16 = jnp.bfloat16

D_MODEL = 1024
GRID_W = 64
CONV_W = 5
EPS = 1e-6
MIN_LOWER = 1e-30
HEAD = 64
SSD_HEADS = 8
SSD_WIDTH = 512
SSD_STATE = 128
SSD_GROUPS = 2
HG_WIDTH = 256
GD_WIDTH = 256
GD_HEADS = 4
N_EXPERTS = 64
TOP_K = 8
EXPERT_DIM = 256
ROUTED_SCALE = 2.5

LANES = 128
SCAN_CHUNK = 128
SCAN_BATCH = 2
BASE = 16
TOKEN_TILE = 256
EXPERT_ROWS = 256
VMEM_LIMIT = 56 * 1024 * 1024

Z_OFF, X_OFF, BC_OFF = 0, 512, 1024
HQ_OFF, HFF_OFF, HFB_OFF, HI_OFF, HGATE_OFF = 1536, 1792, 2048, 2304, 2560
GQ_OFF, GK_OFF, GV_OFF, GGATE_OFF = 2816, 3072, 3328, 3584
SM_OFF = 3840
NCOLS = 3968
SSD_CONV = 1024
GDN_CONV = 768
SM_DT = (0, 8)
SM_A = (16, 20)
SM_B = (24, 28)


def _permute_w_in(w):
    d = w.shape[0]
    parts = [w[:, 0:1536],
             w[:, 1552:2832],
             w[:, 2832:3600],
             w[:, 3616:3872],
             w[:, 1536:1552],
             w[:, 3600:3616],
             jnp.zeros((d, LANES - 32), w.dtype)]
    return jnp.concatenate(parts, axis=1)


def _dot(a, b):
    return lax.dot_general(a, b, (((1,), (0,)), ((), ())), preferred_element_type=F32)


def _dot_nt(a, b):
    return lax.dot_general(a, b, (((1,), (1,)), ((), ())), preferred_element_type=F32)


def _split(a):
    hi = a.astype(BF16)
    lo = (a - hi.astype(F32)).astype(BF16)
    return hi, lo


def _dot3(a, b):
    ah, al = _split(a)
    bh, bl = _split(b)
    return _dot(ah, bh) + (_dot(ah, bl) + _dot(al, bh))


def _dot3_exact_rhs(a, b_bf16):
    ah, al = _split(a)
    return _dot(ah, b_bf16) + _dot(al, b_bf16)


def _silu(x):
    return x * jax.nn.sigmoid(x)


def _softplus(x):
    return jnp.maximum(x, 0.0) + jnp.log1p(jnp.exp(-jnp.abs(x)))


def _log_sigmoid(x):
    return jnp.minimum(x, 0.0) - jnp.log1p(jnp.exp(-jnp.abs(x)))


def _params(sem):
    return pltpu.CompilerParams(dimension_semantics=sem, vmem_limit_bytes=VMEM_LIMIT)


def _iota(shape, dim):
    return lax.broadcasted_iota(jnp.int32, shape, dim)


def _ones_where(mask):
    return jnp.where(mask, 1.0, 0.0).astype(BF16)


def _shr(x, div):
    return jnp.right_shift(x, int(np.log2(div)))


def _scan_mask(n, rev):
    r, c = _iota((n, n), 0), _iota((n, n), 1)
    return (c >= r) if rev else (c <= r)


def _expand_matrix(lane0, width):
    r, c = _iota((LANES, width), 0), _iota((LANES, width), 1)
    return _ones_where(r == lane0 + _shr(c, HEAD))


def _head_blocks(n):
    r, c = _iota((n, n), 0), _iota((n, n), 1)
    return _shr(r, HEAD) == _shr(c, HEAD)


def _cumsum_rows(x, rev):
    tri = _ones_where(_scan_mask(x.shape[0], rev))
    xh, xl = _split(x)
    return _dot(tri, xh) + _dot(tri, xl)


def _mod_kernel(s_ref, w_ref, b_ref, o_ref):
    s = _silu(s_ref[...])
    o_ref[...] = _dot(s.astype(BF16), w_ref[...].astype(BF16)) + b_ref[...]


def _modulation(cond, w, b):
    rows, d = cond.shape
    n = w.shape[1]
    bn = d
    return pl.pallas_call(
        _mod_kernel,
        grid=(n // bn,),
        in_specs=[pl.BlockSpec((rows, d), lambda j: (0, 0)),
                  pl.BlockSpec((d, bn), lambda j: (0, j)),
                  pl.BlockSpec((1, bn), lambda j: (0, j))],
        out_specs=pl.BlockSpec((rows, bn), lambda j: (0, j)),
        out_shape=jax.ShapeDtypeStruct((rows, n), F32),
        compiler_params=_params(("parallel",)),
        name="modulation",
    )(cond, w, b.reshape(1, n))


def _inproj_kernel(h_ref, sc_ref, sh_ref, nw_ref, w_ref, cws_ref, cbs_ref, cwg_ref, u_ref, *, ctx_tiles):
    tm = h_ref.shape[1]
    t = pl.program_id(1)
    h = h_ref[0]
    a = h * lax.rsqrt(jnp.mean(h * h, axis=-1, keepdims=True) + EPS) * nw_ref[...]
    a = a * (1.0 + sc_ref[0]) + sh_ref[0]
    ab = a.astype(BF16)
    seg = jnp.where(t < ctx_tiles, tm, GRID_W)
    pos = _iota((tm, 1), 0) & (seg - 1)
    half = CONV_W // 2
    masks = {o: (pos + o >= 0) & (pos + o < seg) for o in range(-half, half + 1) if o}

    def conv(x, cw_ref, c0, wd, cb_ref):
        acc = x * cw_ref[half:half + 1, c0:c0 + wd]
        for j in range(CONV_W):
            o = j - half
            if o == 0:
                continue
            shifted = pltpu.roll(x, (-o) % tm, 0)
            acc = acc + jnp.where(masks[o], shifted, 0.0) * cw_ref[j:j + 1, c0:c0 + wd]
        if cb_ref is not None:
            acc = acc + cb_ref[:, c0:c0 + wd]
        return _silu(acc)

    step = 2 * LANES
    for c0 in range(0, NCOLS, step):
        wd = min(step, NCOLS - c0)
        u = _dot(ab, w_ref[:, c0:c0 + wd])
        if X_OFF <= c0 < X_OFF + SSD_CONV:
            u = conv(u, cws_ref, c0 - X_OFF, wd, cbs_ref)
        elif GQ_OFF <= c0 < GQ_OFF + GDN_CONV:
            u = conv(u, cwg_ref, c0 - GQ_OFF, wd, None)
        u_ref[0, :, c0:c0 + wd] = u


def _mod_row(k, ctx_tiles, rows):
    return lambda b, t: (jnp.where(t < ctx_tiles, rows - 1, b) * 6 + k, 0, 0)


def _inproj(h, mod, norm_w, w_p, conv_s, bias_s, conv_g, *, ctx):
    bsz, t_all, d = h.shape
    tm = TOKEN_TILE
    ctx_tiles = ctx // tm
    rows = mod.shape[0] // 6
    return pl.pallas_call(
        functools.partial(_inproj_kernel, ctx_tiles=ctx_tiles),
        grid=(bsz, t_all // tm),
        in_specs=[pl.BlockSpec((1, tm, d), lambda b, t: (b, t, 0)),
                  pl.BlockSpec((1, 1, d), _mod_row(1, ctx_tiles, rows)),
                  pl.BlockSpec((1, 1, d), _mod_row(0, ctx_tiles, rows)),
                  pl.BlockSpec((1, d), lambda b, t: (0, 0)),
                  pl.BlockSpec((d, NCOLS), lambda b, t: (0, 0)),
                  pl.BlockSpec((CONV_W, SSD_CONV), lambda b, t: (0, 0)),
                  pl.BlockSpec((1, SSD_CONV), lambda b, t: (0, 0)),
                  pl.BlockSpec((CONV_W, GDN_CONV), lambda b, t: (0, 0))],
        out_specs=pl.BlockSpec((1, tm, NCOLS), lambda b, t: (b, t, 0)),
        out_shape=jax.ShapeDtypeStruct((bsz, t_all, NCOLS), F32),
        compiler_params=_params(("parallel", "parallel")),
        name="in_projection",
    )(h, mod, mod, norm_w.reshape(1, d), w_p, conv_s, bias_s.reshape(1, -1), conv_g)


def _chunk_index(rev, n_ctx, n_all):
    if not rev:
        return lambda c: c
    return lambda c: jnp.where(c < n_ctx, n_ctx - 1 - c, n_all + n_ctx - 1 - c)


def _scan_call(body, u, col_blocks, extra, out_width, scratch, *, rev, n_ctx, name):
    bsz, t_all, _ = u.shape
    cn, bb = SCAN_CHUNK, SCAN_BATCH
    assert bsz % bb == 0
    n_all = t_all // cn
    cidx = _chunk_index(rev, n_ctx, n_all)

    def tok(width, off):
        return pl.BlockSpec((bb, cn, width), lambda b, c: (b, cidx(c), off // width))

    in_specs = [tok(w, off) for w, off in col_blocks]
    in_specs += [pl.BlockSpec(a.shape, lambda b, c: (0, 0)) for a in extra]
    return pl.pallas_call(
        body,
        grid=(bsz // bb, n_all),
        in_specs=in_specs,
        out_specs=tok(out_width, 0),
        out_shape=jax.ShapeDtypeStruct((bsz, t_all, out_width), F32),
        scratch_shapes=[scratch],
        compiler_params=_params(("parallel", "arbitrary")),
        name=name,
    )(*([u] * len(col_blocks)), *extra)


def _reset_state(st_ref):
    @pl.when(pl.program_id(1) == 0)
    def _():
        st_ref[...] = jnp.zeros_like(st_ref)


def _ssd_kernel(x_ref, bc_ref, sm_ref, bias_ref, arow_ref, drow_ref, y_ref, st_ref, *, rev):
    bb, cn = x_ref.shape[0], x_ref.shape[1]
    _reset_state(st_ref)
    lane0 = SM_DT[1] if rev else SM_DT[0]
    mask = _scan_mask(cn, rev)
    ex = _expand_matrix(lane0, SSD_WIDTH)
    gw = SSD_WIDTH // SSD_GROUPS
    hpg = SSD_HEADS // SSD_GROUPS
    lane_head = _shr(_iota((1, gw), 1), HEAD)
    last = 0 if rev else cn - 1
    for i in range(bb):
        xs = x_ref[i]
        bc = bc_ref[i]
        dt = _softplus(sm_ref[i] + bias_ref[...])
        da = dt * arow_ref[...]
        acs = _cumsum_rows(da, rev)
        acs_r = acs.T
        acs_e = _dot3_exact_rhs(acs, ex)
        dt_e = _dot3_exact_rhs(dt, ex)
        tot_e = acs_e[last:last + 1]
        xdt = xs * dt_e
        xdt_b = xdt.astype(BF16)
        wst = (xdt * jnp.exp(tot_e - acs_e)).astype(BF16)
        eacs = jnp.exp(acs_e)
        etot = jnp.exp(tot_e)
        for g in range(SSD_GROUPS):
            bm = bc[:, g * SSD_STATE:(g + 1) * SSD_STATE]
            cm = bc[:, (SSD_GROUPS + g) * SSD_STATE:(SSD_GROUPS + g + 1) * SSD_STATE]
            bmb, cmb = bm.astype(BF16), cm.astype(BF16)
            cb = _dot_nt(cmb, bmb)
            xg = xdt_b[:, g * gw:(g + 1) * gw]
            yd = jnp.zeros((cn, gw), F32)
            for r in range(hpg):
                ln = lane0 + g * hpg + r
                diff = acs[:, ln:ln + 1] - acs_r[ln:ln + 1, :]
                dec = jnp.where(mask, jnp.exp(jnp.where(mask, diff, 0.0)), 0.0)
                yh = _dot((cb * dec).astype(BF16), xg)
                yd = jnp.where(lane_head == r, yh, yd)
            st = st_ref[i * SSD_GROUPS + g]
            yo = _dot(cmb, st.astype(BF16)) * eacs[:, g * gw:(g + 1) * gw]
            y = yd + yo
            if drow_ref is not None:
                y = y + drow_ref[:, g * gw:(g + 1) * gw] * xs[:, g * gw:(g + 1) * gw]
            y_ref[i, :, g * gw:(g + 1) * gw] = y
            st_ref[i * SSD_GROUPS + g] = (st * etot[:, g * gw:(g + 1) * gw]
                                          + _dot(bm.T.astype(BF16), wst[:, g * gw:(g + 1) * gw]))


def _ssd_kernel_no_skip(x_ref, bc_ref, sm_ref, bias_ref, arow_ref, y_ref, st_ref, *, rev):
    _ssd_kernel(x_ref, bc_ref, sm_ref, bias_ref, arow_ref, None, y_ref, st_ref, rev=rev)


def _ssd_scan(u, dt_bias, a_log, d_skip, *, rev, n_ctx):
    di = 1 if rev else 0
    lane0 = SM_DT[di]
    bias = jnp.zeros((1, LANES), F32).at[0, lane0:lane0 + SSD_HEADS].set(dt_bias[di])
    arow = jnp.zeros((1, LANES), F32).at[0, lane0:lane0 + SSD_HEADS].set(-jnp.exp(a_log[di].astype(F32)))
    extra = [bias, arow]
    if rev:
        body = functools.partial(_ssd_kernel_no_skip, rev=rev)
    else:
        extra.append(jnp.repeat(d_skip.astype(F32), HEAD).reshape(1, SSD_WIDTH))
        body = functools.partial(_ssd_kernel, rev=rev)
    scratch = pltpu.VMEM((SCAN_BATCH * SSD_GROUPS, SSD_STATE, SSD_WIDTH // SSD_GROUPS), F32)
    return _scan_call(body, u, [(SSD_WIDTH, X_OFF), (512, BC_OFF), (LANES, SM_OFF)], extra, SSD_WIDTH, scratch,
                      rev=rev, n_ctx=n_ctx, name="ssd_scan_bwd" if rev else "ssd_scan_fwd")


def _hgrn_kernel(q_ref, f_ref, i_ref, low_ref, y_ref, st_ref, *, rev):
    bb, cn = q_ref.shape[0], q_ref.shape[1]
    wdt = HG_WIDTH
    heads = wdt // HEAD
    _reset_state(st_ref)
    lower = low_ref[0:1]
    log_lower = low_ref[1:2]
    hb = _head_blocks(wdt)
    bd = _ones_where(hb)
    lane_head = _shr(_iota((1, wdt), 1), HEAD)
    last = 0 if rev else cn - 1

    for i in range(bb):
        fr = f_ref[i]
        qs = _silu(q_ref[i])
        v = i_ref[i]
        a1, a2 = _log_sigmoid(fr), log_lower + _log_sigmoid(-fr)
        logf = jnp.maximum(a1, a2) + jnp.log1p(jnp.exp(-jnp.abs(a1 - a2)))
        kg = (1.0 - lower) * jax.nn.sigmoid(-fr)
        b = _cumsum_rows(logf, rev)
        vb = v.astype(BF16)

        st = st_ref[i]
        y_ref[i] = _dot_nt((qs * jnp.exp(b)).astype(BF16), st.astype(BF16))
        b_last = b[last:last + 1]
        kend = (kg * jnp.exp(b_last - b)).astype(BF16)
        st_ref[i] = st * jnp.exp(b_last) + jnp.where(hb, _dot(v.T.astype(BF16), kend), 0.0)

        def offdiag(t0, t1, s0, s1, r):
            br = b[r:r + 1]
            qp = qs[t0:t1] * jnp.exp(b[t0:t1] - br)
            kp = (kg[s0:s1] * jnp.exp(br - b[s0:s1])).astype(BF16)
            nt = t1 - t0
            qstack = jnp.concatenate([jnp.where(lane_head == h, qp, 0.0) for h in range(heads)], axis=0)
            att = _dot_nt(qstack.astype(BF16), kp)
            res = _dot(att.astype(BF16), vb[s0:s1])
            out = jnp.zeros((nt, wdt), F32)
            for h in range(heads):
                out = jnp.where(lane_head == h, res[h * nt:(h + 1) * nt], out)
            y_ref[i, t0:t1, :] += out

        def diag(t0, t1):
            n = t1 - t0
            bt = b[t0:t1]
            ti, si = _iota((n, n, wdt), 0), _iota((n, n, wdt), 1)
            m3 = (si >= ti) if rev else (si <= ti)
            diff = bt[:, None, :] - bt[None, :, :]
            w = jnp.where(m3, jnp.exp(jnp.where(m3, diff, 0.0)), 0.0)
            p = w * qs[t0:t1][:, None, :] * kg[t0:t1][None, :, :]
            r2 = _dot(p.reshape(n * n, wdt).astype(BF16), bd)
            y_ref[i, t0:t1, :] += jnp.sum(r2.reshape(n, n, wdt) * v[t0:t1][None, :, :], axis=1)

        def block(lo, hi):
            if hi - lo <= BASE:
                diag(lo, hi)
                return
            mid = (lo + hi) // 2
            if rev:
                offdiag(lo, mid, mid, hi, mid)
            else:
                offdiag(mid, hi, lo, mid, mid - 1)
            block(lo, mid)
            block(mid, hi)

        block(0, cn)


def _hgrn_scan(u, lower, *, rev, n_ctx):
    f_off = HFB_OFF if rev else HFF_OFF
    low = jnp.stack([lower, jnp.log(jnp.maximum(lower, MIN_LOWER))]).astype(F32)
    low = jnp.concatenate([low, jnp.zeros((6, HG_WIDTH), F32)], axis=0)
    scratch = pltpu.VMEM((SCAN_BATCH, HG_WIDTH, HG_WIDTH), F32)
    return _scan_call(functools.partial(_hgrn_kernel, rev=rev), u,
                      [(HG_WIDTH, HQ_OFF), (HG_WIDTH, f_off), (HG_WIDTH, HI_OFF)], [low], HG_WIDTH, scratch,
                      rev=rev, n_ctx=n_ctx, name="hgrn_scan_bwd" if rev else "hgrn_scan_fwd")


def _unit_inverse_delta(a):
    n_rows = a.shape[0]
    r, c = _iota((n_rows, n_rows), 0), _iota((n_rows, n_rows), 1)
    d = jnp.where(_shr(r, BASE) == _shr(c, BASE), a, 0.0)
    db = d.astype(BF16)
    p = _dot(db, db)
    n = p - d - _dot(db, p.astype(BF16))
    e = 4
    while e < BASE:
        pb = p.astype(BF16)
        p = _dot(pb, pb)
        n = n + p + _dot(n.astype(BF16), p.astype(BF16))
        e *= 2
    size = BASE
    while size < n_rows:
        big = 2 * size
        off = (_shr(r, big) == _shr(c, big)) & (_shr(r, size) != _shr(c, size))
        a_off = jnp.where(off, a, 0.0)
        nb = n.astype(BF16)
        m = a_off + _dot(nb, a_off.astype(BF16))
        n = n - (m + _dot(m.astype(BF16), nb))
        size = big
    return n


def _gdn_kernel(q_ref, k_ref, v_ref, sm_ref, bias_ref, arow_ref, y_ref, st_ref, *, rev):
    bb, cn = q_ref.shape[0], q_ref.shape[1]
    wdt = GD_WIDTH
    _reset_state(st_ref)
    hb = _head_blocks(wdt)
    bd = _ones_where(hb)
    lane_head = _shr(_iota((1, wdt), 1), HEAD)
    la = SM_A[1] if rev else SM_A[0]
    lb = SM_B[1] if rev else SM_B[0]
    ex_a, ex_b = _expand_matrix(la, wdt), _expand_matrix(lb, wdt)
    mask = _scan_mask(cn, rev)
    strict = mask & (_iota((cn, cn), 0) != _iota((cn, cn), 1))
    last = 0 if rev else cn - 1
    for i in range(bb):
        q, k, v = q_ref[i], k_ref[i], v_ref[i]
        q = q * lax.rsqrt(_dot3_exact_rhs(q * q, bd) + 1e-6) * (HEAD ** -0.5)
        k = k * lax.rsqrt(_dot3_exact_rhs(k * k, bd) + 1e-6)
        sm = sm_ref[i]
        g = arow_ref[...] * _softplus(sm + bias_ref[...])
        beta = jax.nn.sigmoid(sm)
        gam = _cumsum_rows(g, rev)
        gam_r = gam.T
        gam_e = _dot3_exact_rhs(gam, ex_a)
        beta_e = _dot3_exact_rhs(beta, ex_b)
        tot_e = gam_e[last:last + 1]
        egam = jnp.exp(gam_e)
        kb = k * beta_e
        kbf = k.astype(BF16)
        rhs = jnp.concatenate([v * beta_e, kb * egam], axis=1)
        rhs_b = rhs.astype(BF16)
        u_all = jnp.zeros((cn, wdt), F32)
        w_all = jnp.zeros((cn, wdt), F32)
        qks = []
        for h in range(GD_HEADS):
            ln = la + h
            diff = gam[:, ln:ln + 1] - gam_r[ln:ln + 1, :]
            dec = jnp.where(mask, jnp.exp(jnp.where(mask, diff, 0.0)), 0.0)
            hm = lane_head == h
            kk = _dot_nt(jnp.where(hm, kb, 0.0).astype(BF16), kbf)
            a = jnp.where(strict, kk * dec, 0.0)
            n = _unit_inverse_delta(a)
            sol = rhs + _dot(n.astype(BF16), rhs_b)
            u_all = jnp.where(hm, sol[:, :wdt], u_all)
            w_all = jnp.where(hm, sol[:, wdt:], w_all)
            qks.append((_dot_nt(jnp.where(hm, q, 0.0).astype(BF16), kbf) * dec).astype(BF16))
        st = st_ref[i]
        stb = st.astype(BF16)
        v_new = u_all - _dot(w_all.astype(BF16), stb)
        vnb = v_new.astype(BF16)
        o = _dot((q * egam).astype(BF16), stb)
        for h in range(GD_HEADS):
            o = o + jnp.where(lane_head == h, _dot(qks[h], vnb), 0.0)
        y_ref[i] = o
        kend = k * jnp.exp(tot_e - gam_e)
        st_ref[i] = st * jnp.exp(tot_e) + jnp.where(hb, _dot(kend.T.astype(BF16), vnb), 0.0)


def _gdn_scan(u, dt_bias, a_log, *, rev, n_ctx):
    di = 1 if rev else 0
    la = SM_A[di]
    bias = jnp.zeros((1, LANES), F32).at[0, la:la + GD_HEADS].set(dt_bias[di])
    arow = jnp.zeros((1, LANES), F32).at[0, la:la + GD_HEADS].set(-jnp.exp(a_log[di].astype(F32)))
    scratch = pltpu.VMEM((SCAN_BATCH, GD_WIDTH, GD_WIDTH), F32)
    return _scan_call(functools.partial(_gdn_kernel, rev=rev), u,
                      [(GD_WIDTH, GQ_OFF), (GD_WIDTH, GK_OFF), (GD_WIDTH, GV_OFF), (LANES, SM_OFF)], [bias, arow],
                      GD_WIDTH, scratch, rev=rev, n_ctx=n_ctx, name="gdn_scan_bwd" if rev else "gdn_scan_fwd")


def _group_rms(y, group, bd):
    ms = _dot3_exact_rhs(y * y, bd) * (1.0 / group)
    return y * lax.rsqrt(ms + EPS)


def _outproj_kernel(ysf, ysb, yhf, yhb, ygf, ygb, z_ref, hg_ref, gg_ref, h_ref, gm_ref, scf_ref, shf_ref, gf_ref,
                    nrm_ref, wout_ref, nf_ref, rw_ref, rb_ref, sg_ref, su_ref, sd_ref,
                    h2_ref, xf_ref, idx_ref, gate_ref):
    tm = h_ref.shape[1]
    gw = SSD_WIDTH // SSD_GROUPS
    r, c = _iota((SSD_WIDTH, SSD_WIDTH), 0), _iota((SSD_WIDTH, SSD_WIDTH), 1)
    bd_s = _ones_where(_shr(r, gw) == _shr(c, gw))
    bd_h = _ones_where(_head_blocks(HG_WIDTH))
    ys = (ysf[0] + ysb[0]) * _silu(z_ref[0])
    ys = _group_rms(ys, gw, bd_s) * nrm_ref[:, 0:SSD_WIDTH]
    yh = _group_rms(yhf[0] + yhb[0], HEAD, bd_h) * nrm_ref[:, SSD_WIDTH:SSD_WIDTH + HG_WIDTH] * _silu(hg_ref[0])
    yg = _group_rms(ygf[0] + ygb[0], HEAD, bd_h) * nrm_ref[:, SSD_WIDTH + HG_WIDTH:] * _silu(gg_ref[0])
    y = jnp.concatenate([ys, yh, yg], axis=1).astype(BF16)
    h1 = h_ref[0] + gm_ref[0] * _dot(y, wout_ref[...])
    xf = h1 * lax.rsqrt(jnp.mean(h1 * h1, axis=-1, keepdims=True) + EPS) * nf_ref[...]
    xf = xf * (1.0 + scf_ref[0]) + shf_ref[0]
    xb = xf.astype(BF16)
    xf_ref[0] = xb
    hid = _silu(_dot(xb, sg_ref[...])) * _dot(xb, su_ref[...])
    h2_ref[0] = h1 + gf_ref[0] * _dot(hid.astype(BF16), sd_ref[...])
    scores = jax.nn.sigmoid(_dot3(xf, rw_ref[...]))
    lane = _iota((tm, LANES), 1)
    sel = jnp.where(lane < N_EXPERTS, scores + rb_ref[...], -jnp.inf)
    idx_out = jnp.zeros((tm, LANES), jnp.int32)
    gate_out = jnp.zeros((tm, LANES), F32)
    gsum = jnp.zeros((tm, 1), F32)
    for j in range(TOP_K):
        m = jnp.max(sel, axis=-1, keepdims=True)
        pick = jnp.min(jnp.where(sel == m, lane, LANES), axis=-1, keepdims=True)
        hit = lane == pick
        gv = jnp.sum(jnp.where(hit, scores, 0.0), axis=-1, keepdims=True)
        idx_out = jnp.where(lane == j, pick, idx_out)
        gate_out = jnp.where(lane == j, gv, gate_out)
        gsum = gsum + gv
        sel = jnp.where(hit, -jnp.inf, sel)
    idx_ref[0] = idx_out
    gate_ref[0] = gate_out / gsum * ROUTED_SCALE


def _outproj(ys, yh, yg, u, h, mod, nrm, w_out, norm_ffn, router_wp, router_bp, sg, su, sd, *, ctx):
    bsz, t_all, d = h.shape
    tm = TOKEN_TILE
    ctx_tiles = ctx // tm
    rows = mod.shape[0] // 6
    mrow = lambda k: pl.BlockSpec((1, 1, d), _mod_row(k, ctx_tiles, rows))
    tok = lambda w, j=0: pl.BlockSpec((1, tm, w), lambda b, t: (b, t, j))
    full = lambda a: pl.BlockSpec(a.shape, lambda b, t: (0,) * a.ndim)
    in_specs = [tok(SSD_WIDTH), tok(SSD_WIDTH), tok(HG_WIDTH), tok(HG_WIDTH), tok(GD_WIDTH), tok(GD_WIDTH),
                tok(SSD_WIDTH, Z_OFF // SSD_WIDTH), tok(HG_WIDTH, HGATE_OFF // HG_WIDTH),
                tok(GD_WIDTH, GGATE_OFF // GD_WIDTH),
                tok(d), mrow(2), mrow(4), mrow(3), mrow(5),
                full(nrm), full(w_out), full(norm_ffn), full(router_wp), full(router_bp), full(sg), full(su), full(sd)]
    out_shape = (jax.ShapeDtypeStruct((bsz, t_all, d), F32), jax.ShapeDtypeStruct((bsz, t_all, d), BF16),
                 jax.ShapeDtypeStruct((bsz, t_all, LANES), jnp.int32), jax.ShapeDtypeStruct((bsz, t_all, LANES), F32))
    out_specs = (tok(d), tok(d), tok(LANES), tok(LANES))
    return pl.pallas_call(
        _outproj_kernel,
        grid=(bsz, t_all // tm),
        in_specs=in_specs,
        out_specs=out_specs,
        out_shape=out_shape,
        compiler_params=_params(("parallel", "parallel")),
        name="out_projection_router",
    )(ys[0], ys[1], yh[0], yh[1], yg[0], yg[1], u, u, u, h, mod, mod, mod, mod,
      nrm, w_out, norm_ffn, router_wp, router_bp, sg, su, sd)


def _rank_kernel(idx_ref, rank_ref, cnt_ref, base_ref):
    tm = idx_ref.shape[0]

    @pl.when(pl.program_id(0) == 0)
    def _():
        base_ref[...] = jnp.zeros_like(base_ref)

    idx = idx_ref[...]
    lane = _iota((tm, LANES), 1)
    hits = [lane == idx[:, j:j + 1] for j in range(TOP_K)]
    m = jnp.zeros((tm, LANES), F32)
    for hit in hits:
        m = m + jnp.where(hit, 1.0, 0.0)
    before = _ones_where(_iota((tm, tm), 1) < _iota((tm, tm), 0))
    base = base_ref[...]
    val = _dot(before, m.astype(BF16)) + base
    out = jnp.zeros((tm, LANES), jnp.int32)
    for j, hit in enumerate(hits):
        rj = jnp.sum(jnp.where(hit, val, 0.0), axis=-1, keepdims=True)
        out = jnp.where(lane == j, rj.astype(jnp.int32), out)
    rank_ref[...] = out
    total = base + jnp.sum(m, axis=0, keepdims=True)
    base_ref[...] = total
    cnt_ref[...] = total


def _route_ranks(idx2d):
    n_tok = idx2d.shape[0]
    tm = TOKEN_TILE
    return pl.pallas_call(
        _rank_kernel,
        grid=(n_tok // tm,),
        in_specs=[pl.BlockSpec((tm, LANES), lambda i: (i, 0))],
        out_specs=(pl.BlockSpec((tm, LANES), lambda i: (i, 0)), pl.BlockSpec((1, LANES), lambda i: (0, 0))),
        out_shape=(jax.ShapeDtypeStruct((n_tok, LANES), jnp.int32), jax.ShapeDtypeStruct((1, LANES), F32)),
        scratch_shapes=[pltpu.VMEM((1, LANES), F32)],
        compiler_params=_params(("arbitrary",)),
        name="route_ranks",
    )(idx2d)


def _expert_kernel(be_ref, x_ref, wg_ref, wu_ref, wd_ref, o_ref):
    x = x_ref[...]
    hid = _silu(_dot(x, wg_ref[0])) * _dot(x, wu_ref[0])
    o_ref[...] = _dot(hid.astype(BF16), wd_ref[0])


def _expert_blocks(xs, block_expert, w_gate, w_up, w_down):
    n_rows, d = xs.shape
    bm = EXPERT_ROWS
    e_dim = w_gate.shape[-1]
    grid_spec = pltpu.PrefetchScalarGridSpec(
        num_scalar_prefetch=1,
        grid=(n_rows // bm,),
        in_specs=[pl.BlockSpec((bm, d), lambda i, be: (i, 0)),
                  pl.BlockSpec((1, d, e_dim), lambda i, be: (be[i], 0, 0)),
                  pl.BlockSpec((1, d, e_dim), lambda i, be: (be[i], 0, 0)),
                  pl.BlockSpec((1, e_dim, d), lambda i, be: (be[i], 0, 0))],
        out_specs=pl.BlockSpec((bm, d), lambda i, be: (i, 0)),
    )
    return pl.pallas_call(
        _expert_kernel,
        grid_spec=grid_spec,
        out_shape=jax.ShapeDtypeStruct((n_rows, d), F32),
        compiler_params=_params(("arbitrary",)),
        name="routed_experts",
    )(block_expert, xs, w_gate, w_up, w_down)


def _combine_kernel(h_ref, r_ref, gate_ref, gf_ref, o_ref):
    acc = jnp.zeros(h_ref.shape[1:], F32)
    gate = gate_ref[0]
    for j in range(TOP_K):
        acc = acc + r_ref[j] * gate[:, j:j + 1]
    o_ref[0] = h_ref[0] + gf_ref[0] * acc


def _combine(h2, routed, gate, mod, *, ctx):
    bsz, t_all, d = h2.shape
    tm = TOKEN_TILE
    tiles = t_all // tm
    ctx_tiles = ctx // tm
    rows = mod.shape[0] // 6
    return pl.pallas_call(
        _combine_kernel,
        grid=(bsz, tiles),
        in_specs=[pl.BlockSpec((1, tm, d), lambda b, t: (b, t, 0)),
                  pl.BlockSpec((TOP_K, tm, d), lambda b, t: (0, b * tiles + t, 0)),
                  pl.BlockSpec((1, tm, LANES), lambda b, t: (b, t, 0)),
                  pl.BlockSpec((1, 1, d), _mod_row(5, ctx_tiles, rows))],
        out_specs=pl.BlockSpec((1, tm, d), lambda b, t: (b, t, 0)),
        out_shape=jax.ShapeDtypeStruct((bsz, t_all, d), F32),
        compiler_params=_params(("parallel", "parallel")),
        name="moe_combine",
    )(h2, routed, gate, mod)


def _routed(xf, idx, w_gate, w_up, w_down):
    bsz, t_all, d = xf.shape
    n_tok = bsz * t_all
    n_assign = n_tok * TOP_K
    bm = EXPERT_ROWS
    n_blocks = -(-n_assign // bm) + N_EXPERTS
    n_slots = n_blocks * bm
    idx2d = idx.reshape(n_tok, LANES)
    rank, cnt = _route_ranks(idx2d)
    counts = cnt[0, :N_EXPERTS].astype(jnp.int32)
    padded = (counts + bm - 1) // bm * bm
    pad_end = jnp.cumsum(padded)
    offset = pad_end - padded
    first = jnp.cumsum(counts) - counts
    ek = idx2d[:, :TOP_K]
    slot = jnp.take(offset, ek) + rank[:, :TOP_K]
    block_expert = jnp.minimum(
        jnp.sum(pad_end[None, :] <= (jnp.arange(n_blocks, dtype=jnp.int32) * bm)[:, None], axis=1),
        N_EXPERTS - 1).astype(jnp.int32)
    token_sorted = jnp.argsort(ek.reshape(-1)) // TOP_K
    e_slot = jnp.repeat(block_expert, bm)
    r_slot = jnp.arange(n_slots, dtype=jnp.int32) - jnp.take(offset, e_slot)
    live = r_slot < jnp.take(counts, e_slot)
    src = jnp.where(live, jnp.take(token_sorted, jnp.clip(jnp.take(first, e_slot) + r_slot, 0, n_assign - 1)), 0)
    xs = jnp.take(xf.reshape(n_tok, d), src.astype(jnp.int32), axis=0)
    out = _expert_blocks(xs, block_expert, w_gate, w_up, w_down)
    return jnp.take(out, slot.T.reshape(-1), axis=0).reshape(TOP_K, n_tok, d)


def _final_kernel(h_ref, w_ref, o_ref):
    h = h_ref[0]
    o_ref[0] = h * lax.rsqrt(jnp.mean(h * h, axis=-1, keepdims=True) + EPS) * w_ref[...]


def _final_norm(h, w, *, ctx):
    bsz, t_all, d = h.shape
    tm = TOKEN_TILE
    seq = t_all - ctx
    ctx_tiles = ctx // tm
    return pl.pallas_call(
        _final_kernel,
        grid=(bsz, seq // tm),
        in_specs=[pl.BlockSpec((1, tm, d), lambda b, t: (b, t + ctx_tiles, 0)),
                  pl.BlockSpec((1, d), lambda b, t: (0, 0))],
        out_specs=pl.BlockSpec((1, tm, d), lambda b, t: (b, t, 0)),
        out_shape=jax.ShapeDtypeStruct((bsz, seq, d), F32),
        compiler_params=_params(("parallel", "parallel")),
        name="final_norm",
    )(h, w.reshape(1, d))


def kernel(x, c, ctx, c_ctx, w_mod, b_mod, norm_mix, w_in, ssd_conv_w, ssd_conv_b, ssd_dt_bias, ssd_a_log, ssd_d,
           ssd_norm, hgrn_lb, hgrn_norm, gdn_conv_w, gdn_dt_bias, gdn_a_log, gdn_norm, w_out, norm_ffn, router_w,
           router_bias, exp_gate, exp_up, exp_down, sh_gate, sh_up, sh_down, norm_final):
    bsz, seq, d = x.shape
    n_ctx_tok = ctx.shape[1]
    depth = w_in.shape[0]
    assert d == D_MODEL and n_ctx_tok == TOKEN_TILE and seq % TOKEN_TILE == 0 and TOKEN_TILE % SCAN_CHUNK == 0
    assert TOKEN_TILE % GRID_W == 0 and seq % GRID_W == 0
    n_ctx = n_ctx_tok // SCAN_CHUNK

    p_lb = jax.nn.softmax(hgrn_lb.astype(F32), axis=0)
    lower_all = jnp.cumsum(p_lb, axis=0) - p_lb[0]

    rows = -(-(bsz + 1) // 8) * 8
    cond = jnp.zeros((rows, d), F32).at[:bsz].set(c).at[rows - 1].set(c_ctx)

    h = jnp.concatenate([ctx, x], axis=1)
    for l in range(depth):
        mod = _modulation(cond, w_mod[l], b_mod[l]).reshape(rows * 6, 1, d)
        w_p = _permute_w_in(w_in[l]).astype(BF16)
        u = _inproj(h, mod, norm_mix[l], w_p, ssd_conv_w[l], ssd_conv_b[l], gdn_conv_w[l], ctx=n_ctx_tok)
        ys = [_ssd_scan(u, ssd_dt_bias[l], ssd_a_log[l], ssd_d[l], rev=r, n_ctx=n_ctx) for r in (False, True)]
        yh = [_hgrn_scan(u, lower_all[l], rev=r, n_ctx=n_ctx) for r in (False, True)]
        yg = [_gdn_scan(u, gdn_dt_bias[l], gdn_a_log[l], rev=r, n_ctx=n_ctx) for r in (False, True)]
        nrm = jnp.concatenate([ssd_norm[l], jnp.tile(hgrn_norm[l], HG_WIDTH // HEAD),
                               jnp.tile(gdn_norm[l], GD_WIDTH // HEAD)]).astype(F32).reshape(1, d)
        rwp = jnp.pad(router_w[l].astype(F32), ((0, 0), (0, LANES - N_EXPERTS)))
        rbp = jnp.pad(router_bias[l].astype(F32), (0, LANES - N_EXPERTS)).reshape(1, LANES)
        h2, xf, idx, gate = _outproj(ys, yh, yg, u, h, mod, nrm, w_out[l].astype(BF16), norm_ffn[l].reshape(1, d),
                                     rwp, rbp, sh_gate[l].astype(BF16), sh_up[l].astype(BF16),
                                     sh_down[l].astype(BF16), ctx=n_ctx_tok)
        routed = _routed(xf, idx, exp_gate[l].astype(BF16), exp_up[l].astype(BF16), exp_down[l].astype(BF16))
        h = _combine(h2, routed, gate, mod, ctx=n_ctx_tok)
    return _final_norm(h, norm_final, ctx=n_ctx_tok)
```

```python
import functools

import numpy as np
import jax
import jax.numpy as jnp
from jax import lax
from jax.experimental import pallas as pl
from jax.experimental.pallas import tpu as pltpu

F32 = jnp.float32
BF16 = jnp.bfloat16

D_MODEL = 1024
GRID_W = 64
CONV_W = 5
EPS = 1e-6
MIN_LOWER = 1e-30
HEAD = 64
SSD_HEADS = 8
SSD_WIDTH = 512
SSD_STATE = 128
SSD_GROUPS = 2
HG_WIDTH = 256
GD_WIDTH = 256
GD_HEADS = 4
N_EXPERTS = 64
TOP_K = 8
EXPERT_DIM = 256
ROUTED_SCALE = 2.5

LANES = 128
SCAN_CHUNK = 128
SCAN_BATCH = 4
BASE = 16
TOKEN_TILE = 256
EXPERT_ROWS = 512
VMEM_LIMIT = 56 * 1024 * 1024

Z_OFF, X_OFF, BC_OFF = 0, 512, 1024
HQ_OFF, HFF_OFF, HFB_OFF, HI_OFF, HGATE_OFF = 1536, 1792, 2048, 2304, 2560
GQ_OFF, GK_OFF, GV_OFF, GGATE_OFF = 2816, 3072, 3328, 3584
SM_OFF = 3840
NCOLS = 3968
SSD_CONV = 1024
GDN_CONV = 768
SM_DT = (0, 8)
SM_A = (16, 20)
SM_B = (24, 28)


def _permute_w_in(w):
    d = w.shape[0]
    parts = [w[:, 0:1536],
             w[:, 1552:2832],
             w[:, 2832:3600],
             w[:, 3616:3872],
             w[:, 1536:1552],
             w[:, 3600:3616],
             jnp.zeros((d, LANES - 32), w.dtype)]
    return jnp.concatenate(parts, axis=1)


def _dot(a, b):
    return lax.dot_general(a, b, (((1,), (0,)), ((), ())), preferred_element_type=F32)


def _dot_nt(a, b):
    return lax.dot_general(a, b, (((1,), (1,)), ((), ())), preferred_element_type=F32)


def _split(a):
    hi = a.astype(BF16)
    lo = (a - hi.astype(F32)).astype(BF16)
    return hi, lo


def _dot3(a, b):
    ah, al = _split(a)
    bh, bl = _split(b)
    return _dot(ah, bh) + (_dot(ah, bl) + _dot(al, bh))


def _dot3_exact_rhs(a, b_bf16):
    ah, al = _split(a)
    return _dot(ah, b_bf16) + _dot(al, b_bf16)


def _silu(x):
    return x * jax.nn.sigmoid(x)


def _softplus(x):
    return jnp.maximum(x, 0.0) + jnp.log1p(jnp.exp(-jnp.abs(x)))


def _log_sigmoid(x):
    return jnp.minimum(x, 0.0) - jnp.log1p(jnp.exp(-jnp.abs(x)))


def _params(sem):
    return pltpu.CompilerParams(dimension_semantics=sem, vmem_limit_bytes=VMEM_LIMIT)


def _iota(shape, dim):
    return lax.broadcasted_iota(jnp.int32, shape, dim)


def _ones_where(mask):
    return jnp.where(mask, 1.0, 0.0).astype(BF16)


def _shr(x, div):
    return jnp.right_shift(x, int(np.log2(div)))


def _scan_mask(n, rev):
    r, c = _iota((n, n), 0), _iota((n, n), 1)
    return (c >= r) if rev else (c <= r)


def _expand_matrix(lane0, width):
    r, c = _iota((LANES, width), 0), _iota((LANES, width), 1)
    return _ones_where(r == lane0 + _shr(c, HEAD))


def _head_blocks(n):
    r, c = _iota((n, n), 0), _iota((n, n), 1)
    return _shr(r, HEAD) == _shr(c, HEAD)


def _cumsum_rows(x, rev):
    tri = _ones_where(_scan_mask(x.shape[0], rev))
    xh, xl = _split(x)
    return _dot(tri, xh) + _dot(tri, xl)


def _mod_kernel(s_ref, w_ref, b_ref, o_ref):
    s = _silu(s_ref[...])
    o_ref[...] = _dot(s.astype(BF16), w_ref[...].astype(BF16)) + b_ref[...]


def _modulation(cond, w, b):
    rows, d = cond.shape
    n = w.shape[1]
    bn = d
    return pl.pallas_call(
        _mod_kernel,
        grid=(n // bn,),
        in_specs=[pl.BlockSpec((rows, d), lambda j: (0, 0)),
                  pl.BlockSpec((d, bn), lambda j: (0, j)),
                  pl.BlockSpec((1, bn), lambda j: (0, j))],
        out_specs=pl.BlockSpec((rows, bn), lambda j: (0, j)),
        out_shape=jax.ShapeDtypeStruct((rows, n), F32),
        compiler_params=_params(("parallel",)),
        name="modulation",
    )(cond, w, b.reshape(1, n))


def _inproj_kernel(h_ref, sc_ref, sh_ref, nw_ref, w_ref, cws_ref, cbs_ref, cwg_ref, u_ref, *, ctx_tiles):
    tm = h_ref.shape[1]
    t = pl.program_id(1)
    h = h_ref[0]
    a = h * lax.rsqrt(jnp.mean(h * h, axis=-1, keepdims=True) + EPS) * nw_ref[...]
    a = a * (1.0 + sc_ref[0]) + sh_ref[0]
    ab = a.astype(BF16)
    seg = jnp.where(t < ctx_tiles, tm, GRID_W)
    pos = _iota((tm, 1), 0) & (seg - 1)
    half = CONV_W // 2
    masks = {o: (pos + o >= 0) & (pos + o < seg) for o in range(-half, half + 1) if o}

    def conv(x, cw_ref, c0, wd, cb_ref):
        acc = x * cw_ref[half:half + 1, c0:c0 + wd]
        for j in range(CONV_W):
            o = j - half
            if o == 0:
                continue
            shifted = pltpu.roll(x, (-o) % tm, 0)
            acc = acc + jnp.where(masks[o], shifted, 0.0) * cw_ref[j:j + 1, c0:c0 + wd]
        if cb_ref is not None:
            acc = acc + cb_ref[:, c0:c0 + wd]
        return _silu(acc)

    step = 2 * LANES
    for c0 in range(0, NCOLS, step):
        wd = min(step, NCOLS - c0)
        u = _dot(ab, w_ref[:, c0:c0 + wd])
        if X_OFF <= c0 < X_OFF + SSD_CONV:
            u = conv(u, cws_ref, c0 - X_OFF, wd, cbs_ref)
        elif GQ_OFF <= c0 < GQ_OFF + GDN_CONV:
            u = conv(u, cwg_ref, c0 - GQ_OFF, wd, None)
        u_ref[0, :, c0:c0 + wd] = u


def _mod_row(k, ctx_tiles, rows):
    return lambda b, t: (jnp.where(t < ctx_tiles, rows - 1, b) * 6 + k, 0, 0)


def _inproj(h, mod, norm_w, w_p, conv_s, bias_s, conv_g, *, ctx):
    bsz, t_all, d = h.shape
    tm = TOKEN_TILE
    ctx_tiles = ctx // tm
    rows = mod.shape[0] // 6
    return pl.pallas_call(
        functools.partial(_inproj_kernel, ctx_tiles=ctx_tiles),
        grid=(bsz, t_all // tm),
        in_specs=[pl.BlockSpec((1, tm, d), lambda b, t: (b, t, 0)),
                  pl.BlockSpec((1, 1, d), _mod_row(1, ctx_tiles, rows)),
                  pl.BlockSpec((1, 1, d), _mod_row(0, ctx_tiles, rows)),
                  pl.BlockSpec((1, d), lambda b, t: (0, 0)),
                  pl.BlockSpec((d, NCOLS), lambda b, t: (0, 0)),
                  pl.BlockSpec((CONV_W, SSD_CONV), lambda b, t: (0, 0)),
                  pl.BlockSpec((1, SSD_CONV), lambda b, t: (0, 0)),
                  pl.BlockSpec((CONV_W, GDN_CONV), lambda b, t: (0, 0))],
        out_specs=pl.BlockSpec((1, tm, NCOLS), lambda b, t: (b, t, 0)),
        out_shape=jax.ShapeDtypeStruct((bsz, t_all, NCOLS), F32),
        compiler_params=_params(("parallel", "parallel")),
        name="in_projection",
    )(h, mod, mod, norm_w.reshape(1, d), w_p, conv_s, bias_s.reshape(1, -1), conv_g)


def _chunk_index(rev, n_ctx, n_all):
    if not rev:
        return lambda c: c
    return lambda c: jnp.where(c < n_ctx, n_ctx - 1 - c, n_all + n_ctx - 1 - c)


def _scan_call(body, u, col_blocks, extra, out_width, scratch, *, rev, n_ctx, name):
    bsz, t_all, _ = u.shape
    cn, bb = SCAN_CHUNK, SCAN_BATCH
    assert bsz % bb == 0
    n_all = t_all // cn
    cidx = _chunk_index(rev, n_ctx, n_all)

    def tok(width, off):
        return pl.BlockSpec((bb, cn, width), lambda b, c: (b, cidx(c), off // width))

    in_specs = [tok(w, off) for w, off in col_blocks]
    in_specs += [pl.BlockSpec(a.shape, lambda b, c: (0, 0)) for a in extra]
    return pl.pallas_call(
        body,
        grid=(bsz // bb, n_all),
        in_specs=in_specs,
        out_specs=tok(out_width, 0),
        out_shape=jax.ShapeDtypeStruct((bsz, t_all, out_width), F32),
        scratch_shapes=[scratch],
        compiler_params=_params(("parallel", "arbitrary")),
        name=name,
    )(*([u] * len(col_blocks)), *extra)


def _reset_state(st_ref):
    @pl.when(pl.program_id(1) == 0)
    def _():
        st_ref[...] = jnp.zeros_like(st_ref)


def _ssd_kernel(x_ref, bc_ref, sm_ref, bias_ref, arow_ref, drow_ref, y_ref, st_ref, *, rev):
    bb, cn = x_ref.shape[0], x_ref.shape[1]
    _reset_state(st_ref)
    lane0 = SM_DT[1] if rev else SM_DT[0]
    mask = _scan_mask(cn, rev)
    ex = _expand_matrix(lane0, SSD_WIDTH)
    gw = SSD_WIDTH // SSD_GROUPS
    hpg = SSD_HEADS // SSD_GROUPS
    lane_head = _shr(_iota((1, gw), 1), HEAD)
    last = 0 if rev else cn - 1
    for i in range(bb):
        xs = x_ref[i]
        bc = bc_ref[i]
        dt = _softplus(sm_ref[i] + bias_ref[...])
        da = dt * arow_ref[...]
        acs = _cumsum_rows(da, rev)
        acs_r = acs.T
        acs_e = _dot3_exact_rhs(acs, ex)
        dt_e = _dot3_exact_rhs(dt, ex)
        tot_e = acs_e[last:last + 1]
        xdt = xs * dt_e
        xdt_b = xdt.astype(BF16)
        wst = (xdt * jnp.exp(tot_e - acs_e)).astype(BF16)
        eacs = jnp.exp(acs_e)
        etot = jnp.exp(tot_e)
        for g in range(SSD_GROUPS):
            bm = bc[:, g * SSD_STATE:(g + 1) * SSD_STATE]
            cm = bc[:, (SSD_GROUPS + g) * SSD_STATE:(SSD_GROUPS + g + 1) * SSD_STATE]
            bmb, cmb = bm.astype(BF16), cm.astype(BF16)
            cb = _dot_nt(cmb, bmb)
            xg = xdt_b[:, g * gw:(g + 1) * gw]
            yd = jnp.zeros((cn, gw), F32)
            for r in range(hpg):
                ln = lane0 + g * hpg + r
                diff = acs[:, ln:ln + 1] - acs_r[ln:ln + 1, :]
                dec = jnp.where(mask, jnp.exp(jnp.where(mask, diff, 0.0)), 0.0)
                yh = _dot((cb * dec).astype(BF16), xg)
                yd = jnp.where(lane_head == r, yh, yd)
            st = st_ref[i * SSD_GROUPS + g]
            yo = _dot(cmb, st.astype(BF16)) * eacs[:, g * gw:(g + 1) * gw]
            y = yd + yo
            if drow_ref is not None:
                y = y + drow_ref[:, g * gw:(g + 1) * gw] * xs[:, g * gw:(g + 1) * gw]
            y_ref[i, :, g * gw:(g + 1) * gw] = y
            st_ref[i * SSD_GROUPS + g] = (st * etot[:, g * gw:(g + 1) * gw]
                                          + _dot(bm.T.astype(BF16), wst[:, g * gw:(g + 1) * gw]))


def _ssd_kernel_no_skip(x_ref, bc_ref, sm_ref, bias_ref, arow_ref, y_ref, st_ref, *, rev):
    _ssd_kernel(x_ref, bc_ref, sm_ref, bias_ref, arow_ref, None, y_ref, st_ref, rev=rev)


def _ssd_scan(u, dt_bias, a_log, d_skip, *, rev, n_ctx):
    di = 1 if rev else 0
    lane0 = SM_DT[di]
    bias = jnp.zeros((1, LANES), F32).at[0, lane0:lane0 + SSD_HEADS].set(dt_bias[di])
    arow = jnp.zeros((1, LANES), F32).at[0, lane0:lane0 + SSD_HEADS].set(-jnp.exp(a_log[di].astype(F32)))
    extra = [bias, arow]
    if rev:
        body = functools.partial(_ssd_kernel_no_skip, rev=rev)
    else:
        extra.append(jnp.repeat(d_skip.astype(F32), HEAD).reshape(1, SSD_WIDTH))
        body = functools.partial(_ssd_kernel, rev=rev)
    scratch = pltpu.VMEM((SCAN_BATCH * SSD_GROUPS, SSD_STATE, SSD_WIDTH // SSD_GROUPS), F32)
    return _scan_call(body, u, [(SSD_WIDTH, X_OFF), (512, BC_OFF), (LANES, SM_OFF)], extra, SSD_WIDTH, scratch,
                      rev=rev, n_ctx=n_ctx, name="ssd_scan_bwd" if rev else "ssd_scan_fwd")


def _hgrn_kernel(q_ref, f_ref, i_ref, low_ref, y_ref, st_ref, *, rev):
    bb, cn = q_ref.shape[0], q_ref.shape[1]
    wdt = HG_WIDTH
    heads = wdt // HEAD
    _reset_state(st_ref)
    lower = low_ref[0:1]
    log_lower = low_ref[1:2]
    hb = _head_blocks(wdt)
    bd = _ones_where(hb)
    lane_head = _shr(_iota((1, wdt), 1), HEAD)
    last = 0 if rev else cn - 1

    for i in range(bb):
        fr = f_ref[i]
        qs = _silu(q_ref[i])
        v = i_ref[i]
        a1, a2 = _log_sigmoid(fr), log_lower + _log_sigmoid(-fr)
        logf = jnp.maximum(a1, a2) + jnp.log1p(jnp.exp(-jnp.abs(a1 - a2)))
        kg = (1.0 - lower) * jax.nn.sigmoid(-fr)
        b = _cumsum_rows(logf, rev)
        vb = v.astype(BF16)

        st = st_ref[i]
        y_ref[i] = _dot_nt((qs * jnp.exp(b)).astype(BF16), st.astype(BF16))
        b_last = b[last:last + 1]
        kend = (kg * jnp.exp(b_last - b)).astype(BF16)
        st_ref[i] = st * jnp.exp(b_last) + jnp.where(hb, _dot(v.T.astype(BF16), kend), 0.0)

        def offdiag(t0, t1, s0, s1, r):
            br = b[r:r + 1]
            qp = qs[t0:t1] * jnp.exp(b[t0:t1] - br)
            kp = (kg[s0:s1] * jnp.exp(br - b[s0:s1])).astype(BF16)
            nt = t1 - t0
            qstack = jnp.concatenate([jnp.where(lane_head == h, qp, 0.0) for h in range(heads)], axis=0)
            att = _dot_nt(qstack.astype(BF16), kp)
            res = _dot(att.astype(BF16), vb[s0:s1])
            out = jnp.zeros((nt, wdt), F32)
            for h in range(heads):
                out = jnp.where(lane_head == h, res[h * nt:(h + 1) * nt], out)
            y_ref[i, t0:t1, :] += out

        def diag(t0, t1):
            n = t1 - t0
            bt = b[t0:t1]
            ti, si = _iota((n, n, wdt), 0), _iota((n, n, wdt), 1)
            m3 = (si >= ti) if rev else (si <= ti)
            diff = bt[:, None, :] - bt[None, :, :]
            w = jnp.where(m3, jnp.exp(jnp.where(m3, diff, 0.0)), 0.0)
            p = w * qs[t0:t1][:, None, :] * kg[t0:t1][None, :, :]
            r2 = _dot(p.reshape(n * n, wdt).astype(BF16), bd)
            y_ref[i, t0:t1, :] += jnp.sum(r2.reshape(n, n, wdt) * v[t0:t1][None, :, :], axis=1)

        def block(lo, hi):
            if hi - lo <= BASE:
                diag(lo, hi)
                return
            mid = (lo + hi) // 2
            if rev:
                offdiag(lo, mid, mid, hi, mid)
            else:
                offdiag(mid, hi, lo, mid, mid - 1)
            block(lo, mid)
            block(mid, hi)

        block(0, cn)


def _hgrn_scan(u, lower, *, rev, n_ctx):
    f_off = HFB_OFF if rev else HFF_OFF
    low = jnp.stack([lower, jnp.log(jnp.maximum(lower, MIN_LOWER))]).astype(F32)
    low = jnp.concatenate([low, jnp.zeros((6, HG_WIDTH), F32)], axis=0)
    scratch = pltpu.VMEM((SCAN_BATCH, HG_WIDTH, HG_WIDTH), F32)
    return _scan_call(functools.partial(_hgrn_kernel, rev=rev), u,
                      [(HG_WIDTH, HQ_OFF), (HG_WIDTH, f_off), (HG_WIDTH, HI_OFF)], [low], HG_WIDTH, scratch,
                      rev=rev, n_ctx=n_ctx, name="hgrn_scan_bwd" if rev else "hgrn_scan_fwd")


def _unit_inverse_delta(a):
    n_rows = a.shape[0]
    r, c = _iota((n_rows, n_rows), 0), _iota((n_rows, n_rows), 1)
    d = jnp.where(_shr(r, BASE) == _shr(c, BASE), a, 0.0)
    db = d.astype(BF16)
    p = _dot(db, db)
    n = p - d - _dot(db, p.astype(BF16))
    e = 4
    while e < BASE:
        pb = p.astype(BF16)
        p = _dot(pb, pb)
        n = n + p + _dot(n.astype(BF16), p.astype(BF16))
        e *= 2
    size = BASE
    while size < n_rows:
        big = 2 * size
        off = (_shr(r, big) == _shr(c, big)) & (_shr(r, size) != _shr(c, size))
        a_off = jnp.where(off, a, 0.0)
        nb = n.astype(BF16)
        m = a_off + _dot(nb, a_off.astype(BF16))
        n = n - (m + _dot(m.astype(BF16), nb))
        size = big
    return n


def _gdn_kernel(q_ref, k_ref, v_ref, sm_ref, bias_ref, arow_ref, y_ref, st_ref, *, rev):
    bb, cn = q_ref.shape[0], q_ref.shape[1]
    wdt = GD_WIDTH
    _reset_state(st_ref)
    hb = _head_blocks(wdt)
    bd = _ones_where(hb)
    lane_head = _shr(_iota((1, wdt), 1), HEAD)
    la = SM_A[1] if rev else SM_A[0]
    lb = SM_B[1] if rev else SM_B[0]
    ex_a, ex_b = _expand_matrix(la, wdt), _expand_matrix(lb, wdt)
    mask = _scan_mask(cn, rev)
    strict = mask & (_iota((cn, cn), 0) != _iota((cn, cn), 1))
    last = 0 if rev else cn - 1
    for i in range(bb):
        q, k, v = q_ref[i], k_ref[i], v_ref[i]
        q = q * lax.rsqrt(_dot3_exact_rhs(q * q, bd) + 1e-6) * (HEAD ** -0.5)
        k = k * lax.rsqrt(_dot3_exact_rhs(k * k, bd) + 1e-6)
        sm = sm_ref[i]
        g = arow_ref[...] * _softplus(sm + bias_ref[...])
        beta = jax.nn.sigmoid(sm)
        gam = _cumsum_rows(g, rev)
        gam_r = gam.T
        gam_e = _dot3_exact_rhs(gam, ex_a)
        beta_e = _dot3_exact_rhs(beta, ex_b)
        tot_e = gam_e[last:last + 1]
        egam = jnp.exp(gam_e)
        kb = k * beta_e
        kbf = k.astype(BF16)
        rhs = jnp.concatenate([v * beta_e, kb * egam], axis=1)
        rhs_b = rhs.astype(BF16)
        u_all = jnp.zeros((cn, wdt), F32)
        w_all = jnp.zeros((cn, wdt), F32)
        qks = []
        for h in range(GD_HEADS):
            ln = la + h
            diff = gam[:, ln:ln + 1] - gam_r[ln:ln + 1, :]
            dec = jnp.where(mask, jnp.exp(jnp.where(mask, diff, 0.0)), 0.0)
            hm = lane_head == h
            kk = _dot_nt(jnp.where(hm, kb, 0.0).astype(BF16), kbf)
            a = jnp.where(strict, kk * dec, 0.0)
            n = _unit_inverse_delta(a)
            sol = rhs + _dot(n.astype(BF16), rhs_b)
            u_all = jnp.where(hm, sol[:, :wdt], u_all)
            w_all = jnp.where(hm, sol[:, wdt:], w_all)
            qks.append((_dot_nt(jnp.where(hm, q, 0.0).astype(BF16), kbf) * dec).astype(BF16))
        st = st_ref[i]
        stb = st.astype(BF16)
        v_new = u_all - _dot(w_all.astype(BF16), stb)
        vnb = v_new.astype(BF16)
        o = _dot((q * egam).astype(BF16), stb)
        for h in range(GD_HEADS):
            o = o + jnp.where(lane_head == h, _dot(qks[h], vnb), 0.0)
        y_ref[i] = o
        kend = k * jnp.exp(tot_e - gam_e)
        st_ref[i] = st * jnp.exp(tot_e) + jnp.where(hb, _dot(kend.T.astype(BF16), vnb), 0.0)


def _gdn_scan(u, dt_bias, a_log, *, rev, n_ctx):
    di = 1 if rev else 0
    la = SM_A[di]
    bias = jnp.zeros((1, LANES), F32).at[0, la:la + GD_HEADS].set(dt_bias[di])
    arow = jnp.zeros((1, LANES), F32).at[0, la:la + GD_HEADS].set(-jnp.exp(a_log[di].astype(F32)))
    scratch = pltpu.VMEM((SCAN_BATCH, GD_WIDTH, GD_WIDTH), F32)
    return _scan_call(functools.partial(_gdn_kernel, rev=rev), u,
                      [(GD_WIDTH, GQ_OFF), (GD_WIDTH, GK_OFF), (GD_WIDTH, GV_OFF), (LANES, SM_OFF)], [bias, arow],
                      GD_WIDTH, scratch, rev=rev, n_ctx=n_ctx, name="gdn_scan_bwd" if rev else "gdn_scan_fwd")


def _group_rms(y, group, bd):
    ms = _dot3_exact_rhs(y * y, bd) * (1.0 / group)
    return y * lax.rsqrt(ms + EPS)


def _outproj_kernel(ysf, ysb, yhf, yhb, ygf, ygb, z_ref, hg_ref, gg_ref, h_ref, gm_ref, scf_ref, shf_ref, gf_ref,
                    nrm_ref, wout_ref, nf_ref, rw_ref, rb_ref, sg_ref, su_ref, sd_ref,
                    h2_ref, xf_ref, idx_ref, gate_ref):
    tm = h_ref.shape[1]
    gw = SSD_WIDTH // SSD_GROUPS
    r, c = _iota((SSD_WIDTH, SSD_WIDTH), 0), _iota((SSD_WIDTH, SSD_WIDTH), 1)
    bd_s = _ones_where(_shr(r, gw) == _shr(c, gw))
    bd_h = _ones_where(_head_blocks(HG_WIDTH))
    ys = (ysf[0] + ysb[0]) * _silu(z_ref[0])
    ys = _group_rms(ys, gw, bd_s) * nrm_ref[:, 0:SSD_WIDTH]
    yh = _group_rms(yhf[0] + yhb[0], HEAD, bd_h) * nrm_ref[:, SSD_WIDTH:SSD_WIDTH + HG_WIDTH] * _silu(hg_ref[0])
    yg = _group_rms(ygf[0] + ygb[0], HEAD, bd_h) * nrm_ref[:, SSD_WIDTH + HG_WIDTH:] * _silu(gg_ref[0])
    y = jnp.concatenate([ys, yh, yg], axis=1).astype(BF16)
    h1 = h_ref[0] + gm_ref[0] * _dot(y, wout_ref[...])
    xf = h1 * lax.rsqrt(jnp.mean(h1 * h1, axis=-1, keepdims=True) + EPS) * nf_ref[...]
    xf = xf * (1.0 + scf_ref[0]) + shf_ref[0]
    xb = xf.astype(BF16)
    xf_ref[0] = xb
    hid = _silu(_dot(xb, sg_ref[...])) * _dot(xb, su_ref[...])
    h2_ref[0] = h1 + gf_ref[0] * _dot(hid.astype(BF16), sd_ref[...])
    scores = jax.nn.sigmoid(_dot3(xf, rw_ref[...]))
    lane = _iota((tm, LANES), 1)
    sel = jnp.where(lane < N_EXPERTS, scores + rb_ref[...], -jnp.inf)
    idx_out = jnp.zeros((tm, LANES), jnp.int32)
    gate_out = jnp.zeros((tm, LANES), F32)
    gsum = jnp.zeros((tm, 1), F32)
    for j in range(TOP_K):
        m = jnp.max(sel, axis=-1, keepdims=True)
        pick = jnp.min(jnp.where(sel == m, lane, LANES), axis=-1, keepdims=True)
        hit = lane == pick
        gv = jnp.sum(jnp.where(hit, scores, 0.0), axis=-1, keepdims=True)
        idx_out = jnp.where(lane == j, pick, idx_out)
        gate_out = jnp.where(lane == j, gv, gate_out)
        gsum = gsum + gv
        sel = jnp.where(hit, -jnp.inf, sel)
    idx_ref[0] = idx_out
    gate_ref[0] = gate_out / gsum * ROUTED_SCALE


def _outproj(ys, yh, yg, u, h, mod, nrm, w_out, norm_ffn, router_wp, router_bp, sg, su, sd, *, ctx):
    bsz, t_all, d = h.shape
    tm = TOKEN_TILE
    ctx_tiles = ctx // tm
    rows = mod.shape[0] // 6
    mrow = lambda k: pl.BlockSpec((1, 1, d), _mod_row(k, ctx_tiles, rows))
    tok = lambda w, j=0: pl.BlockSpec((1, tm, w), lambda b, t: (b, t, j))
    full = lambda a: pl.BlockSpec(a.shape, lambda b, t: (0,) * a.ndim)
    in_specs = [tok(SSD_WIDTH), tok(SSD_WIDTH), tok(HG_WIDTH), tok(HG_WIDTH), tok(GD_WIDTH), tok(GD_WIDTH),
                tok(SSD_WIDTH, Z_OFF // SSD_WIDTH), tok(HG_WIDTH, HGATE_OFF // HG_WIDTH),
                tok(GD_WIDTH, GGATE_OFF // GD_WIDTH),
                tok(d), mrow(2), mrow(4), mrow(3), mrow(5),
                full(nrm), full(w_out), full(norm_ffn), full(router_wp), full(router_bp), full(sg), full(su), full(sd)]
    out_shape = (jax.ShapeDtypeStruct((bsz, t_all, d), F32), jax.ShapeDtypeStruct((bsz, t_all, d), BF16),
                 jax.ShapeDtypeStruct((bsz, t_all, LANES), jnp.int32), jax.ShapeDtypeStruct((bsz, t_all, LANES), F32))
    out_specs = (tok(d), tok(d), tok(LANES), tok(LANES))
    return pl.pallas_call(
        _outproj_kernel,
        grid=(bsz, t_all // tm),
        in_specs=in_specs,
        out_specs=out_specs,
        out_shape=out_shape,
        compiler_params=_params(("parallel", "parallel")),
        name="out_projection_router",
    )(ys[0], ys[1], yh[0], yh[1], yg[0], yg[1], u, u, u, h, mod, mod, mod, mod,
      nrm, w_out, norm_ffn, router_wp, router_bp, sg, su, sd)


def _rank_kernel(idx_ref, rank_ref, cnt_ref, base_ref):
    tm = idx_ref.shape[0]

    @pl.when(pl.program_id(0) == 0)
    def _():
        base_ref[...] = jnp.zeros_like(base_ref)

    idx = idx_ref[...]
    lane = _iota((tm, LANES), 1)
    hits = [lane == idx[:, j:j + 1] for j in range(TOP_K)]
    m = jnp.zeros((tm, LANES), F32)
    for hit in hits:
        m = m + jnp.where(hit, 1.0, 0.0)
    before = _ones_where(_iota((tm, tm), 1) < _iota((tm, tm), 0))
    base = base_ref[...]
    val = _dot(before, m.astype(BF16)) + base
    out = jnp.zeros((tm, LANES), jnp.int32)
    for j, hit in enumerate(hits):
        rj = jnp.sum(jnp.where(hit, val, 0.0), axis=-1, keepdims=True)
        out = jnp.where(lane == j, rj.astype(jnp.int32), out)
    rank_ref[...] = out
    total = base + jnp.sum(m, axis=0, keepdims=True)
    base_ref[...] = total
    cnt_ref[...] = total


def _route_ranks(idx2d):
    n_tok = idx2d.shape[0]
    tm = TOKEN_TILE
    return pl.pallas_call(
        _rank_kernel,
        grid=(n_tok // tm,),
        in_specs=[pl.BlockSpec((tm, LANES), lambda i: (i, 0))],
        out_specs=(pl.BlockSpec((tm, LANES), lambda i: (i, 0)), pl.BlockSpec((1, LANES), lambda i: (0, 0))),
        out_shape=(jax.ShapeDtypeStruct((n_tok, LANES), jnp.int32), jax.ShapeDtypeStruct((1, LANES), F32)),
        scratch_shapes=[pltpu.VMEM((1, LANES), F32)],
        compiler_params=_params(("arbitrary",)),
        name="route_ranks",
    )(idx2d)


def _expert_kernel(be_ref, x_ref, wg_ref, wu_ref, wd_ref, o_ref):
    x = x_ref[...]
    hid = _silu(_dot(x, wg_ref[0])) * _dot(x, wu_ref[0])
    o_ref[...] = _dot(hid.astype(BF16), wd_ref[0])


def _expert_blocks(xs, block_expert, w_gate, w_up, w_down):
    n_rows, d = xs.shape
    bm = EXPERT_ROWS
    e_dim = w_gate.shape[-1]
    grid_spec = pltpu.PrefetchScalarGridSpec(
        num_scalar_prefetch=1,
        grid=(n_rows // bm,),
        in_specs=[pl.BlockSpec((bm, d), lambda i, be: (i, 0)),
                  pl.BlockSpec((1, d, e_dim), lambda i, be: (be[i], 0, 0)),
                  pl.BlockSpec((1, d, e_dim), lambda i, be: (be[i], 0, 0)),
                  pl.BlockSpec((1, e_dim, d), lambda i, be: (be[i], 0, 0))],
        out_specs=pl.BlockSpec((bm, d), lambda i, be: (i, 0)),
    )
    return pl.pallas_call(
        _expert_kernel,
        grid_spec=grid_spec,
        out_shape=jax.ShapeDtypeStruct((n_rows, d), F32),
        compiler_params=_params(("arbitrary",)),
        name="routed_experts",
    )(block_expert, xs, w_gate, w_up, w_down)


def _combine_kernel(h_ref, r_ref, gate_ref, gf_ref, o_ref):
    acc = jnp.zeros(h_ref.shape[1:], F32)
    gate = gate_ref[0]
    for j in range(TOP_K):
        acc = acc + r_ref[j] * gate[:, j:j + 1]
    o_ref[0] = h_ref[0] + gf_ref[0] * acc


def _combine(h2, routed, gate, mod, *, ctx):
    bsz, t_all, d = h2.shape
    tm = TOKEN_TILE
    tiles = t_all // tm
    ctx_tiles = ctx // tm
    rows = mod.shape[0] // 6
    return pl.pallas_call(
        _combine_kernel,
        grid=(bsz, tiles),
        in_specs=[pl.BlockSpec((1, tm, d), lambda b, t: (b, t, 0)),
                  pl.BlockSpec((TOP_K, tm, d), lambda b, t: (0, b * tiles + t, 0)),
                  pl.BlockSpec((1, tm, LANES), lambda b, t: (b, t, 0)),
                  pl.BlockSpec((1, 1, d), _mod_row(5, ctx_tiles, rows))],
        out_specs=pl.BlockSpec((1, tm, d), lambda b, t: (b, t, 0)),
        out_shape=jax.ShapeDtypeStruct((bsz, t_all, d), F32),
        compiler_params=_params(("parallel", "parallel")),
        name="moe_combine",
    )(h2, routed, gate, mod)


def _routed(xf, idx, w_gate, w_up, w_down):
    bsz, t_all, d = xf.shape
    n_tok = bsz * t_all
    n_assign = n_tok * TOP_K
    bm = EXPERT_ROWS
    n_blocks = -(-n_assign // bm) + N_EXPERTS
    n_slots = n_blocks * bm
    idx2d = idx.reshape(n_tok, LANES)
    rank, cnt = _route_ranks(idx2d)
    counts = cnt[0, :N_EXPERTS].astype(jnp.int32)
    padded = (counts + bm - 1) // bm * bm
    pad_end = jnp.cumsum(padded)
    offset = pad_end - padded
    first = jnp.cumsum(counts) - counts
    ek = idx2d[:, :TOP_K]
    slot = jnp.take(offset, ek) + rank[:, :TOP_K]
    block_expert = jnp.minimum(
        jnp.sum(pad_end[None, :] <= (jnp.arange(n_blocks, dtype=jnp.int32) * bm)[:, None], axis=1),
        N_EXPERTS - 1).astype(jnp.int32)
    token_sorted = jnp.argsort(ek.reshape(-1)) // TOP_K
    e_slot = jnp.repeat(block_expert, bm)
    r_slot = jnp.arange(n_slots, dtype=jnp.int32) - jnp.take(offset, e_slot)
    live = r_slot < jnp.take(counts, e_slot)
    src = jnp.where(live, jnp.take(token_sorted, jnp.clip(jnp.take(first, e_slot) + r_slot, 0, n_assign - 1)), 0)
    xs = xf.reshape(n_tok, d).at[src.astype(jnp.int32)].get(mode="promise_in_bounds")
    out = _expert_blocks(xs, block_expert, w_gate, w_up, w_down)
    return out.at[slot.T.reshape(-1)].get(mode="promise_in_bounds").reshape(TOP_K, n_tok, d)


def _final_kernel(h_ref, w_ref, o_ref):
    h = h_ref[0]
    o_ref[0] = h * lax.rsqrt(jnp.mean(h * h, axis=-1, keepdims=True) + EPS) * w_ref[...]


def _final_norm(h, w, *, ctx):
    bsz, t_all, d = h.shape
    tm = TOKEN_TILE
    seq = t_all - ctx
    ctx_tiles = ctx // tm
    return pl.pallas_call(
        _final_kernel,
        grid=(bsz, seq // tm),
        in_specs=[pl.BlockSpec((1, tm, d), lambda b, t: (b, t + ctx_tiles, 0)),
                  pl.BlockSpec((1, d), lambda b, t: (0, 0))],
        out_specs=pl.BlockSpec((1, tm, d), lambda b, t: (b, t, 0)),
        out_shape=jax.ShapeDtypeStruct((bsz, seq, d), F32),
        compiler_params=_params(("parallel", "parallel")),
        name="final_norm",
    )(h, w.reshape(1, d))


def kernel(x, c, ctx, c_ctx, w_mod, b_mod, norm_mix, w_in, ssd_conv_w, ssd_conv_b, ssd_dt_bias, ssd_a_log, ssd_d,
           ssd_norm, hgrn_lb, hgrn_norm, gdn_conv_w, gdn_dt_bias, gdn_a_log, gdn_norm, w_out, norm_ffn, router_w,
           router_bias, exp_gate, exp_up, exp_down, sh_gate, sh_up, sh_down, norm_final):
    bsz, seq, d = x.shape
    n_ctx_tok = ctx.shape[1]
    depth = w_in.shape[0]
    assert d == D_MODEL and n_ctx_tok == TOKEN_TILE and seq % TOKEN_TILE == 0 and TOKEN_TILE % SCAN_CHUNK == 0
    assert TOKEN_TILE % GRID_W == 0 and seq % GRID_W == 0
    n_ctx = n_ctx_tok // SCAN_CHUNK

    p_lb = jax.nn.softmax(hgrn_lb.astype(F32), axis=0)
    lower_all = jnp.cumsum(p_lb, axis=0) - p_lb[0]

    rows = -(-(bsz + 1) // 8) * 8
    cond = jnp.zeros((rows, d), F32).at[:bsz].set(c).at[rows - 1].set(c_ctx)

    h = jnp.concatenate([ctx, x], axis=1)
    for l in range(depth):
        mod = _modulation(cond, w_mod[l], b_mod[l]).reshape(rows * 6, 1, d)
        w_p = _permute_w_in(w_in[l]).astype(BF16)
        u = _inproj(h, mod, norm_mix[l], w_p, ssd_conv_w[l], ssd_conv_b[l], gdn_conv_w[l], ctx=n_ctx_tok)
        ys = [_ssd_scan(u, ssd_dt_bias[l], ssd_a_log[l], ssd_d[l], rev=r, n_ctx=n_ctx) for r in (False, True)]
        yh = [_hgrn_scan(u, lower_all[l], rev=r, n_ctx=n_ctx) for r in (False, True)]
        yg = [_gdn_scan(u, gdn_dt_bias[l], gdn_a_log[l], rev=r, n_ctx=n_ctx) for r in (False, True)]
        nrm = jnp.concatenate([ssd_norm[l], jnp.tile(hgrn_norm[l], HG_WIDTH // HEAD),
                               jnp.tile(gdn_norm[l], GD_WIDTH // HEAD)]).astype(F32).reshape(1, d)
        rwp = jnp.pad(router_w[l].astype(F32), ((0, 0), (0, LANES - N_EXPERTS)))
        rbp = jnp.pad(router_bias[l].astype(F32), (0, LANES - N_EXPERTS)).reshape(1, LANES)
        h2, xf, idx, gate = _outproj(ys, yh, yg, u, h, mod, nrm, w_out[l].astype(BF16), norm_ffn[l].reshape(1, d),
                                     rwp, rbp, sh_gate[l].astype(BF16), sh_up[l].astype(BF16),
                                     sh_down[l].astype(BF16), ctx=n_ctx_tok)
        routed = _routed(xf, idx, exp_gate[l].astype(BF16), exp_up[l].astype(BF16), exp_down[l].astype(BF16))
        h = _combine(h2, routed, gate, mod, ctx=n_ctx_tok)
    return _final_norm(h, norm_final, ctx=n_ctx_tok)
```

```python
import functools

import numpy as np
import jax
import jax.numpy as jnp
from jax import lax
from jax.experimental import pallas as pl
from jax.experimental.pallas import tpu as pltpu
from jax.experimental.pallas import tpu_sc as plsc

F32 = jnp.float32
BF16 = jnp.bfloat16

D_MODEL = 1024
GRID_W = 64
CONV_W = 5
EPS = 1e-6
MIN_LOWER = 1e-30
HEAD = 64
SSD_HEADS = 8
SSD_WIDTH = 512
SSD_STATE = 128
SSD_GROUPS = 2
HG_WIDTH = 256
GD_WIDTH = 256
GD_HEADS = 4
N_EXPERTS = 64
TOP_K = 8
EXPERT_DIM = 256
ROUTED_SCALE = 2.5

LANES = 128
SCAN_CHUNK = 128
SCAN_BATCH = 4
BASE = 16
TOKEN_TILE = 256
EXPERT_ROWS = 512
GATHER_WINDOW = 128
VMEM_LIMIT = 56 * 1024 * 1024

Z_OFF, X_OFF, BC_OFF = 0, 512, 1024
HQ_OFF, HFF_OFF, HFB_OFF, HI_OFF, HGATE_OFF = 1536, 1792, 2048, 2304, 2560
GQ_OFF, GK_OFF, GV_OFF, GGATE_OFF = 2816, 3072, 3328, 3584
SM_OFF = 3840
NCOLS = 3968
SSD_CONV = 1024
GDN_CONV = 768
SM_DT = (0, 8)
SM_A = (16, 20)
SM_B = (24, 28)


def _permute_w_in(w):
    d = w.shape[0]
    parts = [w[:, 0:1536],
             w[:, 1552:2832],
             w[:, 2832:3600],
             w[:, 3616:3872],
             w[:, 1536:1552],
             w[:, 3600:3616],
             jnp.zeros((d, LANES - 32), w.dtype)]
    return jnp.concatenate(parts, axis=1)


def _dot(a, b):
    return lax.dot_general(a, b, (((1,), (0,)), ((), ())), preferred_element_type=F32)


def _dot_nt(a, b):
    return lax.dot_general(a, b, (((1,), (1,)), ((), ())), preferred_element_type=F32)


def _split(a):
    hi = a.astype(BF16)
    lo = (a - hi.astype(F32)).astype(BF16)
    return hi, lo


def _dot3(a, b):
    ah, al = _split(a)
    bh, bl = _split(b)
    return _dot(ah, bh) + (_dot(ah, bl) + _dot(al, bh))


def _dot3_exact_rhs(a, b_bf16):
    ah, al = _split(a)
    return _dot(ah, b_bf16) + _dot(al, b_bf16)


def _silu(x):
    return x * jax.nn.sigmoid(x)


def _softplus(x):
    return jnp.maximum(x, 0.0) + jnp.log1p(jnp.exp(-jnp.abs(x)))


def _log_sigmoid(x):
    return jnp.minimum(x, 0.0) - jnp.log1p(jnp.exp(-jnp.abs(x)))


def _params(sem):
    return pltpu.CompilerParams(dimension_semantics=sem, vmem_limit_bytes=VMEM_LIMIT)


def _iota(shape, dim):
    return lax.broadcasted_iota(jnp.int32, shape, dim)


def _ones_where(mask):
    return jnp.where(mask, 1.0, 0.0).astype(BF16)


def _shr(x, div):
    return jnp.right_shift(x, int(np.log2(div)))


def _scan_mask(n, rev):
    r, c = _iota((n, n), 0), _iota((n, n), 1)
    return (c >= r) if rev else (c <= r)


def _expand_matrix(lane0, width):
    r, c = _iota((LANES, width), 0), _iota((LANES, width), 1)
    return _ones_where(r == lane0 + _shr(c, HEAD))


def _head_blocks(n):
    r, c = _iota((n, n), 0), _iota((n, n), 1)
    return _shr(r, HEAD) == _shr(c, HEAD)


def _cumsum_rows(x, rev):
    tri = _ones_where(_scan_mask(x.shape[0], rev))
    xh, xl = _split(x)
    return _dot(tri, xh) + _dot(tri, xl)


def _mod_kernel(s_ref, w_ref, b_ref, o_ref):
    s = _silu(s_ref[...])
    o_ref[...] = _dot(s.astype(BF16), w_ref[...].astype(BF16)) + b_ref[...]


def _modulation(cond, w, b):
    rows, d = cond.shape
    n = w.shape[1]
    bn = d
    return pl.pallas_call(
        _mod_kernel,
        grid=(n // bn,),
        in_specs=[pl.BlockSpec((rows, d), lambda j: (0, 0)),
                  pl.BlockSpec((d, bn), lambda j: (0, j)),
                  pl.BlockSpec((1, bn), lambda j: (0, j))],
        out_specs=pl.BlockSpec((rows, bn), lambda j: (0, j)),
        out_shape=jax.ShapeDtypeStruct((rows, n), F32),
        compiler_params=_params(("parallel",)),
        name="modulation",
    )(cond, w, b.reshape(1, n))


def _inproj_kernel(h_ref, sc_ref, sh_ref, nw_ref, w_ref, cws_ref, cbs_ref, cwg_ref, u_ref, *, ctx_tiles):
    tm = h_ref.shape[1]
    t = pl.program_id(1)
    h = h_ref[0]
    a = h * lax.rsqrt(jnp.mean(h * h, axis=-1, keepdims=True) + EPS) * nw_ref[...]
    a = a * (1.0 + sc_ref[0]) + sh_ref[0]
    ab = a.astype(BF16)
    seg = jnp.where(t < ctx_tiles, tm, GRID_W)
    pos = _iota((tm, 1), 0) & (seg - 1)
    half = CONV_W // 2
    masks = {o: (pos + o >= 0) & (pos + o < seg) for o in range(-half, half + 1) if o}

    def conv(x, cw_ref, c0, wd, cb_ref):
        acc = x * cw_ref[half:half + 1, c0:c0 + wd]
        for j in range(CONV_W):
            o = j - half
            if o == 0:
                continue
            shifted = pltpu.roll(x, (-o) % tm, 0)
            acc = acc + jnp.where(masks[o], shifted, 0.0) * cw_ref[j:j + 1, c0:c0 + wd]
        if cb_ref is not None:
            acc = acc + cb_ref[:, c0:c0 + wd]
        return _silu(acc)

    step = 2 * LANES
    for c0 in range(0, NCOLS, step):
        wd = min(step, NCOLS - c0)
        u = _dot(ab, w_ref[:, c0:c0 + wd])
        if X_OFF <= c0 < X_OFF + SSD_CONV:
            u = conv(u, cws_ref, c0 - X_OFF, wd, cbs_ref)
        elif GQ_OFF <= c0 < GQ_OFF + GDN_CONV:
            u = conv(u, cwg_ref, c0 - GQ_OFF, wd, None)
        u_ref[0, :, c0:c0 + wd] = u


def _mod_row(k, ctx_tiles, rows):
    return lambda b, t: (jnp.where(t < ctx_tiles, rows - 1, b) * 6 + k, 0, 0)


def _inproj(h, mod, norm_w, w_p, conv_s, bias_s, conv_g, *, ctx):
    bsz, t_all, d = h.shape
    tm = TOKEN_TILE
    ctx_tiles = ctx // tm
    rows = mod.shape[0] // 6
    return pl.pallas_call(
        functools.partial(_inproj_kernel, ctx_tiles=ctx_tiles),
        grid=(bsz, t_all // tm),
        in_specs=[pl.BlockSpec((1, tm, d), lambda b, t: (b, t, 0)),
                  pl.BlockSpec((1, 1, d), _mod_row(1, ctx_tiles, rows)),
                  pl.BlockSpec((1, 1, d), _mod_row(0, ctx_tiles, rows)),
                  pl.BlockSpec((1, d), lambda b, t: (0, 0)),
                  pl.BlockSpec((d, NCOLS), lambda b, t: (0, 0)),
                  pl.BlockSpec((CONV_W, SSD_CONV), lambda b, t: (0, 0)),
                  pl.BlockSpec((1, SSD_CONV), lambda b, t: (0, 0)),
                  pl.BlockSpec((CONV_W, GDN_CONV), lambda b, t: (0, 0))],
        out_specs=pl.BlockSpec((1, tm, NCOLS), lambda b, t: (b, t, 0)),
        out_shape=jax.ShapeDtypeStruct((bsz, t_all, NCOLS), F32),
        compiler_params=_params(("parallel", "parallel")),
        name="in_projection",
    )(h, mod, mod, norm_w.reshape(1, d), w_p, conv_s, bias_s.reshape(1, -1), conv_g)


def _chunk_index(rev, n_ctx, n_all):
    if not rev:
        return lambda c: c
    return lambda c: jnp.where(c < n_ctx, n_ctx - 1 - c, n_all + n_ctx - 1 - c)


def _scan_call(body, u, col_blocks, extra, out_width, scratch, *, rev, n_ctx, name):
    bsz, t_all, _ = u.shape
    cn, bb = SCAN_CHUNK, SCAN_BATCH
    assert bsz % bb == 0
    n_all = t_all // cn
    cidx = _chunk_index(rev, n_ctx, n_all)

    def tok(width, off):
        return pl.BlockSpec((bb, cn, width), lambda b, c: (b, cidx(c), off // width))

    in_specs = [tok(w, off) for w, off in col_blocks]
    in_specs += [pl.BlockSpec(a.shape, lambda b, c: (0, 0)) for a in extra]
    return pl.pallas_call(
        body,
        grid=(bsz // bb, n_all),
        in_specs=in_specs,
        out_specs=tok(out_width, 0),
        out_shape=jax.ShapeDtypeStruct((bsz, t_all, out_width), F32),
        scratch_shapes=[scratch],
        compiler_params=_params(("parallel", "arbitrary")),
        name=name,
    )(*([u] * len(col_blocks)), *extra)


def _reset_state(st_ref):
    @pl.when(pl.program_id(1) == 0)
    def _():
        st_ref[...] = jnp.zeros_like(st_ref)


def _ssd_kernel(x_ref, bc_ref, sm_ref, bias_ref, arow_ref, drow_ref, y_ref, st_ref, *, rev):
    bb, cn = x_ref.shape[0], x_ref.shape[1]
    _reset_state(st_ref)
    lane0 = SM_DT[1] if rev else SM_DT[0]
    mask = _scan_mask(cn, rev)
    ex = _expand_matrix(lane0, SSD_WIDTH)
    gw = SSD_WIDTH // SSD_GROUPS
    hpg = SSD_HEADS // SSD_GROUPS
    lane_head = _shr(_iota((1, gw), 1), HEAD)
    last = 0 if rev else cn - 1
    for i in range(bb):
        xs = x_ref[i]
        bc = bc_ref[i]
        dt = _softplus(sm_ref[i] + bias_ref[...])
        da = dt * arow_ref[...]
        acs = _cumsum_rows(da, rev)
        acs_r = acs.T
        acs_e = _dot3_exact_rhs(acs, ex)
        dt_e = _dot3_exact_rhs(dt, ex)
        tot_e = acs_e[last:last + 1]
        xdt = xs * dt_e
        xdt_b = xdt.astype(BF16)
        wst = (xdt * jnp.exp(tot_e - acs_e)).astype(BF16)
        eacs = jnp.exp(acs_e)
        etot = jnp.exp(tot_e)
        for g in range(SSD_GROUPS):
            bm = bc[:, g * SSD_STATE:(g + 1) * SSD_STATE]
            cm = bc[:, (SSD_GROUPS + g) * SSD_STATE:(SSD_GROUPS + g + 1) * SSD_STATE]
            bmb, cmb = bm.astype(BF16), cm.astype(BF16)
            cb = _dot_nt(cmb, bmb)
            xg = xdt_b[:, g * gw:(g + 1) * gw]
            yd = jnp.zeros((cn, gw), F32)
            for r in range(hpg):
                ln = lane0 + g * hpg + r
                diff = acs[:, ln:ln + 1] - acs_r[ln:ln + 1, :]
                dec = jnp.where(mask, jnp.exp(jnp.where(mask, diff, 0.0)), 0.0)
                yh = _dot((cb * dec).astype(BF16), xg)
                yd = jnp.where(lane_head == r, yh, yd)
            st = st_ref[i * SSD_GROUPS + g]
            yo = _dot(cmb, st.astype(BF16)) * eacs[:, g * gw:(g + 1) * gw]
            y = yd + yo
            if drow_ref is not None:
                y = y + drow_ref[:, g * gw:(g + 1) * gw] * xs[:, g * gw:(g + 1) * gw]
            y_ref[i, :, g * gw:(g + 1) * gw] = y
            st_ref[i * SSD_GROUPS + g] = (st * etot[:, g * gw:(g + 1) * gw]
                                          + _dot(bm.T.astype(BF16), wst[:, g * gw:(g + 1) * gw]))


def _ssd_kernel_no_skip(x_ref, bc_ref, sm_ref, bias_ref, arow_ref, y_ref, st_ref, *, rev):
    _ssd_kernel(x_ref, bc_ref, sm_ref, bias_ref, arow_ref, None, y_ref, st_ref, rev=rev)


def _ssd_scan(u, dt_bias, a_log, d_skip, *, rev, n_ctx):
    di = 1 if rev else 0
    lane0 = SM_DT[di]
    bias = jnp.zeros((1, LANES), F32).at[0, lane0:lane0 + SSD_HEADS].set(dt_bias[di])
    arow = jnp.zeros((1, LANES), F32).at[0, lane0:lane0 + SSD_HEADS].set(-jnp.exp(a_log[di].astype(F32)))
    extra = [bias, arow]
    if rev:
        body = functools.partial(_ssd_kernel_no_skip, rev=rev)
    else:
        extra.append(jnp.repeat(d_skip.astype(F32), HEAD).reshape(1, SSD_WIDTH))
        body = functools.partial(_ssd_kernel, rev=rev)
    scratch = pltpu.VMEM((SCAN_BATCH * SSD_GROUPS, SSD_STATE, SSD_WIDTH // SSD_GROUPS), F32)
    return _scan_call(body, u, [(SSD_WIDTH, X_OFF), (512, BC_OFF), (LANES, SM_OFF)], extra, SSD_WIDTH, scratch,
                      rev=rev, n_ctx=n_ctx, name="ssd_scan_bwd" if rev else "ssd_scan_fwd")


def _hgrn_kernel(q_ref, f_ref, i_ref, low_ref, y_ref, st_ref, *, rev):
    bb, cn = q_ref.shape[0], q_ref.shape[1]
    wdt = HG_WIDTH
    heads = wdt // HEAD
    _reset_state(st_ref)
    lower = low_ref[0:1]
    log_lower = low_ref[1:2]
    hb = _head_blocks(wdt)
    bd = _ones_where(hb)
    lane_head = _shr(_iota((1, wdt), 1), HEAD)
    last = 0 if rev else cn - 1

    for i in range(bb):
        fr = f_ref[i]
        qs = _silu(q_ref[i])
        v = i_ref[i]
        a1, a2 = _log_sigmoid(fr), log_lower + _log_sigmoid(-fr)
        logf = jnp.maximum(a1, a2) + jnp.log1p(jnp.exp(-jnp.abs(a1 - a2)))
        kg = (1.0 - lower) * jax.nn.sigmoid(-fr)
        b = _cumsum_rows(logf, rev)
        vb = v.astype(BF16)

        st = st_ref[i]
        y_ref[i] = _dot_nt((qs * jnp.exp(b)).astype(BF16), st.astype(BF16))
        b_last = b[last:last + 1]
        kend = (kg * jnp.exp(b_last - b)).astype(BF16)
        st_ref[i] = st * jnp.exp(b_last) + jnp.where(hb, _dot(v.T.astype(BF16), kend), 0.0)

        def offdiag(t0, t1, s0, s1, r):
            br = b[r:r + 1]
            qp = qs[t0:t1] * jnp.exp(b[t0:t1] - br)
            kp = (kg[s0:s1] * jnp.exp(br - b[s0:s1])).astype(BF16)
            nt = t1 - t0
            qstack = jnp.concatenate([jnp.where(lane_head == h, qp, 0.0) for h in range(heads)], axis=0)
            att = _dot_nt(qstack.astype(BF16), kp)
            res = _dot(att.astype(BF16), vb[s0:s1])
            out = jnp.zeros((nt, wdt), F32)
            for h in range(heads):
                out = jnp.where(lane_head == h, res[h * nt:(h + 1) * nt], out)
            y_ref[i, t0:t1, :] += out

        def diag(t0, t1):
            n = t1 - t0
            bt = b[t0:t1]
            ti, si = _iota((n, n, wdt), 0), _iota((n, n, wdt), 1)
            m3 = (si >= ti) if rev else (si <= ti)
            diff = bt[:, None, :] - bt[None, :, :]
            w = jnp.where(m3, jnp.exp(jnp.where(m3, diff, 0.0)), 0.0)
            p = w * qs[t0:t1][:, None, :] * kg[t0:t1][None, :, :]
            r2 = _dot(p.reshape(n * n, wdt).astype(BF16), bd)
            y_ref[i, t0:t1, :] += jnp.sum(r2.reshape(n, n, wdt) * v[t0:t1][None, :, :], axis=1)

        def block(lo, hi):
            if hi - lo <= BASE:
                diag(lo, hi)
                return
            mid = (lo + hi) // 2
            if rev:
                offdiag(lo, mid, mid, hi, mid)
            else:
                offdiag(mid, hi, lo, mid, mid - 1)
            block(lo, mid)
            block(mid, hi)

        block(0, cn)


def _hgrn_scan(u, lower, *, rev, n_ctx):
    f_off = HFB_OFF if rev else HFF_OFF
    low = jnp.stack([lower, jnp.log(jnp.maximum(lower, MIN_LOWER))]).astype(F32)
    low = jnp.concatenate([low, jnp.zeros((6, HG_WIDTH), F32)], axis=0)
    scratch = pltpu.VMEM((SCAN_BATCH, HG_WIDTH, HG_WIDTH), F32)
    return _scan_call(functools.partial(_hgrn_kernel, rev=rev), u,
                      [(HG_WIDTH, HQ_OFF), (HG_WIDTH, f_off), (HG_WIDTH, HI_OFF)], [low], HG_WIDTH, scratch,
                      rev=rev, n_ctx=n_ctx, name="hgrn_scan_bwd" if rev else "hgrn_scan_fwd")


def _bdot(a, b):
    return lax.dot_general(a, b, (((2,), (1,)), ((0,), (0,))), preferred_element_type=F32)


def _unit_inverse_delta(a):
    n_rows = a.shape[-1]
    r, c = _iota((n_rows, n_rows), 0), _iota((n_rows, n_rows), 1)
    d = jnp.where((_shr(r, BASE) == _shr(c, BASE))[None], a, 0.0)
    db = d.astype(BF16)
    p = _bdot(db, db)
    n = p - d - _bdot(db, p.astype(BF16))
    e = 4
    while e < BASE:
        pb = p.astype(BF16)
        p = _bdot(pb, pb)
        n = n + p + _bdot(n.astype(BF16), p.astype(BF16))
        e *= 2
    size = BASE
    while size < n_rows:
        big = 2 * size
        off = (_shr(r, big) == _shr(c, big)) & (_shr(r, size) != _shr(c, size))
        a_off = jnp.where(off[None], a, 0.0)
        nb = n.astype(BF16)
        m = a_off + _bdot(nb, a_off.astype(BF16))
        n = n - (m + _bdot(m.astype(BF16), nb))
        size = big
    return n


def _gdn_kernel(q_ref, k_ref, v_ref, sm_ref, bias_ref, arow_ref, y_ref, st_ref, *, rev):
    bb, cn = q_ref.shape[0], q_ref.shape[1]
    wdt = GD_WIDTH
    _reset_state(st_ref)
    hb = _head_blocks(wdt)
    bd = _ones_where(hb)
    lane_head = _shr(_iota((1, wdt), 1), HEAD)
    la = SM_A[1] if rev else SM_A[0]
    lb = SM_B[1] if rev else SM_B[0]
    ex_a, ex_b = _expand_matrix(la, wdt), _expand_matrix(lb, wdt)
    mask = _scan_mask(cn, rev)
    strict = mask & (_iota((cn, cn), 0) != _iota((cn, cn), 1))
    last = 0 if rev else cn - 1
    pre, a_all = [], []
    for i in range(bb):
        q, k, v = q_ref[i], k_ref[i], v_ref[i]
        q = q * lax.rsqrt(_dot3_exact_rhs(q * q, bd) + 1e-6) * (HEAD ** -0.5)
        k = k * lax.rsqrt(_dot3_exact_rhs(k * k, bd) + 1e-6)
        sm = sm_ref[i]
        g = arow_ref[...] * _softplus(sm + bias_ref[...])
        beta = jax.nn.sigmoid(sm)
        gam = _cumsum_rows(g, rev)
        gam_r = gam.T
        gam_e = _dot3_exact_rhs(gam, ex_a)
        beta_e = _dot3_exact_rhs(beta, ex_b)
        tot_e = gam_e[last:last + 1]
        egam = jnp.exp(gam_e)
        kb = k * beta_e
        kbf = k.astype(BF16)
        rhs = jnp.concatenate([v * beta_e, kb * egam], axis=1)
        qks = []
        for h in range(GD_HEADS):
            ln = la + h
            diff = gam[:, ln:ln + 1] - gam_r[ln:ln + 1, :]
            dec = jnp.where(mask, jnp.exp(jnp.where(mask, diff, 0.0)), 0.0)
            hm = lane_head == h
            kk = _dot_nt(jnp.where(hm, kb, 0.0).astype(BF16), kbf)
            a_all.append(jnp.where(strict, kk * dec, 0.0))
            qks.append((_dot_nt(jnp.where(hm, q, 0.0).astype(BF16), kbf) * dec).astype(BF16))
        pre.append((q, k, rhs, qks, egam, gam_e, tot_e))
    n_all = _unit_inverse_delta(jnp.stack(a_all)).astype(BF16)
    for i in range(bb):
        q, k, rhs, qks, egam, gam_e, tot_e = pre[i]
        rhs_b = rhs.astype(BF16)
        u_all = jnp.zeros((cn, wdt), F32)
        w_all = jnp.zeros((cn, wdt), F32)
        for h in range(GD_HEADS):
            sol = rhs + _dot(n_all[i * GD_HEADS + h], rhs_b)
            hm = lane_head == h
            u_all = jnp.where(hm, sol[:, :wdt], u_all)
            w_all = jnp.where(hm, sol[:, wdt:], w_all)
        st = st_ref[i]
        stb = st.astype(BF16)
        v_new = u_all - _dot(w_all.astype(BF16), stb)
        vnb = v_new.astype(BF16)
        o = _dot((q * egam).astype(BF16), stb)
        for h in range(GD_HEADS):
            o = o + jnp.where(lane_head == h, _dot(qks[h], vnb), 0.0)
        y_ref[i] = o
        kend = k * jnp.exp(tot_e - gam_e)
        st_ref[i] = st * jnp.exp(tot_e) + jnp.where(hb, _dot(kend.T.astype(BF16), vnb), 0.0)


def _gdn_scan(u, dt_bias, a_log, *, rev, n_ctx):
    di = 1 if rev else 0
    la = SM_A[di]
    bias = jnp.zeros((1, LANES), F32).at[0, la:la + GD_HEADS].set(dt_bias[di])
    arow = jnp.zeros((1, LANES), F32).at[0, la:la + GD_HEADS].set(-jnp.exp(a_log[di].astype(F32)))
    scratch = pltpu.VMEM((SCAN_BATCH, GD_WIDTH, GD_WIDTH), F32)
    return _scan_call(functools.partial(_gdn_kernel, rev=rev), u,
                      [(GD_WIDTH, GQ_OFF), (GD_WIDTH, GK_OFF), (GD_WIDTH, GV_OFF), (LANES, SM_OFF)], [bias, arow],
                      GD_WIDTH, scratch, rev=rev, n_ctx=n_ctx, name="gdn_scan_bwd" if rev else "gdn_scan_fwd")


def _pack_rows(x):
    half = x.shape[1] // 2
    q = half // 2
    xr = x.astype(BF16).astype(F32)
    lo = lax.bitcast_convert_type(xr[:, :half], jnp.uint32)
    hi = lax.bitcast_convert_type(xr[:, half:], jnp.uint32)
    word = (lo >> 16) | (hi & jnp.uint32(0xFFFF0000))
    return word[:, :q], word[:, q:]


def _unpack_rows(wa, wb):
    mask = jnp.uint32(0xFFFF0000)
    f = lambda w: lax.bitcast_convert_type(w, F32)
    return f(wa << 16), f(wb << 16), f(wa & mask), f(wb & mask)


def _group_rms(y, group, bd):
    ms = _dot3_exact_rhs(y * y, bd) * (1.0 / group)
    return y * lax.rsqrt(ms + EPS)


def _outproj_kernel(ysf, ysb, yhf, yhb, ygf, ygb, z_ref, hg_ref, gg_ref, h_ref, gm_ref, scf_ref, shf_ref, gf_ref,
                    nrm_ref, wout_ref, nf_ref, rw_ref, rb_ref, sg_ref, su_ref, sd_ref,
                    h2_ref, xpa_ref, xpb_ref, idx_ref, gate_ref):
    tm = h_ref.shape[1]
    gw = SSD_WIDTH // SSD_GROUPS
    r, c = _iota((SSD_WIDTH, SSD_WIDTH), 0), _iota((SSD_WIDTH, SSD_WIDTH), 1)
    bd_s = _ones_where(_shr(r, gw) == _shr(c, gw))
    bd_h = _ones_where(_head_blocks(HG_WIDTH))
    ys = (ysf[0] + ysb[0]) * _silu(z_ref[0])
    ys = _group_rms(ys, gw, bd_s) * nrm_ref[:, 0:SSD_WIDTH]
    yh = _group_rms(yhf[0] + yhb[0], HEAD, bd_h) * nrm_ref[:, SSD_WIDTH:SSD_WIDTH + HG_WIDTH] * _silu(hg_ref[0])
    yg = _group_rms(ygf[0] + ygb[0], HEAD, bd_h) * nrm_ref[:, SSD_WIDTH + HG_WIDTH:] * _silu(gg_ref[0])
    y = jnp.concatenate([ys, yh, yg], axis=1).astype(BF16)
    h1 = h_ref[0] + gm_ref[0] * _dot(y, wout_ref[...])
    xf = h1 * lax.rsqrt(jnp.mean(h1 * h1, axis=-1, keepdims=True) + EPS) * nf_ref[...]
    xf = xf * (1.0 + scf_ref[0]) + shf_ref[0]
    xb = xf.astype(BF16)
    xpa_ref[0], xpb_ref[0] = _pack_rows(xf)
    hid = _silu(_dot(xb, sg_ref[...])) * _dot(xb, su_ref[...])
    h2_ref[0] = h1 + gf_ref[0] * _dot(hid.astype(BF16), sd_ref[...])
    scores = jax.nn.sigmoid(_dot3(xf, rw_ref[...]))
    lane = _iota((tm, LANES), 1)
    sel = jnp.where(lane < N_EXPERTS, scores + rb_ref[...], -jnp.inf)
    idx_out = jnp.zeros((tm, LANES), jnp.int32)
    gate_out = jnp.zeros((tm, LANES), F32)
    gsum = jnp.zeros((tm, 1), F32)
    for j in range(TOP_K):
        m = jnp.max(sel, axis=-1, keepdims=True)
        pick = jnp.min(jnp.where(sel == m, lane, LANES), axis=-1, keepdims=True)
        hit = lane == pick
        gv = jnp.sum(jnp.where(hit, scores, 0.0), axis=-1, keepdims=True)
        idx_out = jnp.where(lane == j, pick, idx_out)
        gate_out = jnp.where(lane == j, gv, gate_out)
        gsum = gsum + gv
        sel = jnp.where(hit, -jnp.inf, sel)
    idx_ref[0] = idx_out
    gate_ref[0] = gate_out / gsum * ROUTED_SCALE


def _outproj(ys, yh, yg, u, h, mod, nrm, w_out, norm_ffn, router_wp, router_bp, sg, su, sd, *, ctx):
    bsz, t_all, d = h.shape
    tm = TOKEN_TILE
    ctx_tiles = ctx // tm
    rows = mod.shape[0] // 6
    mrow = lambda k: pl.BlockSpec((1, 1, d), _mod_row(k, ctx_tiles, rows))
    tok = lambda w, j=0: pl.BlockSpec((1, tm, w), lambda b, t: (b, t, j))
    full = lambda a: pl.BlockSpec(a.shape, lambda b, t: (0,) * a.ndim)
    in_specs = [tok(SSD_WIDTH), tok(SSD_WIDTH), tok(HG_WIDTH), tok(HG_WIDTH), tok(GD_WIDTH), tok(GD_WIDTH),
                tok(SSD_WIDTH, Z_OFF // SSD_WIDTH), tok(HG_WIDTH, HGATE_OFF // HG_WIDTH),
                tok(GD_WIDTH, GGATE_OFF // GD_WIDTH),
                tok(d), mrow(2), mrow(4), mrow(3), mrow(5),
                full(nrm), full(w_out), full(norm_ffn), full(router_wp), full(router_bp), full(sg), full(su), full(sd)]
    pw = d // 4
    out_shape = (jax.ShapeDtypeStruct((bsz, t_all, d), F32),
                 jax.ShapeDtypeStruct((bsz, t_all, pw), jnp.uint32), jax.ShapeDtypeStruct((bsz, t_all, pw), jnp.uint32),
                 jax.ShapeDtypeStruct((bsz, t_all, LANES), jnp.int32), jax.ShapeDtypeStruct((bsz, t_all, LANES), F32))
    out_specs = (tok(d), tok(pw), tok(pw), tok(LANES), tok(LANES))
    return pl.pallas_call(
        _outproj_kernel,
        grid=(bsz, t_all // tm),
        in_specs=in_specs,
        out_specs=out_specs,
        out_shape=out_shape,
        compiler_params=_params(("parallel", "parallel")),
        name="out_projection_router",
    )(ys[0], ys[1], yh[0], yh[1], yg[0], yg[1], u, u, u, h, mod, mod, mod, mod,
      nrm, w_out, norm_ffn, router_wp, router_bp, sg, su, sd)


def _rank_kernel(idx_ref, rank_ref, cnt_ref, base_ref):
    tm = idx_ref.shape[0]

    @pl.when(pl.program_id(0) == 0)
    def _():
        base_ref[...] = jnp.zeros_like(base_ref)

    idx = idx_ref[...]
    lane = _iota((tm, LANES), 1)
    hits = [lane == idx[:, j:j + 1] for j in range(TOP_K)]
    m = jnp.zeros((tm, LANES), F32)
    for hit in hits:
        m = m + jnp.where(hit, 1.0, 0.0)
    before = _ones_where(_iota((tm, tm), 1) < _iota((tm, tm), 0))
    base = base_ref[...]
    val = _dot(before, m.astype(BF16)) + base
    out = jnp.zeros((tm, LANES), jnp.int32)
    for j, hit in enumerate(hits):
        rj = jnp.sum(jnp.where(hit, val, 0.0), axis=-1, keepdims=True)
        out = jnp.where(lane == j, rj.astype(jnp.int32), out)
    rank_ref[...] = out
    total = base + jnp.sum(m, axis=0, keepdims=True)
    base_ref[...] = total
    cnt_ref[...] = total


def _route_ranks(idx2d):
    n_tok = idx2d.shape[0]
    tm = TOKEN_TILE
    return pl.pallas_call(
        _rank_kernel,
        grid=(n_tok // tm,),
        in_specs=[pl.BlockSpec((tm, LANES), lambda i: (i, 0))],
        out_specs=(pl.BlockSpec((tm, LANES), lambda i: (i, 0)), pl.BlockSpec((1, LANES), lambda i: (0, 0))),
        out_shape=(jax.ShapeDtypeStruct((n_tok, LANES), jnp.int32), jax.ShapeDtypeStruct((1, LANES), F32)),
        scratch_shapes=[pltpu.VMEM((1, LANES), F32)],
        compiler_params=_params(("arbitrary",)),
        name="route_ranks",
    )(idx2d)


def _expert_kernel(be_ref, xa_ref, xb_ref, wg_ref, wu_ref, wd_ref, oa_ref, ob_ref):
    q = xa_ref.shape[1]
    parts = [p.astype(BF16) for p in _unpack_rows(xa_ref[...], xb_ref[...])]

    def proj(w_ref):
        acc = _dot(parts[0], w_ref[0, 0:q, :])
        for j in range(1, 4):
            acc = acc + _dot(parts[j], w_ref[0, j * q:(j + 1) * q, :])
        return acc

    hid = _silu(proj(wg_ref)) * proj(wu_ref)
    oa_ref[...], ob_ref[...] = _pack_rows(_dot(hid.astype(BF16), wd_ref[0]))


def _expert_blocks(xsa, xsb, block_expert, w_gate, w_up, w_down):
    n_rows, pw = xsa.shape
    bm = EXPERT_ROWS
    d, e_dim = w_gate.shape[1], w_gate.shape[2]
    row = lambda: pl.BlockSpec((bm, pw), lambda i, be: (i, 0))
    grid_spec = pltpu.PrefetchScalarGridSpec(
        num_scalar_prefetch=1,
        grid=(n_rows // bm,),
        in_specs=[row(), row(),
                  pl.BlockSpec((1, d, e_dim), lambda i, be: (be[i], 0, 0)),
                  pl.BlockSpec((1, d, e_dim), lambda i, be: (be[i], 0, 0)),
                  pl.BlockSpec((1, e_dim, d), lambda i, be: (be[i], 0, 0))],
        out_specs=(row(), row()),
    )
    return pl.pallas_call(
        _expert_kernel,
        grid_spec=grid_spec,
        out_shape=(jax.ShapeDtypeStruct((n_rows, pw), jnp.uint32), jax.ShapeDtypeStruct((n_rows, pw), jnp.uint32)),
        compiler_params=_params(("arbitrary",)),
        name="routed_experts",
    )(block_expert, xsa, xsb, w_gate, w_up, w_down)


def _combine_kernel(h_ref, ra_ref, rb_ref, gate_ref, gf_ref, o_ref):
    tm, q = ra_ref.shape[1], ra_ref.shape[2]
    gate = gate_ref[0]
    acc = [jnp.zeros((tm, q), F32) for _ in range(4)]
    for j in range(TOP_K):
        gj = gate[:, j:j + 1]
        for blk, part in enumerate(_unpack_rows(ra_ref[j], rb_ref[j])):
            acc[blk] = acc[blk] + part * gj
    for blk in range(4):
        cols = slice(blk * q, (blk + 1) * q)
        o_ref[0, :, cols] = h_ref[0, :, cols] + gf_ref[0, :, cols] * acc[blk]


def _combine(h2, ra, rb, gate, mod, *, ctx):
    bsz, t_all, d = h2.shape
    tm = TOKEN_TILE
    tiles = t_all // tm
    ctx_tiles = ctx // tm
    rows = mod.shape[0] // 6
    pw = ra.shape[-1]
    per = lambda: pl.BlockSpec((TOP_K, tm, pw), lambda b, t: (0, b * tiles + t, 0))
    return pl.pallas_call(
        _combine_kernel,
        grid=(bsz, tiles),
        in_specs=[pl.BlockSpec((1, tm, d), lambda b, t: (b, t, 0)), per(), per(),
                  pl.BlockSpec((1, tm, LANES), lambda b, t: (b, t, 0)),
                  pl.BlockSpec((1, 1, d), _mod_row(5, ctx_tiles, rows))],
        out_specs=pl.BlockSpec((1, tm, d), lambda b, t: (b, t, 0)),
        out_shape=jax.ShapeDtypeStruct((bsz, t_all, d), F32),
        compiler_params=_params(("parallel", "parallel")),
        name="moe_combine",
    )(h2, ra, rb, gate, mod)


def _gather_rows(xa, xb, indices):
    n = indices.shape[0]
    q = xa.shape[1]
    win = GATHER_WINDOW
    mesh = plsc.VectorSubcoreMesh(core_axis_name="core", subcore_axis_name="subcore")
    workers = mesh.num_cores * mesh.num_subcores
    per = n // (win * workers)
    assert per * win * workers == n, (n, win, workers)
    out = jax.ShapeDtypeStruct((n, q), xa.dtype)
    scratch = [pltpu.VMEM((per, win), jnp.int32), pltpu.VMEM((win, q), xa.dtype), pltpu.VMEM((win, q), xa.dtype),
               pltpu.SemaphoreType.DMA((4,))]

    @functools.partial(pl.kernel, out_type=(out, out), mesh=mesh, scratch_types=scratch)
    def gather(xa_hbm, xb_hbm, i_hbm, oa_hbm, ob_hbm, idx_vmem, buf_a, buf_b, sems):
        wid = lax.axis_index("core") * mesh.num_subcores + lax.axis_index("subcore")
        pltpu.sync_copy(i_hbm.at[wid], idx_vmem)

        @pl.loop(0, per)
        def _(s):
            rows = pl.ds(pl.multiple_of((wid * per + s) * win, win), win)
            ga = pltpu.async_copy(xa_hbm.at[idx_vmem.at[s]], buf_a, sems.at[0])
            gb = pltpu.async_copy(xb_hbm.at[idx_vmem.at[s]], buf_b, sems.at[1])
            ga.wait()
            wa = pltpu.async_copy(buf_a, oa_hbm.at[rows], sems.at[2])
            gb.wait()
            wb = pltpu.async_copy(buf_b, ob_hbm.at[rows], sems.at[3])
            wa.wait()
            wb.wait()

    return gather(xa, xb, indices.reshape(workers, per, win))


def _routed(xpa, xpb, idx, w_gate, w_up, w_down):
    bsz, t_all, pw = xpa.shape
    n_tok = bsz * t_all
    n_assign = n_tok * TOP_K
    bm = EXPERT_ROWS
    n_blocks = -(-n_assign // bm) + N_EXPERTS
    n_slots = n_blocks * bm
    idx2d = idx.reshape(n_tok, LANES)
    rank, cnt = _route_ranks(idx2d)
    counts = cnt[0, :N_EXPERTS].astype(jnp.int32)
    padded = (counts + bm - 1) // bm * bm
    pad_end = jnp.cumsum(padded)
    offset = pad_end - padded
    first = jnp.cumsum(counts) - counts
    ek = idx2d[:, :TOP_K]
    slot = jnp.take(offset, ek) + rank[:, :TOP_K]
    block_expert = jnp.minimum(
        jnp.sum(pad_end[None, :] <= (jnp.arange(n_blocks, dtype=jnp.int32) * bm)[:, None], axis=1),
        N_EXPERTS - 1).astype(jnp.int32)
    token_sorted = jnp.argsort(ek.reshape(-1)) // TOP_K
    e_slot = jnp.repeat(block_expert, bm)
    r_slot = jnp.arange(n_slots, dtype=jnp.int32) - jnp.take(offset, e_slot)
    live = r_slot < jnp.take(counts, e_slot)
    src = jnp.where(live, jnp.take(token_sorted, jnp.clip(jnp.take(first, e_slot) + r_slot, 0, n_assign - 1)), 0)
    xsa, xsb = _gather_rows(xpa.reshape(n_tok, pw), xpb.reshape(n_tok, pw), src.astype(jnp.int32))
    oa, ob = _expert_blocks(xsa, xsb, block_expert, w_gate, w_up, w_down)
    ra, rb = _gather_rows(oa, ob, slot.T.reshape(-1))
    return ra.reshape(TOP_K, n_tok, pw), rb.reshape(TOP_K, n_tok, pw)


def _final_kernel(h_ref, w_ref, o_ref):
    h = h_ref[0]
    o_ref[0] = h * lax.rsqrt(jnp.mean(h * h, axis=-1, keepdims=True) + EPS) * w_ref[...]


def _final_norm(h, w, *, ctx):
    bsz, t_all, d = h.shape
    tm = TOKEN_TILE
    seq = t_all - ctx
    ctx_tiles = ctx // tm
    return pl.pallas_call(
        _final_kernel,
        grid=(bsz, seq // tm),
        in_specs=[pl.BlockSpec((1, tm, d), lambda b, t: (b, t + ctx_tiles, 0)),
                  pl.BlockSpec((1, d), lambda b, t: (0, 0))],
        out_specs=pl.BlockSpec((1, tm, d), lambda b, t: (b, t, 0)),
        out_shape=jax.ShapeDtypeStruct((bsz, seq, d), F32),
        compiler_params=_params(("parallel", "parallel")),
        name="final_norm",
    )(h, w.reshape(1, d))


def kernel(x, c, ctx, c_ctx, w_mod, b_mod, norm_mix, w_in, ssd_conv_w, ssd_conv_b, ssd_dt_bias, ssd_a_log, ssd_d,
           ssd_norm, hgrn_lb, hgrn_norm, gdn_conv_w, gdn_dt_bias, gdn_a_log, gdn_norm, w_out, norm_ffn, router_w,
           router_bias, exp_gate, exp_up, exp_down, sh_gate, sh_up, sh_down, norm_final):
    bsz, seq, d = x.shape
    n_ctx_tok = ctx.shape[1]
    depth = w_in.shape[0]
    assert d == D_MODEL and n_ctx_tok == TOKEN_TILE and seq % TOKEN_TILE == 0 and TOKEN_TILE % SCAN_CHUNK == 0
    assert TOKEN_TILE % GRID_W == 0 and seq % GRID_W == 0
    n_ctx = n_ctx_tok // SCAN_CHUNK

    p_lb = jax.nn.softmax(hgrn_lb.astype(F32), axis=0)
    lower_all = jnp.cumsum(p_lb, axis=0) - p_lb[0]

    rows = -(-(bsz + 1) // 8) * 8
    cond = jnp.zeros((rows, d), F32).at[:bsz].set(c).at[rows - 1].set(c_ctx)

    h = jnp.concatenate([ctx, x], axis=1)
    for l in range(depth):
        mod = _modulation(cond, w_mod[l], b_mod[l]).reshape(rows * 6, 1, d)
        w_p = _permute_w_in(w_in[l]).astype(BF16)
        u = _inproj(h, mod, norm_mix[l], w_p, ssd_conv_w[l], ssd_conv_b[l], gdn_conv_w[l], ctx=n_ctx_tok)
        ys = [_ssd_scan(u, ssd_dt_bias[l], ssd_a_log[l], ssd_d[l], rev=r, n_ctx=n_ctx) for r in (False, True)]
        yh = [_hgrn_scan(u, lower_all[l], rev=r, n_ctx=n_ctx) for r in (False, True)]
        yg = [_gdn_scan(u, gdn_dt_bias[l], gdn_a_log[l], rev=r, n_ctx=n_ctx) for r in (False, True)]
        nrm = jnp.concatenate([ssd_norm[l], jnp.tile(hgrn_norm[l], HG_WIDTH // HEAD),
                               jnp.tile(gdn_norm[l], GD_WIDTH // HEAD)]).astype(F32).reshape(1, d)
        rwp = jnp.pad(router_w[l].astype(F32), ((0, 0), (0, LANES - N_EXPERTS)))
        rbp = jnp.pad(router_bias[l].astype(F32), (0, LANES - N_EXPERTS)).reshape(1, LANES)
        h2, xpa, xpb, idx, gate = _outproj(ys, yh, yg, u, h, mod, nrm, w_out[l].astype(BF16),
                                           norm_ffn[l].reshape(1, d), rwp, rbp, sh_gate[l].astype(BF16),
                                           sh_up[l].astype(BF16), sh_down[l].astype(BF16), ctx=n_ctx_tok)
        ra, rb = _routed(xpa, xpb, idx, exp_gate[l].astype(BF16), exp_up[l].astype(BF16), exp_down[l].astype(BF16))
        h = _combine(h2, ra, rb, gate, mod, ctx=n_ctx_tok)
    return _final_norm(h, norm_final, ctx=n_ctx_tok)
```

```python
import functools

import numpy as np
import jax
import jax.numpy as jnp
from jax import lax
from jax.experimental import pallas as pl
from jax.experimental.pallas import tpu as pltpu
from jax.experimental.pallas import tpu_sc as plsc

F32 = jnp.float32
BF16 = jnp.bfloat16

D_MODEL = 1024
GRID_W = 64
CONV_W = 5
EPS = 1e-6
MIN_LOWER = 1e-30
MASKED_EXPONENT = -1e30
HEAD = 64
SSD_HEADS = 8
SSD_WIDTH = 512
SSD_STATE = 128
SSD_GROUPS = 2
HG_WIDTH = 256
GD_WIDTH = 256
GD_HEADS = 4
N_EXPERTS = 64
TOP_K = 8
EXPERT_DIM = 256
ROUTED_SCALE = 2.5

LANES = 128
SCAN_CHUNK = 128
SCAN_BATCH = 4
BASE = 16
TOKEN_TILE = 256
EXPERT_ROWS = 512
GATHER_WINDOW = 128
VMEM_LIMIT = 56 * 1024 * 1024

Z_OFF, X_OFF, BC_OFF = 0, 512, 1024
HQ_OFF, HFF_OFF, HFB_OFF, HI_OFF, HGATE_OFF = 1536, 1792, 2048, 2304, 2560
GQ_OFF, GK_OFF, GV_OFF, GGATE_OFF = 2816, 3072, 3328, 3584
SM_OFF = 3840
NCOLS = 3968
SSD_CONV = 1024
GDN_CONV = 768
SM_DT = (0, 8)
SM_A = (16, 20)
SM_B = (24, 28)


def _permute_w_in(w):
    d = w.shape[0]
    parts = [w[:, 0:1536],
             w[:, 1552:2832],
             w[:, 2832:3600],
             w[:, 3616:3872],
             w[:, 1536:1552],
             w[:, 3600:3616],
             jnp.zeros((d, LANES - 32), w.dtype)]
    return jnp.concatenate(parts, axis=1)


def _dot(a, b):
    return lax.dot_general(a, b, (((1,), (0,)), ((), ())), preferred_element_type=F32)


def _dot_nt(a, b):
    return lax.dot_general(a, b, (((1,), (1,)), ((), ())), preferred_element_type=F32)


def _split(a):
    hi = a.astype(BF16)
    lo = (a - hi.astype(F32)).astype(BF16)
    return hi, lo


def _dot3(a, b):
    ah, al = _split(a)
    bh, bl = _split(b)
    return _dot(ah, bh) + (_dot(ah, bl) + _dot(al, bh))


def _dot3_exact_rhs(a, b_bf16):
    ah, al = _split(a)
    return _dot(ah, b_bf16) + _dot(al, b_bf16)


def _silu(x):
    return x * jax.nn.sigmoid(x)


def _softplus(x):
    return jnp.maximum(x, 0.0) + jnp.log1p(jnp.exp(-jnp.abs(x)))


def _log_sigmoid(x):
    return jnp.minimum(x, 0.0) - jnp.log1p(jnp.exp(-jnp.abs(x)))


def _params(sem):
    return pltpu.CompilerParams(dimension_semantics=sem, vmem_limit_bytes=VMEM_LIMIT)


def _iota(shape, dim):
    return lax.broadcasted_iota(jnp.int32, shape, dim)


def _ones_where(mask):
    return jnp.where(mask, 1.0, 0.0).astype(BF16)


def _shr(x, div):
    return jnp.right_shift(x, int(np.log2(div)))


def _scan_mask(n, rev):
    r, c = _iota((n, n), 0), _iota((n, n), 1)
    return (c >= r) if rev else (c <= r)


def _expand_matrix(lane0, width):
    r, c = _iota((LANES, width), 0), _iota((LANES, width), 1)
    return _ones_where(r == lane0 + _shr(c, HEAD))


def _head_blocks(n):
    r, c = _iota((n, n), 0), _iota((n, n), 1)
    return _shr(r, HEAD) == _shr(c, HEAD)


def _cumsum_rows(x, rev):
    tri = _ones_where(_scan_mask(x.shape[0], rev))
    xh, xl = _split(x)
    return _dot(tri, xh) + _dot(tri, xl)


def _mod_kernel(s_ref, w_ref, b_ref, o_ref):
    s = _silu(s_ref[...])
    o_ref[...] = _dot(s.astype(BF16), w_ref[...].astype(BF16)) + b_ref[...]


def _modulation(cond, w, b):
    rows, d = cond.shape
    n = w.shape[1]
    bn = d
    return pl.pallas_call(
        _mod_kernel,
        grid=(n // bn,),
        in_specs=[pl.BlockSpec((rows, d), lambda j: (0, 0)),
                  pl.BlockSpec((d, bn), lambda j: (0, j)),
                  pl.BlockSpec((1, bn), lambda j: (0, j))],
        out_specs=pl.BlockSpec((rows, bn), lambda j: (0, j)),
        out_shape=jax.ShapeDtypeStruct((rows, n), F32),
        compiler_params=_params(("parallel",)),
        name="modulation",
    )(cond, w, b.reshape(1, n))


def _inproj_kernel(h_ref, sc_ref, sh_ref, nw_ref, w_ref, cws_ref, cbs_ref, cwg_ref, u_ref, *, ctx_tiles):
    tm = h_ref.shape[1]
    t = pl.program_id(1)
    h = h_ref[0]
    a = h * lax.rsqrt(jnp.mean(h * h, axis=-1, keepdims=True) + EPS) * nw_ref[...]
    a = a * (1.0 + sc_ref[0]) + sh_ref[0]
    ab = a.astype(BF16)
    seg = jnp.where(t < ctx_tiles, tm, GRID_W)
    pos = _iota((tm, 1), 0) & (seg - 1)
    half = CONV_W // 2
    masks = {o: (pos + o >= 0) & (pos + o < seg) for o in range(-half, half + 1) if o}

    def conv(x, cw_ref, c0, wd, cb_ref):
        acc = x * cw_ref[half:half + 1, c0:c0 + wd]
        for j in range(CONV_W):
            o = j - half
            if o == 0:
                continue
            shifted = pltpu.roll(x, (-o) % tm, 0)
            acc = acc + jnp.where(masks[o], shifted, 0.0) * cw_ref[j:j + 1, c0:c0 + wd]
        if cb_ref is not None:
            acc = acc + cb_ref[:, c0:c0 + wd]
        return _silu(acc)

    step = 2 * LANES
    for c0 in range(0, NCOLS, step):
        wd = min(step, NCOLS - c0)
        u = _dot(ab, w_ref[:, c0:c0 + wd])
        if X_OFF <= c0 < X_OFF + SSD_CONV:
            u = conv(u, cws_ref, c0 - X_OFF, wd, cbs_ref)
        elif GQ_OFF <= c0 < GQ_OFF + GDN_CONV:
            u = conv(u, cwg_ref, c0 - GQ_OFF, wd, None)
        u_ref[0, :, c0:c0 + wd] = u


def _mod_row(k, ctx_tiles, rows):
    return lambda b, t: (jnp.where(t < ctx_tiles, rows - 1, b) * 6 + k, 0, 0)


def _inproj(h, mod, norm_w, w_p, conv_s, bias_s, conv_g, *, ctx):
    bsz, t_all, d = h.shape
    tm = TOKEN_TILE
    ctx_tiles = ctx // tm
    rows = mod.shape[0] // 6
    return pl.pallas_call(
        functools.partial(_inproj_kernel, ctx_tiles=ctx_tiles),
        grid=(bsz, t_all // tm),
        in_specs=[pl.BlockSpec((1, tm, d), lambda b, t: (b, t, 0)),
                  pl.BlockSpec((1, 1, d), _mod_row(1, ctx_tiles, rows)),
                  pl.BlockSpec((1, 1, d), _mod_row(0, ctx_tiles, rows)),
                  pl.BlockSpec((1, d), lambda b, t: (0, 0)),
                  pl.BlockSpec((d, NCOLS), lambda b, t: (0, 0)),
                  pl.BlockSpec((CONV_W, SSD_CONV), lambda b, t: (0, 0)),
                  pl.BlockSpec((1, SSD_CONV), lambda b, t: (0, 0)),
                  pl.BlockSpec((CONV_W, GDN_CONV), lambda b, t: (0, 0))],
        out_specs=pl.BlockSpec((1, tm, NCOLS), lambda b, t: (b, t, 0)),
        out_shape=jax.ShapeDtypeStruct((bsz, t_all, NCOLS), F32),
        compiler_params=_params(("parallel", "parallel")),
        name="in_projection",
    )(h, mod, mod, norm_w.reshape(1, d), w_p, conv_s, bias_s.reshape(1, -1), conv_g)


def _chunk_index(rev, n_ctx, n_all):
    if not rev:
        return lambda c: c
    return lambda c: jnp.where(c < n_ctx, n_ctx - 1 - c, n_all + n_ctx - 1 - c)


def _scan_call(body, u, col_blocks, extra, out_width, scratch, *, rev, n_ctx, name):
    bsz, t_all, _ = u.shape
    cn, bb = SCAN_CHUNK, SCAN_BATCH
    assert bsz % bb == 0
    n_all = t_all // cn
    cidx = _chunk_index(rev, n_ctx, n_all)

    def tok(width, off):
        return pl.BlockSpec((bb, cn, width), lambda b, c: (b, cidx(c), off // width))

    in_specs = [tok(w, off) for w, off in col_blocks]
    in_specs += [pl.BlockSpec(a.shape, lambda b, c: (0, 0)) for a in extra]
    return pl.pallas_call(
        body,
        grid=(bsz // bb, n_all),
        in_specs=in_specs,
        out_specs=tok(out_width, 0),
        out_shape=jax.ShapeDtypeStruct((bsz, t_all, out_width), F32),
        scratch_shapes=[scratch],
        compiler_params=_params(("parallel", "arbitrary")),
        name=name,
    )(*([u] * len(col_blocks)), *extra)


def _reset_state(st_ref):
    @pl.when(pl.program_id(1) == 0)
    def _():
        st_ref[...] = jnp.zeros_like(st_ref)


def _ssd_kernel(x_ref, bc_ref, sm_ref, bias_ref, arow_ref, drow_ref, y_ref, st_ref, *, rev):
    bb, cn = x_ref.shape[0], x_ref.shape[1]
    _reset_state(st_ref)
    lane0 = SM_DT[1] if rev else SM_DT[0]
    mask = _scan_mask(cn, rev)
    ex = _expand_matrix(lane0, SSD_WIDTH)
    gw = SSD_WIDTH // SSD_GROUPS
    hpg = SSD_HEADS // SSD_GROUPS
    lane_head = _shr(_iota((1, gw), 1), HEAD)
    last = 0 if rev else cn - 1
    for i in range(bb):
        xs = x_ref[i]
        bc = bc_ref[i]
        dt = _softplus(sm_ref[i] + bias_ref[...])
        da = dt * arow_ref[...]
        acs = _cumsum_rows(da, rev)
        acs_r = acs.T
        acs_e = _dot3_exact_rhs(acs, ex)
        dt_e = _dot3_exact_rhs(dt, ex)
        tot_e = acs_e[last:last + 1]
        xdt = xs * dt_e
        xdt_b = xdt.astype(BF16)
        wst = (xdt * jnp.exp(tot_e - acs_e)).astype(BF16)
        eacs = jnp.exp(acs_e)
        etot = jnp.exp(tot_e)
        for g in range(SSD_GROUPS):
            bm = bc[:, g * SSD_STATE:(g + 1) * SSD_STATE]
            cm = bc[:, (SSD_GROUPS + g) * SSD_STATE:(SSD_GROUPS + g + 1) * SSD_STATE]
            bmb, cmb = bm.astype(BF16), cm.astype(BF16)
            cb = _dot_nt(cmb, bmb)
            xg = xdt_b[:, g * gw:(g + 1) * gw]
            yd = jnp.zeros((cn, gw), F32)
            for r in range(hpg):
                ln = lane0 + g * hpg + r
                diff = acs[:, ln:ln + 1] - acs_r[ln:ln + 1, :]
                dec = jnp.where(mask, jnp.exp(jnp.where(mask, diff, 0.0)), 0.0)
                yh = _dot((cb * dec).astype(BF16), xg)
                yd = jnp.where(lane_head == r, yh, yd)
            st = st_ref[i * SSD_GROUPS + g]
            yo = _dot(cmb, st.astype(BF16)) * eacs[:, g * gw:(g + 1) * gw]
            y = yd + yo
            if drow_ref is not None:
                y = y + drow_ref[:, g * gw:(g + 1) * gw] * xs[:, g * gw:(g + 1) * gw]
            y_ref[i, :, g * gw:(g + 1) * gw] = y
            st_ref[i * SSD_GROUPS + g] = (st * etot[:, g * gw:(g + 1) * gw]
                                          + _dot(bm.T.astype(BF16), wst[:, g * gw:(g + 1) * gw]))


def _ssd_kernel_no_skip(x_ref, bc_ref, sm_ref, bias_ref, arow_ref, y_ref, st_ref, *, rev):
    _ssd_kernel(x_ref, bc_ref, sm_ref, bias_ref, arow_ref, None, y_ref, st_ref, rev=rev)


def _ssd_scan(u, dt_bias, a_log, d_skip, *, rev, n_ctx):
    di = 1 if rev else 0
    lane0 = SM_DT[di]
    bias = jnp.zeros((1, LANES), F32).at[0, lane0:lane0 + SSD_HEADS].set(dt_bias[di])
    arow = jnp.zeros((1, LANES), F32).at[0, lane0:lane0 + SSD_HEADS].set(-jnp.exp(a_log[di].astype(F32)))
    extra = [bias, arow]
    if rev:
        body = functools.partial(_ssd_kernel_no_skip, rev=rev)
    else:
        extra.append(jnp.repeat(d_skip.astype(F32), HEAD).reshape(1, SSD_WIDTH))
        body = functools.partial(_ssd_kernel, rev=rev)
    scratch = pltpu.VMEM((SCAN_BATCH * SSD_GROUPS, SSD_STATE, SSD_WIDTH // SSD_GROUPS), F32)
    return _scan_call(body, u, [(SSD_WIDTH, X_OFF), (512, BC_OFF), (LANES, SM_OFF)], extra, SSD_WIDTH, scratch,
                      rev=rev, n_ctx=n_ctx, name="ssd_scan_bwd" if rev else "ssd_scan_fwd")


def _hgrn_kernel(q_ref, f_ref, i_ref, low_ref, y_ref, st_ref, *, rev):
    bb, cn = q_ref.shape[0], q_ref.shape[1]
    wdt = HG_WIDTH
    heads = wdt // HEAD
    _reset_state(st_ref)
    lower = low_ref[0:1]
    log_lower = low_ref[1:2]
    hb = _head_blocks(wdt)
    bd = _ones_where(hb)
    lane_head = _shr(_iota((1, wdt), 1), HEAD)
    last = 0 if rev else cn - 1

    for i in range(bb):
        fr = f_ref[i]
        qs = _silu(q_ref[i])
        v = i_ref[i]
        c = log_lower - fr
        tail = lambda z: jnp.log(1.0 + jnp.exp(-jnp.abs(z)))
        logf = (jnp.minimum(fr, 0.0) - tail(fr)) + (jnp.maximum(c, 0.0) + tail(c))
        kg = (1.0 - lower) * jax.nn.sigmoid(-fr)
        b = _cumsum_rows(logf, rev)
        vb = v.astype(BF16)

        st = st_ref[i]
        y_ref[i] = _dot_nt((qs * jnp.exp(b)).astype(BF16), st.astype(BF16))
        b_last = b[last:last + 1]
        kend = (kg * jnp.exp(b_last - b)).astype(BF16)
        st_ref[i] = st * jnp.exp(b_last) + jnp.where(hb, _dot(v.T.astype(BF16), kend), 0.0)

        def offdiag(t0, t1, s0, s1, r):
            br = b[r:r + 1]
            qp = qs[t0:t1] * jnp.exp(b[t0:t1] - br)
            kp = (kg[s0:s1] * jnp.exp(br - b[s0:s1])).astype(BF16)
            nt = t1 - t0
            qstack = jnp.concatenate([jnp.where(lane_head == h, qp, 0.0) for h in range(heads)], axis=0)
            att = _dot_nt(qstack.astype(BF16), kp)
            res = _dot(att.astype(BF16), vb[s0:s1])
            out = jnp.zeros((nt, wdt), F32)
            for h in range(heads):
                out = jnp.where(lane_head == h, res[h * nt:(h + 1) * nt], out)
            y_ref[i, t0:t1, :] += out

        def diag(t0, t1):
            n = t1 - t0
            bt = b[t0:t1]
            ti, si = _iota((n, n, wdt), 0), _iota((n, n, wdt), 1)
            m3 = (si >= ti) if rev else (si <= ti)
            diff = bt[:, None, :] - bt[None, :, :]
            w = jnp.exp(jnp.where(m3, diff, MASKED_EXPONENT))
            p = w * qs[t0:t1][:, None, :] * kg[t0:t1][None, :, :]
            r2 = _dot(p.reshape(n * n, wdt).astype(BF16), bd)
            y_ref[i, t0:t1, :] += jnp.sum(r2.reshape(n, n, wdt) * v[t0:t1][None, :, :], axis=1)

        def block(lo, hi):
            if hi - lo <= BASE:
                diag(lo, hi)
                return
            mid = (lo + hi) // 2
            if rev:
                offdiag(lo, mid, mid, hi, mid)
            else:
                offdiag(mid, hi, lo, mid, mid - 1)
            block(lo, mid)
            block(mid, hi)

        block(0, cn)


def _hgrn_scan(u, lower, *, rev, n_ctx):
    f_off = HFB_OFF if rev else HFF_OFF
    low = jnp.stack([lower, jnp.log(jnp.maximum(lower, MIN_LOWER))]).astype(F32)
    low = jnp.concatenate([low, jnp.zeros((6, HG_WIDTH), F32)], axis=0)
    scratch = pltpu.VMEM((SCAN_BATCH, HG_WIDTH, HG_WIDTH), F32)
    return _scan_call(functools.partial(_hgrn_kernel, rev=rev), u,
                      [(HG_WIDTH, HQ_OFF), (HG_WIDTH, f_off), (HG_WIDTH, HI_OFF)], [low], HG_WIDTH, scratch,
                      rev=rev, n_ctx=n_ctx, name="hgrn_scan_bwd" if rev else "hgrn_scan_fwd")


def _bdot(a, b):
    return lax.dot_general(a, b, (((2,), (1,)), ((0,), (0,))), preferred_element_type=F32)


def _unit_inverse_delta(a):
    n_rows = a.shape[-1]
    r, c = _iota((n_rows, n_rows), 0), _iota((n_rows, n_rows), 1)
    d = jnp.where((_shr(r, BASE) == _shr(c, BASE))[None], a, 0.0)
    db = d.astype(BF16)
    p = _bdot(db, db)
    n = p - d - _bdot(db, p.astype(BF16))
    e = 4
    while e < BASE:
        pb = p.astype(BF16)
        p = _bdot(pb, pb)
        n = n + p + _bdot(n.astype(BF16), p.astype(BF16))
        e *= 2
    size = BASE
    while size < n_rows:
        big = 2 * size
        off = (_shr(r, big) == _shr(c, big)) & (_shr(r, size) != _shr(c, size))
        a_off = jnp.where(off[None], a, 0.0)
        nb = n.astype(BF16)
        m = a_off + _bdot(nb, a_off.astype(BF16))
        n = n - (m + _bdot(m.astype(BF16), nb))
        size = big
    return n


def _gdn_kernel(q_ref, k_ref, v_ref, sm_ref, bias_ref, arow_ref, y_ref, st_ref, *, rev):
    bb, cn = q_ref.shape[0], q_ref.shape[1]
    wdt = GD_WIDTH
    _reset_state(st_ref)
    hb = _head_blocks(wdt)
    bd = _ones_where(hb)
    lane_head = _shr(_iota((1, wdt), 1), HEAD)
    la = SM_A[1] if rev else SM_A[0]
    lb = SM_B[1] if rev else SM_B[0]
    ex_a, ex_b = _expand_matrix(la, wdt), _expand_matrix(lb, wdt)
    mask = _scan_mask(cn, rev)
    strict = mask & (_iota((cn, cn), 0) != _iota((cn, cn), 1))
    last = 0 if rev else cn - 1
    pre, a_all = [], []
    for i in range(bb):
        q, k, v = q_ref[i], k_ref[i], v_ref[i]
        q = q * lax.rsqrt(_dot3_exact_rhs(q * q, bd) + 1e-6) * (HEAD ** -0.5)
        k = k * lax.rsqrt(_dot3_exact_rhs(k * k, bd) + 1e-6)
        sm = sm_ref[i]
        g = arow_ref[...] * _softplus(sm + bias_ref[...])
        beta = jax.nn.sigmoid(sm)
        gam = _cumsum_rows(g, rev)
        gam_r = gam.T
        gam_e = _dot3_exact_rhs(gam, ex_a)
        beta_e = _dot3_exact_rhs(beta, ex_b)
        tot_e = gam_e[last:last + 1]
        egam = jnp.exp(gam_e)
        kb = k * beta_e
        kbf = k.astype(BF16)
        rhs = jnp.concatenate([v * beta_e, kb * egam], axis=1)
        qks = []
        for h in range(GD_HEADS):
            ln = la + h
            diff = gam[:, ln:ln + 1] - gam_r[ln:ln + 1, :]
            dec = jnp.where(mask, jnp.exp(jnp.where(mask, diff, 0.0)), 0.0)
            hm = lane_head == h
            kk = _dot_nt(jnp.where(hm, kb, 0.0).astype(BF16), kbf)
            a_all.append(jnp.where(strict, kk * dec, 0.0))
            qks.append((_dot_nt(jnp.where(hm, q, 0.0).astype(BF16), kbf) * dec).astype(BF16))
        pre.append((q, k, rhs, qks, egam, gam_e, tot_e))
    n_all = _unit_inverse_delta(jnp.stack(a_all)).astype(BF16)
    for i in range(bb):
        q, k, rhs, qks, egam, gam_e, tot_e = pre[i]
        rhs_b = rhs.astype(BF16)
        u_all = jnp.zeros((cn, wdt), F32)
        w_all = jnp.zeros((cn, wdt), F32)
        for h in range(GD_HEADS):
            sol = rhs + _dot(n_all[i * GD_HEADS + h], rhs_b)
            hm = lane_head == h
            u_all = jnp.where(hm, sol[:, :wdt], u_all)
            w_all = jnp.where(hm, sol[:, wdt:], w_all)
        st = st_ref[i]
        stb = st.astype(BF16)
        v_new = u_all - _dot(w_all.astype(BF16), stb)
        vnb = v_new.astype(BF16)
        o = _dot((q * egam).astype(BF16), stb)
        for h in range(GD_HEADS):
            o = o + jnp.where(lane_head == h, _dot(qks[h], vnb), 0.0)
        y_ref[i] = o
        kend = k * jnp.exp(tot_e - gam_e)
        st_ref[i] = st * jnp.exp(tot_e) + jnp.where(hb, _dot(kend.T.astype(BF16), vnb), 0.0)


def _gdn_scan(u, dt_bias, a_log, *, rev, n_ctx):
    di = 1 if rev else 0
    la = SM_A[di]
    bias = jnp.zeros((1, LANES), F32).at[0, la:la + GD_HEADS].set(dt_bias[di])
    arow = jnp.zeros((1, LANES), F32).at[0, la:la + GD_HEADS].set(-jnp.exp(a_log[di].astype(F32)))
    scratch = pltpu.VMEM((SCAN_BATCH, GD_WIDTH, GD_WIDTH), F32)
    return _scan_call(functools.partial(_gdn_kernel, rev=rev), u,
                      [(GD_WIDTH, GQ_OFF), (GD_WIDTH, GK_OFF), (GD_WIDTH, GV_OFF), (LANES, SM_OFF)], [bias, arow],
                      GD_WIDTH, scratch, rev=rev, n_ctx=n_ctx, name="gdn_scan_bwd" if rev else "gdn_scan_fwd")


def _pack_rows(x):
    half = x.shape[1] // 2
    q = half // 2
    xr = x.astype(BF16).astype(F32)
    lo = lax.bitcast_convert_type(xr[:, :half], jnp.uint32)
    hi = lax.bitcast_convert_type(xr[:, half:], jnp.uint32)
    word = (lo >> 16) | (hi & jnp.uint32(0xFFFF0000))
    return word[:, :q], word[:, q:]


def _unpack_rows(wa, wb):
    mask = jnp.uint32(0xFFFF0000)
    f = lambda w: lax.bitcast_convert_type(w, F32)
    return f(wa << 16), f(wb << 16), f(wa & mask), f(wb & mask)


def _group_rms(y, group, bd):
    ms = _dot3_exact_rhs(y * y, bd) * (1.0 / group)
    return y * lax.rsqrt(ms + EPS)


def _outproj_kernel(ysf, ysb, yhf, yhb, ygf, ygb, z_ref, hg_ref, gg_ref, h_ref, gm_ref, scf_ref, shf_ref, gf_ref,
                    nrm_ref, wout_ref, nf_ref, rw_ref, rb_ref, sg_ref, su_ref, sd_ref,
                    h2_ref, xpa_ref, xpb_ref, idx_ref, gate_ref):
    tm = h_ref.shape[1]
    gw = SSD_WIDTH // SSD_GROUPS
    r, c = _iota((SSD_WIDTH, SSD_WIDTH), 0), _iota((SSD_WIDTH, SSD_WIDTH), 1)
    bd_s = _ones_where(_shr(r, gw) == _shr(c, gw))
    bd_h = _ones_where(_head_blocks(HG_WIDTH))
    ys = (ysf[0] + ysb[0]) * _silu(z_ref[0])
    ys = _group_rms(ys, gw, bd_s) * nrm_ref[:, 0:SSD_WIDTH]
    yh = _group_rms(yhf[0] + yhb[0], HEAD, bd_h) * nrm_ref[:, SSD_WIDTH:SSD_WIDTH + HG_WIDTH] * _silu(hg_ref[0])
    yg = _group_rms(ygf[0] + ygb[0], HEAD, bd_h) * nrm_ref[:, SSD_WIDTH + HG_WIDTH:] * _silu(gg_ref[0])
    y = jnp.concatenate([ys, yh, yg], axis=1).astype(BF16)
    h1 = h_ref[0] + gm_ref[0] * _dot(y, wout_ref[...])
    xf = h1 * lax.rsqrt(jnp.mean(h1 * h1, axis=-1, keepdims=True) + EPS) * nf_ref[...]
    xf = xf * (1.0 + scf_ref[0]) + shf_ref[0]
    xb = xf.astype(BF16)
    xpa_ref[0], xpb_ref[0] = _pack_rows(xf)
    hid = _silu(_dot(xb, sg_ref[...])) * _dot(xb, su_ref[...])
    h2_ref[0] = h1 + gf_ref[0] * _dot(hid.astype(BF16), sd_ref[...])
    scores = jax.nn.sigmoid(_dot3(xf, rw_ref[...]))
    lane = _iota((tm, LANES), 1)
    sel = jnp.where(lane < N_EXPERTS, scores + rb_ref[...], -jnp.inf)
    idx_out = jnp.zeros((tm, LANES), jnp.int32)
    gate_out = jnp.zeros((tm, LANES), F32)
    gsum = jnp.zeros((tm, 1), F32)
    for j in range(TOP_K):
        m = jnp.max(sel, axis=-1, keepdims=True)
        pick = jnp.min(jnp.where(sel == m, lane, LANES), axis=-1, keepdims=True)
        hit = lane == pick
        gv = jnp.sum(jnp.where(hit, scores, 0.0), axis=-1, keepdims=True)
        idx_out = jnp.where(lane == j, pick, idx_out)
        gate_out = jnp.where(lane == j, gv, gate_out)
        gsum = gsum + gv
        sel = jnp.where(hit, -jnp.inf, sel)
    idx_ref[0] = idx_out
    gate_ref[0] = gate_out / gsum * ROUTED_SCALE


def _outproj(ys, yh, yg, u, h, mod, nrm, w_out, norm_ffn, router_wp, router_bp, sg, su, sd, *, ctx):
    bsz, t_all, d = h.shape
    tm = TOKEN_TILE
    ctx_tiles = ctx // tm
    rows = mod.shape[0] // 6
    mrow = lambda k: pl.BlockSpec((1, 1, d), _mod_row(k, ctx_tiles, rows))
    tok = lambda w, j=0: pl.BlockSpec((1, tm, w), lambda b, t: (b, t, j))
    full = lambda a: pl.BlockSpec(a.shape, lambda b, t: (0,) * a.ndim)
    in_specs = [tok(SSD_WIDTH), tok(SSD_WIDTH), tok(HG_WIDTH), tok(HG_WIDTH), tok(GD_WIDTH), tok(GD_WIDTH),
                tok(SSD_WIDTH, Z_OFF // SSD_WIDTH), tok(HG_WIDTH, HGATE_OFF // HG_WIDTH),
                tok(GD_WIDTH, GGATE_OFF // GD_WIDTH),
                tok(d), mrow(2), mrow(4), mrow(3), mrow(5),
                full(nrm), full(w_out), full(norm_ffn), full(router_wp), full(router_bp), full(sg), full(su), full(sd)]
    pw = d // 4
    out_shape = (jax.ShapeDtypeStruct((bsz, t_all, d), F32),
                 jax.ShapeDtypeStruct((bsz, t_all, pw), jnp.uint32), jax.ShapeDtypeStruct((bsz, t_all, pw), jnp.uint32),
                 jax.ShapeDtypeStruct((bsz, t_all, LANES), jnp.int32), jax.ShapeDtypeStruct((bsz, t_all, LANES), F32))
    out_specs = (tok(d), tok(pw), tok(pw), tok(LANES), tok(LANES))
    return pl.pallas_call(
        _outproj_kernel,
        grid=(bsz, t_all // tm),
        in_specs=in_specs,
        out_specs=out_specs,
        out_shape=out_shape,
        compiler_params=_params(("parallel", "parallel")),
        name="out_projection_router",
    )(ys[0], ys[1], yh[0], yh[1], yg[0], yg[1], u, u, u, h, mod, mod, mod, mod,
      nrm, w_out, norm_ffn, router_wp, router_bp, sg, su, sd)


def _rank_kernel(idx_ref, rank_ref, cnt_ref, base_ref):
    tm = idx_ref.shape[0]

    @pl.when(pl.program_id(0) == 0)
    def _():
        base_ref[...] = jnp.zeros_like(base_ref)

    idx = idx_ref[...]
    lane = _iota((tm, LANES), 1)
    hits = [lane == idx[:, j:j + 1] for j in range(TOP_K)]
    m = jnp.zeros((tm, LANES), F32)
    for hit in hits:
        m = m + jnp.where(hit, 1.0, 0.0)
    before = _ones_where(_iota((tm, tm), 1) < _iota((tm, tm), 0))
    base = base_ref[...]
    val = _dot(before, m.astype(BF16)) + base
    out = jnp.zeros((tm, LANES), jnp.int32)
    for j, hit in enumerate(hits):
        rj = jnp.sum(jnp.where(hit, val, 0.0), axis=-1, keepdims=True)
        out = jnp.where(lane == j, rj.astype(jnp.int32), out)
    rank_ref[...] = out
    total = base + jnp.sum(m, axis=0, keepdims=True)
    base_ref[...] = total
    cnt_ref[...] = total


def _route_ranks(idx2d):
    n_tok = idx2d.shape[0]
    tm = TOKEN_TILE
    return pl.pallas_call(
        _rank_kernel,
        grid=(n_tok // tm,),
        in_specs=[pl.BlockSpec((tm, LANES), lambda i: (i, 0))],
        out_specs=(pl.BlockSpec((tm, LANES), lambda i: (i, 0)), pl.BlockSpec((1, LANES), lambda i: (0, 0))),
        out_shape=(jax.ShapeDtypeStruct((n_tok, LANES), jnp.int32), jax.ShapeDtypeStruct((1, LANES), F32)),
        scratch_shapes=[pltpu.VMEM((1, LANES), F32)],
        compiler_params=_params(("arbitrary",)),
        name="route_ranks",
    )(idx2d)


def _expert_kernel(be_ref, xa_ref, xb_ref, wg_ref, wu_ref, wd_ref, oa_ref, ob_ref, wgb_ref, wub_ref, wdb_ref):
    q = xa_ref.shape[1]
    i = pl.program_id(0)

    @pl.when((i == 0) | (be_ref[i] != be_ref[jnp.maximum(i - 1, 0)]))
    def _():
        wgb_ref[...] = wg_ref[0].astype(BF16)
        wub_ref[...] = wu_ref[0].astype(BF16)
        wdb_ref[...] = wd_ref[0].astype(BF16)

    parts = [p.astype(BF16) for p in _unpack_rows(xa_ref[...], xb_ref[...])]

    def proj(w_ref):
        acc = _dot(parts[0], w_ref[0:q, :])
        for j in range(1, 4):
            acc = acc + _dot(parts[j], w_ref[j * q:(j + 1) * q, :])
        return acc

    hid = _silu(proj(wgb_ref)) * proj(wub_ref)
    oa_ref[...], ob_ref[...] = _pack_rows(_dot(hid.astype(BF16), wdb_ref[...]))


def _expert_blocks(xsa, xsb, block_expert, w_gate, w_up, w_down):
    n_rows, pw = xsa.shape
    bm = EXPERT_ROWS
    d, e_dim = w_gate.shape[1], w_gate.shape[2]
    row = lambda: pl.BlockSpec((bm, pw), lambda i, be: (i, 0))
    grid_spec = pltpu.PrefetchScalarGridSpec(
        num_scalar_prefetch=1,
        grid=(n_rows // bm,),
        in_specs=[row(), row(),
                  pl.BlockSpec((1, d, e_dim), lambda i, be: (be[i], 0, 0)),
                  pl.BlockSpec((1, d, e_dim), lambda i, be: (be[i], 0, 0)),
                  pl.BlockSpec((1, e_dim, d), lambda i, be: (be[i], 0, 0))],
        out_specs=(row(), row()),
        scratch_shapes=[pltpu.VMEM((d, e_dim), BF16), pltpu.VMEM((d, e_dim), BF16), pltpu.VMEM((e_dim, d), BF16)],
    )
    return pl.pallas_call(
        _expert_kernel,
        grid_spec=grid_spec,
        out_shape=(jax.ShapeDtypeStruct((n_rows, pw), jnp.uint32), jax.ShapeDtypeStruct((n_rows, pw), jnp.uint32)),
        compiler_params=_params(("arbitrary",)),
        name="routed_experts",
    )(block_expert, xsa, xsb, w_gate, w_up, w_down)


def _combine_kernel(h_ref, ra_ref, rb_ref, gate_ref, gf_ref, o_ref):
    tm, q = ra_ref.shape[1], ra_ref.shape[2]
    gate = gate_ref[0]
    acc = [jnp.zeros((tm, q), F32) for _ in range(4)]
    for j in range(TOP_K):
        gj = gate[:, j:j + 1]
        for blk, part in enumerate(_unpack_rows(ra_ref[j], rb_ref[j])):
            acc[blk] = acc[blk] + part * gj
    for blk in range(4):
        cols = slice(blk * q, (blk + 1) * q)
        o_ref[0, :, cols] = h_ref[0, :, cols] + gf_ref[0, :, cols] * acc[blk]


def _combine(h2, ra, rb, gate, mod, *, ctx):
    bsz, t_all, d = h2.shape
    tm = TOKEN_TILE
    tiles = t_all // tm
    ctx_tiles = ctx // tm
    rows = mod.shape[0] // 6
    pw = ra.shape[-1]
    per = lambda: pl.BlockSpec((TOP_K, tm, pw), lambda b, t: (0, b * tiles + t, 0))
    return pl.pallas_call(
        _combine_kernel,
        grid=(bsz, tiles),
        in_specs=[pl.BlockSpec((1, tm, d), lambda b, t: (b, t, 0)), per(), per(),
                  pl.BlockSpec((1, tm, LANES), lambda b, t: (b, t, 0)),
                  pl.BlockSpec((1, 1, d), _mod_row(5, ctx_tiles, rows))],
        out_specs=pl.BlockSpec((1, tm, d), lambda b, t: (b, t, 0)),
        out_shape=jax.ShapeDtypeStruct((bsz, t_all, d), F32),
        compiler_params=_params(("parallel", "parallel")),
        name="moe_combine",
    )(h2, ra, rb, gate, mod)


def _gather_rows(xa, xb, indices):
    n = indices.shape[0]
    q = xa.shape[1]
    win = GATHER_WINDOW
    mesh = plsc.VectorSubcoreMesh(core_axis_name="core", subcore_axis_name="subcore")
    workers = mesh.num_cores * mesh.num_subcores
    per = n // (win * workers)
    assert per * win * workers == n, (n, win, workers)
    out = jax.ShapeDtypeStruct((n, q), xa.dtype)
    scratch = [pltpu.VMEM((per, win), jnp.int32), pltpu.VMEM((win, q), xa.dtype), pltpu.VMEM((win, q), xa.dtype),
               pltpu.SemaphoreType.DMA((4,))]

    @functools.partial(pl.kernel, out_type=(out, out), mesh=mesh, scratch_types=scratch)
    def gather(xa_hbm, xb_hbm, i_hbm, oa_hbm, ob_hbm, idx_vmem, buf_a, buf_b, sems):
        wid = lax.axis_index("core") * mesh.num_subcores + lax.axis_index("subcore")
        pltpu.sync_copy(i_hbm.at[wid], idx_vmem)

        @pl.loop(0, per)
        def _(s):
            rows = pl.ds(pl.multiple_of((wid * per + s) * win, win), win)
            ga = pltpu.async_copy(xa_hbm.at[idx_vmem.at[s]], buf_a, sems.at[0])
            gb = pltpu.async_copy(xb_hbm.at[idx_vmem.at[s]], buf_b, sems.at[1])
            ga.wait()
            wa = pltpu.async_copy(buf_a, oa_hbm.at[rows], sems.at[2])
            gb.wait()
            wb = pltpu.async_copy(buf_b, ob_hbm.at[rows], sems.at[3])
            wa.wait()
            wb.wait()

    return gather(xa, xb, indices.reshape(workers, per, win))


def _routed(xpa, xpb, idx, w_gate, w_up, w_down):
    bsz, t_all, pw = xpa.shape
    n_tok = bsz * t_all
    n_assign = n_tok * TOP_K
    bm = EXPERT_ROWS
    n_blocks = -(-n_assign // bm) + N_EXPERTS
    n_slots = n_blocks * bm
    idx2d = idx.reshape(n_tok, LANES)
    rank, cnt = _route_ranks(idx2d)
    counts = cnt[0, :N_EXPERTS].astype(jnp.int32)
    padded = (counts + bm - 1) // bm * bm
    pad_end = jnp.cumsum(padded)
    offset = pad_end - padded
    first = jnp.cumsum(counts) - counts
    ek = idx2d[:, :TOP_K]
    slot = jnp.take(offset, ek) + rank[:, :TOP_K]
    block_expert = jnp.minimum(
        jnp.sum(pad_end[None, :] <= (jnp.arange(n_blocks, dtype=jnp.int32) * bm)[:, None], axis=1),
        N_EXPERTS - 1).astype(jnp.int32)
    token_sorted = jnp.argsort(ek.reshape(-1)) // TOP_K
    e_slot = jnp.repeat(block_expert, bm)
    r_slot = jnp.arange(n_slots, dtype=jnp.int32) - jnp.take(offset, e_slot)
    live = r_slot < jnp.take(counts, e_slot)
    filler = jnp.arange(n_slots, dtype=jnp.int32) % n_tok
    src = jnp.where(live, jnp.take(token_sorted, jnp.clip(jnp.take(first, e_slot) + r_slot, 0, n_assign - 1)), filler)
    xsa, xsb = _gather_rows(xpa.reshape(n_tok, pw), xpb.reshape(n_tok, pw), src.astype(jnp.int32))
    oa, ob = _expert_blocks(xsa, xsb, block_expert, w_gate, w_up, w_down)
    ra, rb = _gather_rows(oa, ob, slot.T.reshape(-1))
    return ra.reshape(TOP_K, n_tok, pw), rb.reshape(TOP_K, n_tok, pw)


def _final_kernel(h_ref, w_ref, o_ref):
    h = h_ref[0]
    o_ref[0] = h * lax.rsqrt(jnp.mean(h * h, axis=-1, keepdims=True) + EPS) * w_ref[...]


def _final_norm(h, w, *, ctx):
    bsz, t_all, d = h.shape
    tm = TOKEN_TILE
    seq = t_all - ctx
    ctx_tiles = ctx // tm
    return pl.pallas_call(
        _final_kernel,
        grid=(bsz, seq // tm),
        in_specs=[pl.BlockSpec((1, tm, d), lambda b, t: (b, t + ctx_tiles, 0)),
                  pl.BlockSpec((1, d), lambda b, t: (0, 0))],
        out_specs=pl.BlockSpec((1, tm, d), lambda b, t: (b, t, 0)),
        out_shape=jax.ShapeDtypeStruct((bsz, seq, d), F32),
        compiler_params=_params(("parallel", "parallel")),
        name="final_norm",
    )(h, w.reshape(1, d))


def kernel(x, c, ctx, c_ctx, w_mod, b_mod, norm_mix, w_in, ssd_conv_w, ssd_conv_b, ssd_dt_bias, ssd_a_log, ssd_d,
           ssd_norm, hgrn_lb, hgrn_norm, gdn_conv_w, gdn_dt_bias, gdn_a_log, gdn_norm, w_out, norm_ffn, router_w,
           router_bias, exp_gate, exp_up, exp_down, sh_gate, sh_up, sh_down, norm_final):
    bsz, seq, d = x.shape
    n_ctx_tok = ctx.shape[1]
    depth = w_in.shape[0]
    assert d == D_MODEL and n_ctx_tok == TOKEN_TILE and seq % TOKEN_TILE == 0 and TOKEN_TILE % SCAN_CHUNK == 0
    assert TOKEN_TILE % GRID_W == 0 and seq % GRID_W == 0
    n_ctx = n_ctx_tok // SCAN_CHUNK

    p_lb = jax.nn.softmax(hgrn_lb.astype(F32), axis=0)
    lower_all = jnp.cumsum(p_lb, axis=0) - p_lb[0]

    rows = -(-(bsz + 1) // 8) * 8
    cond = jnp.zeros((rows, d), F32).at[:bsz].set(c).at[rows - 1].set(c_ctx)

    h = jnp.concatenate([ctx, x], axis=1)
    for l in range(depth):
        mod = _modulation(cond, w_mod[l], b_mod[l]).reshape(rows * 6, 1, d)
        w_p = _permute_w_in(w_in[l]).astype(BF16)
        u = _inproj(h, mod, norm_mix[l], w_p, ssd_conv_w[l], ssd_conv_b[l], gdn_conv_w[l], ctx=n_ctx_tok)
        ys = [_ssd_scan(u, ssd_dt_bias[l], ssd_a_log[l], ssd_d[l], rev=r, n_ctx=n_ctx) for r in (False, True)]
        yh = [_hgrn_scan(u, lower_all[l], rev=r, n_ctx=n_ctx) for r in (False, True)]
        yg = [_gdn_scan(u, gdn_dt_bias[l], gdn_a_log[l], rev=r, n_ctx=n_ctx) for r in (False, True)]
        nrm = jnp.concatenate([ssd_norm[l], jnp.tile(hgrn_norm[l], HG_WIDTH // HEAD),
                               jnp.tile(gdn_norm[l], GD_WIDTH // HEAD)]).astype(F32).reshape(1, d)
        rwp = jnp.pad(router_w[l].astype(F32), ((0, 0), (0, LANES - N_EXPERTS)))
        rbp = jnp.pad(router_bias[l].astype(F32), (0, LANES - N_EXPERTS)).reshape(1, LANES)
        h2, xpa, xpb, idx, gate = _outproj(ys, yh, yg, u, h, mod, nrm, w_out[l].astype(BF16),
                                           norm_ffn[l].reshape(1, d), rwp, rbp, sh_gate[l].astype(BF16),
                                           sh_up[l].astype(BF16), sh_down[l].astype(BF16), ctx=n_ctx_tok)
        ra, rb = _routed(xpa, xpb, idx, exp_gate[l], exp_up[l], exp_down[l])
        h = _combine(h2, ra, rb, gate, mod, ctx=n_ctx_tok)
    return _final_norm(h, norm_final, ctx=n_ctx_tok)
```

```python
import functools

import numpy as np
import jax
import jax.numpy as jnp
from jax import lax
from jax.experimental import pallas as pl
from jax.experimental.pallas import tpu as pltpu
from jax.experimental.pallas import tpu_sc as plsc

F32 = jnp.float32
BF16 = jnp.bfloat16

D_MODEL = 1024
GRID_W = 64
CONV_W = 5
EPS = 1e-6
MIN_LOWER = 1e-30
MASKED_EXPONENT = -1e30
HEAD = 64
SSD_HEADS = 8
SSD_WIDTH = 512
SSD_STATE = 128
SSD_GROUPS = 2
HG_WIDTH = 256
GD_WIDTH = 256
GD_HEADS = 4
N_EXPERTS = 64
TOP_K = 8
EXPERT_DIM = 256
ROUTED_SCALE = 2.5

LANES = 128
SCAN_CHUNK = 128
SCAN_BATCH = 4
BASE = 16
TOKEN_TILE = 256
EXPERT_ROWS = 1024
GATHER_WINDOW = 128
VMEM_LIMIT = 56 * 1024 * 1024

Z_OFF, X_OFF, BC_OFF = 0, 512, 1024
HQ_OFF, HFF_OFF, HFB_OFF, HI_OFF, HGATE_OFF = 1536, 1792, 2048, 2304, 2560
GQ_OFF, GK_OFF, GV_OFF, GGATE_OFF = 2816, 3072, 3328, 3584
SM_OFF = 3840
NCOLS = 3968
SSD_CONV = 1024
GDN_CONV = 768
SM_DT = (0, 8)
SM_A = (16, 20)
SM_B = (24, 28)


def _permute_w_in(w):
    d = w.shape[0]
    parts = [w[:, 0:1536],
             w[:, 1552:2832],
             w[:, 2832:3600],
             w[:, 3616:3872],
             w[:, 1536:1552],
             w[:, 3600:3616],
             jnp.zeros((d, LANES - 32), w.dtype)]
    return jnp.concatenate(parts, axis=1)


def _dot(a, b):
    return lax.dot_general(a, b, (((1,), (0,)), ((), ())), preferred_element_type=F32)


def _dot_nt(a, b):
    return lax.dot_general(a, b, (((1,), (1,)), ((), ())), preferred_element_type=F32)


def _split(a):
    hi = a.astype(BF16)
    lo = (a - hi.astype(F32)).astype(BF16)
    return hi, lo


def _dot3(a, b):
    ah, al = _split(a)
    bh, bl = _split(b)
    return _dot(ah, bh) + (_dot(ah, bl) + _dot(al, bh))


def _dot3_exact_rhs(a, b_bf16):
    ah, al = _split(a)
    return _dot(ah, b_bf16) + _dot(al, b_bf16)


def _silu(x):
    return x * jax.nn.sigmoid(x)


def _softplus(x):
    return jnp.maximum(x, 0.0) + jnp.log1p(jnp.exp(-jnp.abs(x)))


def _log_sigmoid(x):
    return jnp.minimum(x, 0.0) - jnp.log1p(jnp.exp(-jnp.abs(x)))


def _params(sem):
    return pltpu.CompilerParams(dimension_semantics=sem, vmem_limit_bytes=VMEM_LIMIT)


def _iota(shape, dim):
    return lax.broadcasted_iota(jnp.int32, shape, dim)


def _ones_where(mask):
    return jnp.where(mask, 1.0, 0.0).astype(BF16)


def _shr(x, div):
    return jnp.right_shift(x, int(np.log2(div)))


def _scan_mask(n, rev):
    r, c = _iota((n, n), 0), _iota((n, n), 1)
    return (c >= r) if rev else (c <= r)


def _expand_matrix(lane0, width):
    r, c = _iota((LANES, width), 0), _iota((LANES, width), 1)
    return _ones_where(r == lane0 + _shr(c, HEAD))


def _head_blocks(n):
    r, c = _iota((n, n), 0), _iota((n, n), 1)
    return _shr(r, HEAD) == _shr(c, HEAD)


def _cumsum_rows(x, rev):
    tri = _ones_where(_scan_mask(x.shape[0], rev))
    xh, xl = _split(x)
    return _dot(tri, xh) + _dot(tri, xl)


def _mod_kernel(s_ref, w_ref, b_ref, o_ref):
    s = _silu(s_ref[...])
    o_ref[...] = _dot(s.astype(BF16), w_ref[...].astype(BF16)) + b_ref[...]


def _modulation(cond, w, b):
    rows, d = cond.shape
    n = w.shape[1]
    bn = d
    return pl.pallas_call(
        _mod_kernel,
        grid=(n // bn,),
        in_specs=[pl.BlockSpec((rows, d), lambda j: (0, 0)),
                  pl.BlockSpec((d, bn), lambda j: (0, j)),
                  pl.BlockSpec((1, bn), lambda j: (0, j))],
        out_specs=pl.BlockSpec((rows, bn), lambda j: (0, j)),
        out_shape=jax.ShapeDtypeStruct((rows, n), F32),
        compiler_params=_params(("parallel",)),
        name="modulation",
    )(cond, w, b.reshape(1, n))


def _inproj_kernel(h_ref, sc_ref, sh_ref, nw_ref, w_ref, cws_ref, cbs_ref, cwg_ref, u_ref, *, ctx_tiles):
    tm = h_ref.shape[1]
    t = pl.program_id(1)
    h = h_ref[0]
    a = h * lax.rsqrt(jnp.mean(h * h, axis=-1, keepdims=True) + EPS) * nw_ref[...]
    a = a * (1.0 + sc_ref[0]) + sh_ref[0]
    ab = a.astype(BF16)
    seg = jnp.where(t < ctx_tiles, tm, GRID_W)
    pos = _iota((tm, 1), 0) & (seg - 1)
    half = CONV_W // 2
    masks = {o: (pos + o >= 0) & (pos + o < seg) for o in range(-half, half + 1) if o}

    def conv(x, cw_ref, c0, wd, cb_ref):
        acc = x * cw_ref[half:half + 1, c0:c0 + wd]
        for j in range(CONV_W):
            o = j - half
            if o == 0:
                continue
            shifted = pltpu.roll(x, (-o) % tm, 0)
            acc = acc + jnp.where(masks[o], shifted, 0.0) * cw_ref[j:j + 1, c0:c0 + wd]
        if cb_ref is not None:
            acc = acc + cb_ref[:, c0:c0 + wd]
        return _silu(acc)

    step = 2 * LANES
    for c0 in range(0, NCOLS, step):
        wd = min(step, NCOLS - c0)
        u = _dot(ab, w_ref[:, c0:c0 + wd])
        if X_OFF <= c0 < X_OFF + SSD_CONV:
            u = conv(u, cws_ref, c0 - X_OFF, wd, cbs_ref)
        elif GQ_OFF <= c0 < GQ_OFF + GDN_CONV:
            u = conv(u, cwg_ref, c0 - GQ_OFF, wd, None)
        u_ref[0, :, c0:c0 + wd] = u


def _mod_row(k, ctx_tiles, rows):
    return lambda b, t: (jnp.where(t < ctx_tiles, rows - 1, b) * 6 + k, 0, 0)


def _inproj(h, mod, norm_w, w_p, conv_s, bias_s, conv_g, *, ctx):
    bsz, t_all, d = h.shape
    tm = TOKEN_TILE
    ctx_tiles = ctx // tm
    rows = mod.shape[0] // 6
    return pl.pallas_call(
        functools.partial(_inproj_kernel, ctx_tiles=ctx_tiles),
        grid=(bsz, t_all // tm),
        in_specs=[pl.BlockSpec((1, tm, d), lambda b, t: (b, t, 0)),
                  pl.BlockSpec((1, 1, d), _mod_row(1, ctx_tiles, rows)),
                  pl.BlockSpec((1, 1, d), _mod_row(0, ctx_tiles, rows)),
                  pl.BlockSpec((1, d), lambda b, t: (0, 0)),
                  pl.BlockSpec((d, NCOLS), lambda b, t: (0, 0)),
                  pl.BlockSpec((CONV_W, SSD_CONV), lambda b, t: (0, 0)),
                  pl.BlockSpec((1, SSD_CONV), lambda b, t: (0, 0)),
                  pl.BlockSpec((CONV_W, GDN_CONV), lambda b, t: (0, 0))],
        out_specs=pl.BlockSpec((1, tm, NCOLS), lambda b, t: (b, t, 0)),
        out_shape=jax.ShapeDtypeStruct((bsz, t_all, NCOLS), F32),
        compiler_params=_params(("parallel", "parallel")),
        name="in_projection",
    )(h, mod, mod, norm_w.reshape(1, d), w_p, conv_s, bias_s.reshape(1, -1), conv_g)


def _chunk_index(rev, n_ctx, n_all):
    if not rev:
        return lambda c: c
    return lambda c: jnp.where(c < n_ctx, n_ctx - 1 - c, n_all + n_ctx - 1 - c)


def _scan_call(body, u, col_blocks, extra, out_width, scratch, *, rev, n_ctx, name):
    bsz, t_all, _ = u.shape
    cn, bb = SCAN_CHUNK, SCAN_BATCH
    assert bsz % bb == 0
    n_all = t_all // cn
    cidx = _chunk_index(rev, n_ctx, n_all)

    def tok(width, off):
        return pl.BlockSpec((bb, cn, width), lambda b, c: (b, cidx(c), off // width))

    in_specs = [tok(w, off) for w, off in col_blocks]
    in_specs += [pl.BlockSpec(a.shape, lambda b, c: (0, 0)) for a in extra]
    return pl.pallas_call(
        body,
        grid=(bsz // bb, n_all),
        in_specs=in_specs,
        out_specs=tok(out_width, 0),
        out_shape=jax.ShapeDtypeStruct((bsz, t_all, out_width), F32),
        scratch_shapes=[scratch],
        compiler_params=_params(("parallel", "arbitrary")),
        name=name,
    )(*([u] * len(col_blocks)), *extra)


def _reset_state(st_ref):
    @pl.when(pl.program_id(1) == 0)
    def _():
        st_ref[...] = jnp.zeros_like(st_ref)


def _ssd_kernel(x_ref, bc_ref, sm_ref, bias_ref, arow_ref, drow_ref, y_ref, st_ref, *, rev):
    bb, cn = x_ref.shape[0], x_ref.shape[1]
    _reset_state(st_ref)
    lane0 = SM_DT[1] if rev else SM_DT[0]
    mask = _scan_mask(cn, rev)
    ex = _expand_matrix(lane0, SSD_WIDTH)
    gw = SSD_WIDTH // SSD_GROUPS
    hpg = SSD_HEADS // SSD_GROUPS
    lane_head = _shr(_iota((1, gw), 1), HEAD)
    last = 0 if rev else cn - 1
    for i in range(bb):
        xs = x_ref[i]
        bc = bc_ref[i]
        dt = _softplus(sm_ref[i] + bias_ref[...])
        da = dt * arow_ref[...]
        acs = _cumsum_rows(da, rev)
        acs_r = acs.T
        acs_e = _dot3_exact_rhs(acs, ex)
        dt_e = _dot3_exact_rhs(dt, ex)
        tot_e = acs_e[last:last + 1]
        xdt = xs * dt_e
        xdt_b = xdt.astype(BF16)
        wst = (xdt * jnp.exp(tot_e - acs_e)).astype(BF16)
        eacs = jnp.exp(acs_e)
        etot = jnp.exp(tot_e)
        for g in range(SSD_GROUPS):
            bm = bc[:, g * SSD_STATE:(g + 1) * SSD_STATE]
            cm = bc[:, (SSD_GROUPS + g) * SSD_STATE:(SSD_GROUPS + g + 1) * SSD_STATE]
            bmb, cmb = bm.astype(BF16), cm.astype(BF16)
            cb = _dot_nt(cmb, bmb)
            xg = xdt_b[:, g * gw:(g + 1) * gw]
            yd = jnp.zeros((cn, gw), F32)
            for r in range(hpg):
                ln = lane0 + g * hpg + r
                diff = acs[:, ln:ln + 1] - acs_r[ln:ln + 1, :]
                dec = jnp.where(mask, jnp.exp(jnp.where(mask, diff, 0.0)), 0.0)
                yh = _dot((cb * dec).astype(BF16), xg)
                yd = jnp.where(lane_head == r, yh, yd)
            st = st_ref[i * SSD_GROUPS + g]
            yo = _dot(cmb, st.astype(BF16)) * eacs[:, g * gw:(g + 1) * gw]
            y = yd + yo
            if drow_ref is not None:
                y = y + drow_ref[:, g * gw:(g + 1) * gw] * xs[:, g * gw:(g + 1) * gw]
            y_ref[i, :, g * gw:(g + 1) * gw] = y
            st_ref[i * SSD_GROUPS + g] = (st * etot[:, g * gw:(g + 1) * gw]
                                          + _dot(bm.T.astype(BF16), wst[:, g * gw:(g + 1) * gw]))


def _ssd_kernel_no_skip(x_ref, bc_ref, sm_ref, bias_ref, arow_ref, y_ref, st_ref, *, rev):
    _ssd_kernel(x_ref, bc_ref, sm_ref, bias_ref, arow_ref, None, y_ref, st_ref, rev=rev)


def _ssd_scan(u, dt_bias, a_log, d_skip, *, rev, n_ctx):
    di = 1 if rev else 0
    lane0 = SM_DT[di]
    bias = jnp.zeros((1, LANES), F32).at[0, lane0:lane0 + SSD_HEADS].set(dt_bias[di])
    arow = jnp.zeros((1, LANES), F32).at[0, lane0:lane0 + SSD_HEADS].set(-jnp.exp(a_log[di].astype(F32)))
    extra = [bias, arow]
    if rev:
        body = functools.partial(_ssd_kernel_no_skip, rev=rev)
    else:
        extra.append(jnp.repeat(d_skip.astype(F32), HEAD).reshape(1, SSD_WIDTH))
        body = functools.partial(_ssd_kernel, rev=rev)
    scratch = pltpu.VMEM((SCAN_BATCH * SSD_GROUPS, SSD_STATE, SSD_WIDTH // SSD_GROUPS), F32)
    return _scan_call(body, u, [(SSD_WIDTH, X_OFF), (512, BC_OFF), (LANES, SM_OFF)], extra, SSD_WIDTH, scratch,
                      rev=rev, n_ctx=n_ctx, name="ssd_scan_bwd" if rev else "ssd_scan_fwd")


def _hgrn_kernel(q_ref, f_ref, i_ref, low_ref, y_ref, st_ref, *, rev):
    bb, cn = q_ref.shape[0], q_ref.shape[1]
    wdt = HG_WIDTH
    heads = wdt // HEAD
    _reset_state(st_ref)
    lower = low_ref[0:1]
    log_lower = low_ref[1:2]
    hb = _head_blocks(wdt)
    bd = _ones_where(hb)
    lane_head = _shr(_iota((1, wdt), 1), HEAD)
    last = 0 if rev else cn - 1

    for i in range(bb):
        fr = f_ref[i]
        qs = _silu(q_ref[i])
        v = i_ref[i]
        c = log_lower - fr
        tail = lambda z: jnp.log(1.0 + jnp.exp(-jnp.abs(z)))
        logf = (jnp.minimum(fr, 0.0) - tail(fr)) + (jnp.maximum(c, 0.0) + tail(c))
        kg = (1.0 - lower) * jax.nn.sigmoid(-fr)
        b = _cumsum_rows(logf, rev)
        vb = v.astype(BF16)

        st = st_ref[i]
        y_ref[i] = _dot_nt((qs * jnp.exp(b)).astype(BF16), st.astype(BF16))
        b_last = b[last:last + 1]
        kend = (kg * jnp.exp(b_last - b)).astype(BF16)
        st_ref[i] = st * jnp.exp(b_last) + jnp.where(hb, _dot(v.T.astype(BF16), kend), 0.0)

        def offdiag(t0, t1, s0, s1, r):
            br = b[r:r + 1]
            qp = qs[t0:t1] * jnp.exp(b[t0:t1] - br)
            kp = (kg[s0:s1] * jnp.exp(br - b[s0:s1])).astype(BF16)
            nt = t1 - t0
            qstack = jnp.concatenate([jnp.where(lane_head == h, qp, 0.0) for h in range(heads)], axis=0)
            att = _dot_nt(qstack.astype(BF16), kp)
            res = _dot(att.astype(BF16), vb[s0:s1])
            out = jnp.zeros((nt, wdt), F32)
            for h in range(heads):
                out = jnp.where(lane_head == h, res[h * nt:(h + 1) * nt], out)
            y_ref[i, t0:t1, :] += out

        def diag(t0, t1):
            n = t1 - t0
            bt = b[t0:t1]
            ti, si = _iota((n, n, wdt), 0), _iota((n, n, wdt), 1)
            m3 = (si >= ti) if rev else (si <= ti)
            diff = bt[:, None, :] - bt[None, :, :]
            w = jnp.exp(jnp.where(m3, diff, MASKED_EXPONENT))
            p = w * qs[t0:t1][:, None, :] * kg[t0:t1][None, :, :]
            r2 = _dot(p.reshape(n * n, wdt).astype(BF16), bd)
            y_ref[i, t0:t1, :] += jnp.sum(r2.reshape(n, n, wdt) * v[t0:t1][None, :, :], axis=1)

        def block(lo, hi):
            if hi - lo <= BASE:
                diag(lo, hi)
                return
            mid = (lo + hi) // 2
            if rev:
                offdiag(lo, mid, mid, hi, mid)
            else:
                offdiag(mid, hi, lo, mid, mid - 1)
            block(lo, mid)
            block(mid, hi)

        block(0, cn)


def _hgrn_scan(u, lower, *, rev, n_ctx):
    f_off = HFB_OFF if rev else HFF_OFF
    low = jnp.stack([lower, jnp.log(jnp.maximum(lower, MIN_LOWER))]).astype(F32)
    low = jnp.concatenate([low, jnp.zeros((6, HG_WIDTH), F32)], axis=0)
    scratch = pltpu.VMEM((SCAN_BATCH, HG_WIDTH, HG_WIDTH), F32)
    return _scan_call(functools.partial(_hgrn_kernel, rev=rev), u,
                      [(HG_WIDTH, HQ_OFF), (HG_WIDTH, f_off), (HG_WIDTH, HI_OFF)], [low], HG_WIDTH, scratch,
                      rev=rev, n_ctx=n_ctx, name="hgrn_scan_bwd" if rev else "hgrn_scan_fwd")


def _bdot(a, b):
    return lax.dot_general(a, b, (((2,), (1,)), ((0,), (0,))), preferred_element_type=F32)


def _unit_inverse_delta(a):
    n_rows = a.shape[-1]
    r, c = _iota((n_rows, n_rows), 0), _iota((n_rows, n_rows), 1)
    d = jnp.where((_shr(r, BASE) == _shr(c, BASE))[None], a, 0.0)
    db = d.astype(BF16)
    p = _bdot(db, db)
    n = p - d - _bdot(db, p.astype(BF16))
    e = 4
    while e < BASE:
        pb = p.astype(BF16)
        p = _bdot(pb, pb)
        n = n + p + _bdot(n.astype(BF16), p.astype(BF16))
        e *= 2
    size = BASE
    while size < n_rows:
        big = 2 * size
        off = (_shr(r, big) == _shr(c, big)) & (_shr(r, size) != _shr(c, size))
        a_off = jnp.where(off[None], a, 0.0)
        nb = n.astype(BF16)
        m = a_off + _bdot(nb, a_off.astype(BF16))
        n = n - (m + _bdot(m.astype(BF16), nb))
        size = big
    return n


def _gdn_kernel(q_ref, k_ref, v_ref, sm_ref, bias_ref, arow_ref, y_ref, st_ref, *, rev):
    bb, cn = q_ref.shape[0], q_ref.shape[1]
    wdt = GD_WIDTH
    _reset_state(st_ref)
    hb = _head_blocks(wdt)
    bd = _ones_where(hb)
    lane_head = _shr(_iota((1, wdt), 1), HEAD)
    la = SM_A[1] if rev else SM_A[0]
    lb = SM_B[1] if rev else SM_B[0]
    ex_a, ex_b = _expand_matrix(la, wdt), _expand_matrix(lb, wdt)
    mask = _scan_mask(cn, rev)
    strict = mask & (_iota((cn, cn), 0) != _iota((cn, cn), 1))
    last = 0 if rev else cn - 1
    pre, a_all = [], []
    for i in range(bb):
        q, k, v = q_ref[i], k_ref[i], v_ref[i]
        q = q * lax.rsqrt(_dot3_exact_rhs(q * q, bd) + 1e-6) * (HEAD ** -0.5)
        k = k * lax.rsqrt(_dot3_exact_rhs(k * k, bd) + 1e-6)
        sm = sm_ref[i]
        g = arow_ref[...] * _softplus(sm + bias_ref[...])
        beta = jax.nn.sigmoid(sm)
        gam = _cumsum_rows(g, rev)
        gam_r = gam.T
        gam_e = _dot3_exact_rhs(gam, ex_a)
        beta_e = _dot3_exact_rhs(beta, ex_b)
        tot_e = gam_e[last:last + 1]
        egam = jnp.exp(gam_e)
        kb = k * beta_e
        kbf = k.astype(BF16)
        rhs = jnp.concatenate([v * beta_e, kb * egam], axis=1)
        qks = []
        for h in range(GD_HEADS):
            ln = la + h
            diff = gam[:, ln:ln + 1] - gam_r[ln:ln + 1, :]
            dec = jnp.where(mask, jnp.exp(jnp.where(mask, diff, 0.0)), 0.0)
            hm = lane_head == h
            kk = _dot_nt(jnp.where(hm, kb, 0.0).astype(BF16), kbf)
            a_all.append(jnp.where(strict, kk * dec, 0.0))
            qks.append((_dot_nt(jnp.where(hm, q, 0.0).astype(BF16), kbf) * dec).astype(BF16))
        pre.append((q, k, rhs, qks, egam, gam_e, tot_e))
    n_all = _unit_inverse_delta(jnp.stack(a_all)).astype(BF16)
    for i in range(bb):
        q, k, rhs, qks, egam, gam_e, tot_e = pre[i]
        rhs_b = rhs.astype(BF16)
        u_all = jnp.zeros((cn, wdt), F32)
        w_all = jnp.zeros((cn, wdt), F32)
        for h in range(GD_HEADS):
            sol = rhs + _dot(n_all[i * GD_HEADS + h], rhs_b)
            hm = lane_head == h
            u_all = jnp.where(hm, sol[:, :wdt], u_all)
            w_all = jnp.where(hm, sol[:, wdt:], w_all)
        st = st_ref[i]
        stb = st.astype(BF16)
        v_new = u_all - _dot(w_all.astype(BF16), stb)
        vnb = v_new.astype(BF16)
        o = _dot((q * egam).astype(BF16), stb)
        for h in range(GD_HEADS):
            o = o + jnp.where(lane_head == h, _dot(qks[h], vnb), 0.0)
        y_ref[i] = o
        kend = k * jnp.exp(tot_e - gam_e)
        st_ref[i] = st * jnp.exp(tot_e) + jnp.where(hb, _dot(kend.T.astype(BF16), vnb), 0.0)


def _gdn_scan(u, dt_bias, a_log, *, rev, n_ctx):
    di = 1 if rev else 0
    la = SM_A[di]
    bias = jnp.zeros((1, LANES), F32).at[0, la:la + GD_HEADS].set(dt_bias[di])
    arow = jnp.zeros((1, LANES), F32).at[0, la:la + GD_HEADS].set(-jnp.exp(a_log[di].astype(F32)))
    scratch = pltpu.VMEM((SCAN_BATCH, GD_WIDTH, GD_WIDTH), F32)
    return _scan_call(functools.partial(_gdn_kernel, rev=rev), u,
                      [(GD_WIDTH, GQ_OFF), (GD_WIDTH, GK_OFF), (GD_WIDTH, GV_OFF), (LANES, SM_OFF)], [bias, arow],
                      GD_WIDTH, scratch, rev=rev, n_ctx=n_ctx, name="gdn_scan_bwd" if rev else "gdn_scan_fwd")


def _pack_rows(x):
    half = x.shape[1] // 2
    q = half // 2
    xr = x.astype(BF16).astype(F32)
    lo = lax.bitcast_convert_type(xr[:, :half], jnp.uint32)
    hi = lax.bitcast_convert_type(xr[:, half:], jnp.uint32)
    word = (lo >> 16) | (hi & jnp.uint32(0xFFFF0000))
    return word[:, :q], word[:, q:]


def _unpack_rows(wa, wb):
    mask = jnp.uint32(0xFFFF0000)
    f = lambda w: lax.bitcast_convert_type(w, F32)
    return f(wa << 16), f(wb << 16), f(wa & mask), f(wb & mask)


def _group_rms(y, group, bd):
    ms = _dot3_exact_rhs(y * y, bd) * (1.0 / group)
    return y * lax.rsqrt(ms + EPS)


def _outproj_kernel(ysf, ysb, yhf, yhb, ygf, ygb, z_ref, hg_ref, gg_ref, h_ref, gm_ref, scf_ref, shf_ref, gf_ref,
                    nrm_ref, wout_ref, nf_ref, rw_ref, rb_ref, sg_ref, su_ref, sd_ref,
                    h2_ref, xpa_ref, xpb_ref, idx_ref, gate_ref):
    tm = h_ref.shape[1]
    gw = SSD_WIDTH // SSD_GROUPS
    r, c = _iota((SSD_WIDTH, SSD_WIDTH), 0), _iota((SSD_WIDTH, SSD_WIDTH), 1)
    bd_s = _ones_where(_shr(r, gw) == _shr(c, gw))
    bd_h = _ones_where(_head_blocks(HG_WIDTH))
    ys = (ysf[0] + ysb[0]) * _silu(z_ref[0])
    ys = _group_rms(ys, gw, bd_s) * nrm_ref[:, 0:SSD_WIDTH]
    yh = _group_rms(yhf[0] + yhb[0], HEAD, bd_h) * nrm_ref[:, SSD_WIDTH:SSD_WIDTH + HG_WIDTH] * _silu(hg_ref[0])
    yg = _group_rms(ygf[0] + ygb[0], HEAD, bd_h) * nrm_ref[:, SSD_WIDTH + HG_WIDTH:] * _silu(gg_ref[0])
    y = jnp.concatenate([ys, yh, yg], axis=1).astype(BF16)
    h1 = h_ref[0] + gm_ref[0] * _dot(y, wout_ref[...])
    xf = h1 * lax.rsqrt(jnp.mean(h1 * h1, axis=-1, keepdims=True) + EPS) * nf_ref[...]
    xf = xf * (1.0 + scf_ref[0]) + shf_ref[0]
    xb = xf.astype(BF16)
    xpa_ref[0], xpb_ref[0] = _pack_rows(xf)
    hid = _silu(_dot(xb, sg_ref[...])) * _dot(xb, su_ref[...])
    h2_ref[0] = h1 + gf_ref[0] * _dot(hid.astype(BF16), sd_ref[...])
    scores = jax.nn.sigmoid(_dot3(xf, rw_ref[...]))
    lane = _iota((tm, LANES), 1)
    sel = jnp.where(lane < N_EXPERTS, scores + rb_ref[...], -jnp.inf)
    rank_code = (LANES - lane).astype(F32)
    idx_f = jnp.zeros((tm, LANES), F32)
    gate_out = jnp.zeros((tm, LANES), F32)
    gsum = jnp.zeros((tm, 1), F32)
    for j in range(TOP_K):
        m = jnp.max(sel, axis=-1, keepdims=True)
        code = jnp.max(jnp.where(sel == m, rank_code, 0.0), axis=-1, keepdims=True)
        hit = rank_code == code
        gv = jnp.sum(jnp.where(hit, scores, 0.0), axis=-1, keepdims=True)
        idx_f = jnp.where(lane == j, LANES - code, idx_f)
        gate_out = jnp.where(lane == j, gv, gate_out)
        gsum = gsum + gv
        sel = jnp.where(hit, -jnp.inf, sel)
    idx_ref[0] = idx_f.astype(jnp.int32)
    gate_ref[0] = gate_out / gsum * ROUTED_SCALE


def _outproj(ys, yh, yg, u, h, mod, nrm, w_out, norm_ffn, router_wp, router_bp, sg, su, sd, *, ctx):
    bsz, t_all, d = h.shape
    tm = TOKEN_TILE
    ctx_tiles = ctx // tm
    rows = mod.shape[0] // 6
    mrow = lambda k: pl.BlockSpec((1, 1, d), _mod_row(k, ctx_tiles, rows))
    tok = lambda w, j=0: pl.BlockSpec((1, tm, w), lambda b, t: (b, t, j))
    full = lambda a: pl.BlockSpec(a.shape, lambda b, t: (0,) * a.ndim)
    in_specs = [tok(SSD_WIDTH), tok(SSD_WIDTH), tok(HG_WIDTH), tok(HG_WIDTH), tok(GD_WIDTH), tok(GD_WIDTH),
                tok(SSD_WIDTH, Z_OFF // SSD_WIDTH), tok(HG_WIDTH, HGATE_OFF // HG_WIDTH),
                tok(GD_WIDTH, GGATE_OFF // GD_WIDTH),
                tok(d), mrow(2), mrow(4), mrow(3), mrow(5),
                full(nrm), full(w_out), full(norm_ffn), full(router_wp), full(router_bp), full(sg), full(su), full(sd)]
    pw = d // 4
    out_shape = (jax.ShapeDtypeStruct((bsz, t_all, d), F32),
                 jax.ShapeDtypeStruct((bsz, t_all, pw), jnp.uint32), jax.ShapeDtypeStruct((bsz, t_all, pw), jnp.uint32),
                 jax.ShapeDtypeStruct((bsz, t_all, LANES), jnp.int32), jax.ShapeDtypeStruct((bsz, t_all, LANES), F32))
    out_specs = (tok(d), tok(pw), tok(pw), tok(LANES), tok(LANES))
    return pl.pallas_call(
        _outproj_kernel,
        grid=(bsz, t_all // tm),
        in_specs=in_specs,
        out_specs=out_specs,
        out_shape=out_shape,
        compiler_params=_params(("parallel", "parallel")),
        name="out_projection_router",
    )(ys[0], ys[1], yh[0], yh[1], yg[0], yg[1], u, u, u, h, mod, mod, mod, mod,
      nrm, w_out, norm_ffn, router_wp, router_bp, sg, su, sd)


def _rank_kernel(idx_ref, rank_ref, cnt_ref, base_ref):
    tm = idx_ref.shape[0]

    @pl.when(pl.program_id(0) == 0)
    def _():
        base_ref[...] = jnp.zeros_like(base_ref)

    idx = idx_ref[...]
    lane = _iota((tm, LANES), 1)
    hits = [lane == idx[:, j:j + 1] for j in range(TOP_K)]
    m = jnp.zeros((tm, LANES), F32)
    for hit in hits:
        m = m + jnp.where(hit, 1.0, 0.0)
    before = _ones_where(_iota((tm, tm), 1) < _iota((tm, tm), 0))
    base = base_ref[...]
    val = _dot(before, m.astype(BF16)) + base
    out = jnp.zeros((tm, LANES), jnp.int32)
    for j, hit in enumerate(hits):
        rj = jnp.sum(jnp.where(hit, val, 0.0), axis=-1, keepdims=True)
        out = jnp.where(lane == j, rj.astype(jnp.int32), out)
    rank_ref[...] = out
    total = base + jnp.sum(m, axis=0, keepdims=True)
    base_ref[...] = total
    cnt_ref[...] = total


def _route_ranks(idx2d):
    n_tok = idx2d.shape[0]
    tm = TOKEN_TILE
    return pl.pallas_call(
        _rank_kernel,
        grid=(n_tok // tm,),
        in_specs=[pl.BlockSpec((tm, LANES), lambda i: (i, 0))],
        out_specs=(pl.BlockSpec((tm, LANES), lambda i: (i, 0)), pl.BlockSpec((1, LANES), lambda i: (0, 0))),
        out_shape=(jax.ShapeDtypeStruct((n_tok, LANES), jnp.int32), jax.ShapeDtypeStruct((1, LANES), F32)),
        scratch_shapes=[pltpu.VMEM((1, LANES), F32)],
        compiler_params=_params(("arbitrary",)),
        name="route_ranks",
    )(idx2d)


def _expert_kernel(be_ref, xa_ref, xb_ref, wg_ref, wu_ref, wd_ref, oa_ref, ob_ref, wgb_ref, wub_ref, wdb_ref):
    q = xa_ref.shape[1]
    i = pl.program_id(0)
    n_live = be_ref[be_ref.shape[0] - 1]

    @pl.when((i == 0) | (be_ref[i] != be_ref[jnp.maximum(i - 1, 0)]))
    def _():
        wgb_ref[...] = wg_ref[0].astype(BF16)
        wub_ref[...] = wu_ref[0].astype(BF16)
        wdb_ref[...] = wd_ref[0].astype(BF16)

    @pl.when(i < n_live)
    def _():
        parts = [p.astype(BF16) for p in _unpack_rows(xa_ref[...], xb_ref[...])]

        def proj(w_ref):
            acc = _dot(parts[0], w_ref[0:q, :])
            for j in range(1, 4):
                acc = acc + _dot(parts[j], w_ref[j * q:(j + 1) * q, :])
            return acc

        hid = _silu(proj(wgb_ref)) * proj(wub_ref)
        oa_ref[...], ob_ref[...] = _pack_rows(_dot(hid.astype(BF16), wdb_ref[...]))

    @pl.when(i >= n_live)
    def _():
        oa_ref[...] = jnp.zeros_like(oa_ref)
        ob_ref[...] = jnp.zeros_like(ob_ref)


def _expert_blocks(xsa, xsb, block_expert, w_gate, w_up, w_down):
    n_rows, pw = xsa.shape
    bm = EXPERT_ROWS
    d, e_dim = w_gate.shape[1], w_gate.shape[2]
    assert block_expert.shape[0] == n_rows // bm + 1
    row = lambda: pl.BlockSpec((bm, pw), lambda i, be: (i, 0))
    grid_spec = pltpu.PrefetchScalarGridSpec(
        num_scalar_prefetch=1,
        grid=(n_rows // bm,),
        in_specs=[row(), row(),
                  pl.BlockSpec((1, d, e_dim), lambda i, be: (be[i], 0, 0)),
                  pl.BlockSpec((1, d, e_dim), lambda i, be: (be[i], 0, 0)),
                  pl.BlockSpec((1, e_dim, d), lambda i, be: (be[i], 0, 0))],
        out_specs=(row(), row()),
        scratch_shapes=[pltpu.VMEM((d, e_dim), BF16), pltpu.VMEM((d, e_dim), BF16), pltpu.VMEM((e_dim, d), BF16)],
    )
    return pl.pallas_call(
        _expert_kernel,
        grid_spec=grid_spec,
        out_shape=(jax.ShapeDtypeStruct((n_rows, pw), jnp.uint32), jax.ShapeDtypeStruct((n_rows, pw), jnp.uint32)),
        compiler_params=_params(("arbitrary",)),
        name="routed_experts",
    )(block_expert, xsa, xsb, w_gate, w_up, w_down)


def _combine_kernel(h_ref, ra_ref, rb_ref, gate_ref, gf_ref, *rest):
    nw_ref, o_ref = (rest[0], rest[1]) if len(rest) == 2 else (None, rest[0])
    tm, q = ra_ref.shape[1], ra_ref.shape[2]
    gate = gate_ref[0]
    acc = [jnp.zeros((tm, q), F32) for _ in range(4)]
    for j in range(TOP_K):
        gj = gate[:, j:j + 1]
        for blk, part in enumerate(_unpack_rows(ra_ref[j], rb_ref[j])):
            acc[blk] = acc[blk] + part * gj
    out = [h_ref[0, :, blk * q:(blk + 1) * q] + gf_ref[0, :, blk * q:(blk + 1) * q] * acc[blk] for blk in range(4)]
    if nw_ref is not None:
        ms = sum(jnp.sum(o * o, axis=-1, keepdims=True) for o in out) * (1.0 / (4 * q))
        scale = lax.rsqrt(ms + EPS)
        out = [o * scale * nw_ref[:, blk * q:(blk + 1) * q] for blk, o in enumerate(out)]
    for blk in range(4):
        o_ref[0, :, blk * q:(blk + 1) * q] = out[blk]


def _combine(h2, ra, rb, gate, mod, final_w, *, ctx):
    bsz, t_all, d = h2.shape
    tm = TOKEN_TILE
    tiles = t_all // tm
    ctx_tiles = ctx // tm
    rows = mod.shape[0] // 6
    pw = ra.shape[-1]
    skip = ctx_tiles if final_w is not None else 0
    per = lambda: pl.BlockSpec((TOP_K, tm, pw), lambda b, t: (0, b * tiles + t + skip, 0))
    in_specs = [pl.BlockSpec((1, tm, d), lambda b, t: (b, t + skip, 0)), per(), per(),
                pl.BlockSpec((1, tm, LANES), lambda b, t: (b, t + skip, 0)),
                pl.BlockSpec((1, 1, d), lambda b, t: _mod_row(5, ctx_tiles, rows)(b, t + skip))]
    args = [h2, ra, rb, gate, mod]
    if final_w is not None:
        in_specs.append(pl.BlockSpec((1, d), lambda b, t: (0, 0)))
        args.append(final_w)
    return pl.pallas_call(
        _combine_kernel,
        grid=(bsz, tiles - skip),
        in_specs=in_specs,
        out_specs=pl.BlockSpec((1, tm, d), lambda b, t: (b, t, 0)),
        out_shape=jax.ShapeDtypeStruct((bsz, t_all - skip * tm, d), F32),
        compiler_params=_params(("parallel", "parallel")),
        name="moe_combine_final" if final_w is not None else "moe_combine",
    )(*args)


def _gather_rows(xa, xb, indices):
    n = indices.shape[0]
    q = xa.shape[1]
    win = GATHER_WINDOW
    mesh = plsc.VectorSubcoreMesh(core_axis_name="core", subcore_axis_name="subcore")
    workers = mesh.num_cores * mesh.num_subcores
    per = n // (win * workers)
    assert per * win * workers == n, (n, win, workers)
    out = jax.ShapeDtypeStruct((n, q), xa.dtype)
    scratch = [pltpu.VMEM((per, win), jnp.int32), pltpu.VMEM((win, q), xa.dtype), pltpu.VMEM((win, q), xa.dtype),
               pltpu.SemaphoreType.DMA((4,))]

    @functools.partial(pl.kernel, out_type=(out, out), mesh=mesh, scratch_types=scratch)
    def gather(xa_hbm, xb_hbm, i_hbm, oa_hbm, ob_hbm, idx_vmem, buf_a, buf_b, sems):
        wid = lax.axis_index("core") * mesh.num_subcores + lax.axis_index("subcore")
        pltpu.sync_copy(i_hbm.at[wid], idx_vmem)

        @pl.loop(0, per)
        def _(s):
            rows = pl.ds(pl.multiple_of((wid * per + s) * win, win), win)
            ga = pltpu.async_copy(xa_hbm.at[idx_vmem.at[s]], buf_a, sems.at[0])
            gb = pltpu.async_copy(xb_hbm.at[idx_vmem.at[s]], buf_b, sems.at[1])
            ga.wait()
            wa = pltpu.async_copy(buf_a, oa_hbm.at[rows], sems.at[2])
            gb.wait()
            wb = pltpu.async_copy(buf_b, ob_hbm.at[rows], sems.at[3])
            wa.wait()
            wb.wait()

    return gather(xa, xb, indices.reshape(workers, per, win))


def _routed(xpa, xpb, idx, w_gate, w_up, w_down, layer):
    bsz, t_all, pw = xpa.shape
    n_tok = bsz * t_all
    n_assign = n_tok * TOP_K
    bm = EXPERT_ROWS
    n_blocks = -(-n_assign // bm) + N_EXPERTS
    n_slots = n_blocks * bm
    idx2d = idx.reshape(n_tok, LANES)
    rank, cnt = _route_ranks(idx2d)
    counts = cnt[0, :N_EXPERTS].astype(jnp.int32)
    padded = (counts + bm - 1) // bm * bm
    pad_end = jnp.cumsum(padded)
    offset = pad_end - padded
    first = jnp.cumsum(counts) - counts
    ek = idx2d[:, :TOP_K]
    slot = jnp.take(offset, ek) + rank[:, :TOP_K]
    block_expert = jnp.minimum(
        jnp.sum(pad_end[None, :] <= (jnp.arange(n_blocks, dtype=jnp.int32) * bm)[:, None], axis=1),
        N_EXPERTS - 1).astype(jnp.int32)
    token_sorted = jnp.argsort(ek.reshape(-1)) // TOP_K
    e_slot = jnp.repeat(block_expert, bm)
    r_slot = jnp.arange(n_slots, dtype=jnp.int32) - jnp.take(offset, e_slot)
    live = r_slot < jnp.take(counts, e_slot)
    filler = jnp.arange(n_slots, dtype=jnp.int32) % n_tok
    src = jnp.where(live, jnp.take(token_sorted, jnp.clip(jnp.take(first, e_slot) + r_slot, 0, n_assign - 1)), filler)
    xsa, xsb = _gather_rows(xpa.reshape(n_tok, pw), xpb.reshape(n_tok, pw), src.astype(jnp.int32))
    stack = lambda w: w.reshape((-1,) + w.shape[2:])
    table = jnp.concatenate([block_expert + layer * N_EXPERTS, (pad_end[-1:] // bm).astype(jnp.int32)])
    oa, ob = _expert_blocks(xsa, xsb, table, stack(w_gate), stack(w_up), stack(w_down))
    ra, rb = _gather_rows(oa, ob, slot.T.reshape(-1))
    return ra.reshape(TOP_K, n_tok, pw), rb.reshape(TOP_K, n_tok, pw)


def kernel(x, c, ctx, c_ctx, w_mod, b_mod, norm_mix, w_in, ssd_conv_w, ssd_conv_b, ssd_dt_bias, ssd_a_log, ssd_d,
           ssd_norm, hgrn_lb, hgrn_norm, gdn_conv_w, gdn_dt_bias, gdn_a_log, gdn_norm, w_out, norm_ffn, router_w,
           router_bias, exp_gate, exp_up, exp_down, sh_gate, sh_up, sh_down, norm_final):
    bsz, seq, d = x.shape
    n_ctx_tok = ctx.shape[1]
    depth = w_in.shape[0]
    assert d == D_MODEL and n_ctx_tok == TOKEN_TILE and seq % TOKEN_TILE == 0 and TOKEN_TILE % SCAN_CHUNK == 0
    assert TOKEN_TILE % GRID_W == 0 and seq % GRID_W == 0
    n_ctx = n_ctx_tok // SCAN_CHUNK

    p_lb = jax.nn.softmax(hgrn_lb.astype(F32), axis=0)
    lower_all = jnp.cumsum(p_lb, axis=0) - p_lb[0]

    rows = -(-(bsz + 1) // 8) * 8
    cond = jnp.zeros((rows, d), F32).at[:bsz].set(c).at[rows - 1].set(c_ctx)

    h = jnp.concatenate([ctx, x], axis=1)
    for l in range(depth):
        mod = _modulation(cond, w_mod[l], b_mod[l]).reshape(rows * 6, 1, d)
        w_p = _permute_w_in(w_in[l]).astype(BF16)
        u = _inproj(h, mod, norm_mix[l], w_p, ssd_conv_w[l], ssd_conv_b[l], gdn_conv_w[l], ctx=n_ctx_tok)
        ys = [_ssd_scan(u, ssd_dt_bias[l], ssd_a_log[l], ssd_d[l], rev=r, n_ctx=n_ctx) for r in (False, True)]
        yh = [_hgrn_scan(u, lower_all[l], rev=r, n_ctx=n_ctx) for r in (False, True)]
        yg = [_gdn_scan(u, gdn_dt_bias[l], gdn_a_log[l], rev=r, n_ctx=n_ctx) for r in (False, True)]
        nrm = jnp.concatenate([ssd_norm[l], jnp.tile(hgrn_norm[l], HG_WIDTH // HEAD),
                               jnp.tile(gdn_norm[l], GD_WIDTH // HEAD)]).astype(F32).reshape(1, d)
        rwp = jnp.pad(router_w[l].astype(F32), ((0, 0), (0, LANES - N_EXPERTS)))
        rbp = jnp.pad(router_bias[l].astype(F32), (0, LANES - N_EXPERTS)).reshape(1, LANES)
        h2, xpa, xpb, idx, gate = _outproj(ys, yh, yg, u, h, mod, nrm, w_out[l].astype(BF16),
                                           norm_ffn[l].reshape(1, d), rwp, rbp, sh_gate[l].astype(BF16),
                                           sh_up[l].astype(BF16), sh_down[l].astype(BF16), ctx=n_ctx_tok)
        ra, rb = _routed(xpa, xpb, idx, exp_gate, exp_up, exp_down, l)
        last = l == depth - 1
        h = _combine(h2, ra, rb, gate, mod, norm_final.reshape(1, d) if last else None, ctx=n_ctx_tok)
    return h
```

```python
import functools

import numpy as np
import jax
import jax.numpy as jnp
from jax import lax
from jax.experimental import pallas as pl
from jax.experimental.pallas import tpu as pltpu
from jax.experimental.pallas import tpu_sc as plsc

F32 = jnp.float32
BF16 = jnp.bfloat16

D_MODEL = 1024
GRID_W = 64
CONV_W = 5
EPS = 1e-6
MIN_LOWER = 1e-30
MASKED_EXPONENT = -1e30
HEAD = 64
SSD_HEADS = 8
SSD_WIDTH = 512
SSD_STATE = 128
SSD_GROUPS = 2
HG_WIDTH = 256
GD_WIDTH = 256
GD_HEADS = 4
N_EXPERTS = 64
TOP_K = 8
EXPERT_DIM = 256
ROUTED_SCALE = 2.5

LANES = 128
SCAN_CHUNK = 128
SAMPLE_GROUPS = 2
SCAN_BATCH = 4
BASE = 16
TOKEN_TILE = 256
EXPERT_ROWS = 1024
GATHER_WINDOW = 128
VMEM_LIMIT = 56 * 1024 * 1024

Z_OFF, X_OFF, BC_OFF = 0, 512, 1024
HQ_OFF, HFF_OFF, HFB_OFF, HI_OFF, HGATE_OFF = 1536, 1792, 2048, 2304, 2560
GQ_OFF, GK_OFF, GV_OFF, GGATE_OFF = 2816, 3072, 3328, 3584
SM_OFF = 3840
NCOLS = 3968
SSD_CONV = 1024
GDN_CONV = 768
SM_DT = (0, 8)
SM_A = (16, 20)
SM_B = (24, 28)


def _permute_w_in(w):
    d = w.shape[0]
    parts = [w[:, 0:1536],
             w[:, 1552:2832],
             w[:, 2832:3600],
             w[:, 3616:3872],
             w[:, 1536:1552],
             w[:, 3600:3616],
             jnp.zeros((d, LANES - 32), w.dtype)]
    return jnp.concatenate(parts, axis=1)


def _dot(a, b):
    return lax.dot_general(a, b, (((1,), (0,)), ((), ())), preferred_element_type=F32)


def _dot_nt(a, b):
    return lax.dot_general(a, b, (((1,), (1,)), ((), ())), preferred_element_type=F32)


def _split(a):
    hi = a.astype(BF16)
    lo = (a - hi.astype(F32)).astype(BF16)
    return hi, lo


def _dot3(a, b):
    ah, al = _split(a)
    bh, bl = _split(b)
    return _dot(ah, bh) + (_dot(ah, bl) + _dot(al, bh))


def _dot3_exact_rhs(a, b_bf16):
    ah, al = _split(a)
    return _dot(ah, b_bf16) + _dot(al, b_bf16)


def _silu(x):
    return x * jax.nn.sigmoid(x)


def _softplus(x):
    return jnp.maximum(x, 0.0) + jnp.log1p(jnp.exp(-jnp.abs(x)))


def _log_sigmoid(x):
    return jnp.minimum(x, 0.0) - jnp.log1p(jnp.exp(-jnp.abs(x)))


def _params(sem):
    return pltpu.CompilerParams(dimension_semantics=sem, vmem_limit_bytes=VMEM_LIMIT)


def _iota(shape, dim):
    return lax.broadcasted_iota(jnp.int32, shape, dim)


def _ones_where(mask):
    return jnp.where(mask, 1.0, 0.0).astype(BF16)


def _shr(x, div):
    return jnp.right_shift(x, int(np.log2(div)))


def _scan_mask(n, rev):
    r, c = _iota((n, n), 0), _iota((n, n), 1)
    return (c >= r) if rev else (c <= r)


def _expand_matrix(lane0, width):
    r, c = _iota((LANES, width), 0), _iota((LANES, width), 1)
    return _ones_where(r == lane0 + _shr(c, HEAD))


def _head_blocks(n):
    r, c = _iota((n, n), 0), _iota((n, n), 1)
    return _shr(r, HEAD) == _shr(c, HEAD)


def _cumsum_rows(x, rev):
    tri = _ones_where(_scan_mask(x.shape[0], rev))
    xh, xl = _split(x)
    return _dot(tri, xh) + _dot(tri, xl)


def _mod_kernel(s_ref, w_ref, b_ref, o_ref):
    s = _silu(s_ref[...])
    o_ref[...] = _dot(s.astype(BF16), w_ref[...].astype(BF16)) + b_ref[...]


def _modulation(cond, w, b):
    rows, d = cond.shape
    n = w.shape[1]
    bn = d
    return pl.pallas_call(
        _mod_kernel,
        grid=(n // bn,),
        in_specs=[pl.BlockSpec((rows, d), lambda j: (0, 0)),
                  pl.BlockSpec((d, bn), lambda j: (0, j)),
                  pl.BlockSpec((1, bn), lambda j: (0, j))],
        out_specs=pl.BlockSpec((rows, bn), lambda j: (0, j)),
        out_shape=jax.ShapeDtypeStruct((rows, n), F32),
        compiler_params=_params(("parallel",)),
        name="modulation",
    )(cond, w, b.reshape(1, n))


def _inproj_kernel(h_ref, sc_ref, sh_ref, nw_ref, w_ref, cws_ref, cbs_ref, cwg_ref, u_ref, *, ctx_tiles):
    tm = h_ref.shape[1]
    t = pl.program_id(1)
    h = h_ref[0]
    a = h * lax.rsqrt(jnp.mean(h * h, axis=-1, keepdims=True) + EPS) * nw_ref[...]
    a = a * (1.0 + sc_ref[0]) + sh_ref[0]
    ab = a.astype(BF16)
    seg = jnp.where(t < ctx_tiles, tm, GRID_W)
    pos = _iota((tm, 1), 0) & (seg - 1)
    half = CONV_W // 2
    masks = {o: (pos + o >= 0) & (pos + o < seg) for o in range(-half, half + 1) if o}

    def conv(x, cw_ref, c0, wd, cb_ref):
        acc = x * cw_ref[half:half + 1, c0:c0 + wd]
        for j in range(CONV_W):
            o = j - half
            if o == 0:
                continue
            shifted = pltpu.roll(x, (-o) % tm, 0)
            acc = acc + jnp.where(masks[o], shifted, 0.0) * cw_ref[j:j + 1, c0:c0 + wd]
        if cb_ref is not None:
            acc = acc + cb_ref[:, c0:c0 + wd]
        return _silu(acc)

    step = 2 * LANES
    for c0 in range(0, NCOLS, step):
        wd = min(step, NCOLS - c0)
        u = _dot(ab, w_ref[:, c0:c0 + wd])
        if X_OFF <= c0 < X_OFF + SSD_CONV:
            u = conv(u, cws_ref, c0 - X_OFF, wd, cbs_ref)
        elif GQ_OFF <= c0 < GQ_OFF + GDN_CONV:
            u = conv(u, cwg_ref, c0 - GQ_OFF, wd, None)
        u_ref[0, :, c0:c0 + wd] = u


def _mod_row(k, ctx_tiles, rows):
    return lambda b, t: (jnp.where(t < ctx_tiles, rows - 1, b) * 6 + k, 0, 0)


def _inproj(h, mod, norm_w, w_p, conv_s, bias_s, conv_g, *, ctx):
    bsz, t_all, d = h.shape
    tm = TOKEN_TILE
    ctx_tiles = ctx // tm
    rows = mod.shape[0] // 6
    return pl.pallas_call(
        functools.partial(_inproj_kernel, ctx_tiles=ctx_tiles),
        grid=(bsz, t_all // tm),
        in_specs=[pl.BlockSpec((1, tm, d), lambda b, t: (b, t, 0)),
                  pl.BlockSpec((1, 1, d), _mod_row(1, ctx_tiles, rows)),
                  pl.BlockSpec((1, 1, d), _mod_row(0, ctx_tiles, rows)),
                  pl.BlockSpec((1, d), lambda b, t: (0, 0)),
                  pl.BlockSpec((d, NCOLS), lambda b, t: (0, 0)),
                  pl.BlockSpec((CONV_W, SSD_CONV), lambda b, t: (0, 0)),
                  pl.BlockSpec((1, SSD_CONV), lambda b, t: (0, 0)),
                  pl.BlockSpec((CONV_W, GDN_CONV), lambda b, t: (0, 0))],
        out_specs=pl.BlockSpec((1, tm, NCOLS), lambda b, t: (b, t, 0)),
        out_shape=jax.ShapeDtypeStruct((bsz, t_all, NCOLS), F32),
        compiler_params=_params(("parallel", "parallel")),
        name="in_projection",
    )(h, mod, mod, norm_w.reshape(1, d), w_p, conv_s, bias_s.reshape(1, -1), conv_g)


def _chunk_index(rev, n_ctx, n_all):
    if not rev:
        return lambda c: c
    return lambda c: jnp.where(c < n_ctx, n_ctx - 1 - c, n_all + n_ctx - 1 - c)


def _scan_call(body, u, col_blocks, extra, out_width, scratch, *, rev, n_ctx, name):
    bsz, t_all, _ = u.shape
    cn, bb = SCAN_CHUNK, SCAN_BATCH
    assert bsz % bb == 0
    n_all = t_all // cn
    cidx = _chunk_index(rev, n_ctx, n_all)

    def tok(width, off):
        return pl.BlockSpec((bb, cn, width), lambda b, c: (b, cidx(c), off // width))

    in_specs = [tok(w, off) for w, off in col_blocks]
    in_specs += [pl.BlockSpec(a.shape, lambda b, c: (0, 0)) for a in extra]
    return pl.pallas_call(
        body,
        grid=(bsz // bb, n_all),
        in_specs=in_specs,
        out_specs=tok(out_width, 0),
        out_shape=jax.ShapeDtypeStruct((bsz, t_all, out_width), F32),
        scratch_shapes=[scratch],
        compiler_params=_params(("parallel", "arbitrary")),
        name=name,
    )(*([u] * len(col_blocks)), *extra)


def _reset_state(st_ref):
    @pl.when(pl.program_id(1) == 0)
    def _():
        st_ref[...] = jnp.zeros_like(st_ref)


def _ssd_kernel(x_ref, bc_ref, sm_ref, bias_ref, arow_ref, drow_ref, y_ref, st_ref, *, rev):
    bb, cn = x_ref.shape[0], x_ref.shape[1]
    _reset_state(st_ref)
    lane0 = SM_DT[1] if rev else SM_DT[0]
    mask = _scan_mask(cn, rev)
    ex = _expand_matrix(lane0, SSD_WIDTH)
    gw = SSD_WIDTH // SSD_GROUPS
    hpg = SSD_HEADS // SSD_GROUPS
    lane_head = _shr(_iota((1, gw), 1), HEAD)
    last = 0 if rev else cn - 1
    for i in range(bb):
        xs = x_ref[i]
        bc = bc_ref[i]
        dt = _softplus(sm_ref[i] + bias_ref[...])
        da = dt * arow_ref[...]
        acs = _cumsum_rows(da, rev)
        acs_r = acs.T
        acs_e = _dot3_exact_rhs(acs, ex)
        dt_e = _dot3_exact_rhs(dt, ex)
        tot_e = acs_e[last:last + 1]
        xdt = xs * dt_e
        xdt_b = xdt.astype(BF16)
        wst = (xdt * jnp.exp(tot_e - acs_e)).astype(BF16)
        eacs = jnp.exp(acs_e)
        etot = jnp.exp(tot_e)
        for g in range(SSD_GROUPS):
            bm = bc[:, g * SSD_STATE:(g + 1) * SSD_STATE]
            cm = bc[:, (SSD_GROUPS + g) * SSD_STATE:(SSD_GROUPS + g + 1) * SSD_STATE]
            bmb, cmb = bm.astype(BF16), cm.astype(BF16)
            cb = _dot_nt(cmb, bmb)
            xg = xdt_b[:, g * gw:(g + 1) * gw]
            yd = jnp.zeros((cn, gw), F32)
            for r in range(hpg):
                ln = lane0 + g * hpg + r
                diff = acs[:, ln:ln + 1] - acs_r[ln:ln + 1, :]
                dec = jnp.where(mask, jnp.exp(jnp.where(mask, diff, 0.0)), 0.0)
                yh = _dot((cb * dec).astype(BF16), xg)
                yd = jnp.where(lane_head == r, yh, yd)
            st = st_ref[i * SSD_GROUPS + g]
            yo = _dot(cmb, st.astype(BF16)) * eacs[:, g * gw:(g + 1) * gw]
            y = yd + yo
            if drow_ref is not None:
                y = y + drow_ref[:, g * gw:(g + 1) * gw] * xs[:, g * gw:(g + 1) * gw]
            y_ref[i, :, g * gw:(g + 1) * gw] = y
            st_ref[i * SSD_GROUPS + g] = (st * etot[:, g * gw:(g + 1) * gw]
                                          + _dot(bm.T.astype(BF16), wst[:, g * gw:(g + 1) * gw]))


def _ssd_kernel_no_skip(x_ref, bc_ref, sm_ref, bias_ref, arow_ref, y_ref, st_ref, *, rev):
    _ssd_kernel(x_ref, bc_ref, sm_ref, bias_ref, arow_ref, None, y_ref, st_ref, rev=rev)


def _ssd_scan(u, dt_bias, a_log, d_skip, *, rev, n_ctx):
    di = 1 if rev else 0
    lane0 = SM_DT[di]
    bias = jnp.zeros((1, LANES), F32).at[0, lane0:lane0 + SSD_HEADS].set(dt_bias[di])
    arow = jnp.zeros((1, LANES), F32).at[0, lane0:lane0 + SSD_HEADS].set(-jnp.exp(a_log[di].astype(F32)))
    extra = [bias, arow]
    if rev:
        body = functools.partial(_ssd_kernel_no_skip, rev=rev)
    else:
        extra.append(jnp.repeat(d_skip.astype(F32), HEAD).reshape(1, SSD_WIDTH))
        body = functools.partial(_ssd_kernel, rev=rev)
    scratch = pltpu.VMEM((SCAN_BATCH * SSD_GROUPS, SSD_STATE, SSD_WIDTH // SSD_GROUPS), F32)
    return _scan_call(body, u, [(SSD_WIDTH, X_OFF), (512, BC_OFF), (LANES, SM_OFF)], extra, SSD_WIDTH, scratch,
                      rev=rev, n_ctx=n_ctx, name="ssd_scan_bwd" if rev else "ssd_scan_fwd")


def _hgrn_kernel(q_ref, f_ref, i_ref, low_ref, y_ref, st_ref, *, rev):
    bb, cn = q_ref.shape[0], q_ref.shape[1]
    wdt = HG_WIDTH
    heads = wdt // HEAD
    _reset_state(st_ref)
    lower = low_ref[0:1]
    log_lower = low_ref[1:2]
    hb = _head_blocks(wdt)
    bd = _ones_where(hb)
    lane_head = _shr(_iota((1, wdt), 1), HEAD)
    last = 0 if rev else cn - 1

    for i in range(bb):
        fr = f_ref[i]
        qs = _silu(q_ref[i])
        v = i_ref[i]
        c = log_lower - fr
        tail = lambda z: jnp.log(1.0 + jnp.exp(-jnp.abs(z)))
        logf = (jnp.minimum(fr, 0.0) - tail(fr)) + (jnp.maximum(c, 0.0) + tail(c))
        kg = (1.0 - lower) * jax.nn.sigmoid(-fr)
        b = _cumsum_rows(logf, rev)
        vb = v.astype(BF16)

        st = st_ref[i]
        y_ref[i] = _dot_nt((qs * jnp.exp(b)).astype(BF16), st.astype(BF16))
        b_last = b[last:last + 1]
        kend = (kg * jnp.exp(b_last - b)).astype(BF16)
        st_ref[i] = st * jnp.exp(b_last) + jnp.where(hb, _dot(v.T.astype(BF16), kend), 0.0)

        def offdiag(t0, t1, s0, s1, r):
            br = b[r:r + 1]
            qp = qs[t0:t1] * jnp.exp(b[t0:t1] - br)
            kp = (kg[s0:s1] * jnp.exp(br - b[s0:s1])).astype(BF16)
            nt = t1 - t0
            qstack = jnp.concatenate([jnp.where(lane_head == h, qp, 0.0) for h in range(heads)], axis=0)
            att = _dot_nt(qstack.astype(BF16), kp)
            res = _dot(att.astype(BF16), vb[s0:s1])
            out = jnp.zeros((nt, wdt), F32)
            for h in range(heads):
                out = jnp.where(lane_head == h, res[h * nt:(h + 1) * nt], out)
            y_ref[i, t0:t1, :] += out

        def diag(t0, t1):
            n = t1 - t0
            bt = b[t0:t1]
            si, ti = _iota((n, n, wdt), 0), _iota((n, n, wdt), 1)
            m3 = (si >= ti) if rev else (si <= ti)
            diff = bt[None, :, :] - bt[:, None, :]
            w = jnp.exp(jnp.where(m3, diff, MASKED_EXPONENT))
            p = w * qs[t0:t1][None, :, :] * kg[t0:t1][:, None, :]
            r2 = _dot(p.reshape(n * n, wdt).astype(BF16), bd)
            y_ref[i, t0:t1, :] += jnp.sum(r2.reshape(n, n, wdt) * v[t0:t1][:, None, :], axis=0)

        def block(lo, hi):
            if hi - lo <= BASE:
                diag(lo, hi)
                return
            mid = (lo + hi) // 2
            if rev:
                offdiag(lo, mid, mid, hi, mid)
            else:
                offdiag(mid, hi, lo, mid, mid - 1)
            block(lo, mid)
            block(mid, hi)

        block(0, cn)


def _hgrn_scan(u, lower, *, rev, n_ctx):
    f_off = HFB_OFF if rev else HFF_OFF
    low = jnp.stack([lower, jnp.log(jnp.maximum(lower, MIN_LOWER))]).astype(F32)
    low = jnp.concatenate([low, jnp.zeros((6, HG_WIDTH), F32)], axis=0)
    scratch = pltpu.VMEM((SCAN_BATCH, HG_WIDTH, HG_WIDTH), F32)
    return _scan_call(functools.partial(_hgrn_kernel, rev=rev), u,
                      [(HG_WIDTH, HQ_OFF), (HG_WIDTH, f_off), (HG_WIDTH, HI_OFF)], [low], HG_WIDTH, scratch,
                      rev=rev, n_ctx=n_ctx, name="hgrn_scan_bwd" if rev else "hgrn_scan_fwd")


def _bdot(a, b):
    return lax.dot_general(a, b, (((2,), (1,)), ((0,), (0,))), preferred_element_type=F32)


def _unit_inverse_delta(a):
    n_rows = a.shape[-1]
    r, c = _iota((n_rows, n_rows), 0), _iota((n_rows, n_rows), 1)
    d = jnp.where((_shr(r, BASE) == _shr(c, BASE))[None], a, 0.0)
    db = d.astype(BF16)
    p = _bdot(db, db)
    n = p - d - _bdot(db, p.astype(BF16))
    e = 4
    while e < BASE:
        pb = p.astype(BF16)
        p = _bdot(pb, pb)
        n = n + p + _bdot(n.astype(BF16), p.astype(BF16))
        e *= 2
    size = BASE
    while size < n_rows:
        big = 2 * size
        off = (_shr(r, big) == _shr(c, big)) & (_shr(r, size) != _shr(c, size))
        a_off = jnp.where(off[None], a, 0.0)
        nb = n.astype(BF16)
        m = a_off + _bdot(nb, a_off.astype(BF16))
        n = n - (m + _bdot(m.astype(BF16), nb))
        size = big
    return n


def _gdn_kernel(q_ref, k_ref, v_ref, sm_ref, bias_ref, arow_ref, y_ref, st_ref, *, rev):
    bb, cn = q_ref.shape[0], q_ref.shape[1]
    wdt = GD_WIDTH
    _reset_state(st_ref)
    hb = _head_blocks(wdt)
    bd = _ones_where(hb)
    lane_head = _shr(_iota((1, wdt), 1), HEAD)
    la = SM_A[1] if rev else SM_A[0]
    lb = SM_B[1] if rev else SM_B[0]
    ex_a, ex_b = _expand_matrix(la, wdt), _expand_matrix(lb, wdt)
    mask = _scan_mask(cn, rev)
    strict = mask & (_iota((cn, cn), 0) != _iota((cn, cn), 1))
    last = 0 if rev else cn - 1
    pre, a_all = [], []
    for i in range(bb):
        q, k, v = q_ref[i], k_ref[i], v_ref[i]
        q = q * lax.rsqrt(_dot3_exact_rhs(q * q, bd) + 1e-6) * (HEAD ** -0.5)
        k = k * lax.rsqrt(_dot3_exact_rhs(k * k, bd) + 1e-6)
        sm = sm_ref[i]
        g = arow_ref[...] * _softplus(sm + bias_ref[...])
        beta = jax.nn.sigmoid(sm)
        gam = _cumsum_rows(g, rev)
        gam_r = gam.T
        gam_e = _dot3_exact_rhs(gam, ex_a)
        beta_e = _dot3_exact_rhs(beta, ex_b)
        tot_e = gam_e[last:last + 1]
        egam = jnp.exp(gam_e)
        kb = k * beta_e
        kbf = k.astype(BF16)
        rhs = jnp.concatenate([v * beta_e, kb * egam], axis=1)
        qks = []
        for h in range(GD_HEADS):
            ln = la + h
            diff = gam[:, ln:ln + 1] - gam_r[ln:ln + 1, :]
            dec = jnp.where(mask, jnp.exp(jnp.where(mask, diff, 0.0)), 0.0)
            hm = lane_head == h
            kk = _dot_nt(jnp.where(hm, kb, 0.0).astype(BF16), kbf)
            a_all.append(jnp.where(strict, kk * dec, 0.0))
            qks.append((_dot_nt(jnp.where(hm, q, 0.0).astype(BF16), kbf) * dec).astype(BF16))
        pre.append((q, k, rhs, qks, egam, gam_e, tot_e))
    n_all = _unit_inverse_delta(jnp.stack(a_all)).astype(BF16)
    for i in range(bb):
        q, k, rhs, qks, egam, gam_e, tot_e = pre[i]
        rhs_b = rhs.astype(BF16)
        u_all = jnp.zeros((cn, wdt), F32)
        w_all = jnp.zeros((cn, wdt), F32)
        for h in range(GD_HEADS):
            sol = rhs + _dot(n_all[i * GD_HEADS + h], rhs_b)
            hm = lane_head == h
            u_all = jnp.where(hm, sol[:, :wdt], u_all)
            w_all = jnp.where(hm, sol[:, wdt:], w_all)
        st = st_ref[i]
        stb = st.astype(BF16)
        v_new = u_all - _dot(w_all.astype(BF16), stb)
        vnb = v_new.astype(BF16)
        o = _dot((q * egam).astype(BF16), stb)
        for h in range(GD_HEADS):
            o = o + jnp.where(lane_head == h, _dot(qks[h], vnb), 0.0)
        y_ref[i] = o
        kend = k * jnp.exp(tot_e - gam_e)
        st_ref[i] = st * jnp.exp(tot_e) + jnp.where(hb, _dot(kend.T.astype(BF16), vnb), 0.0)


def _gdn_scan(u, dt_bias, a_log, *, rev, n_ctx):
    di = 1 if rev else 0
    la = SM_A[di]
    bias = jnp.zeros((1, LANES), F32).at[0, la:la + GD_HEADS].set(dt_bias[di])
    arow = jnp.zeros((1, LANES), F32).at[0, la:la + GD_HEADS].set(-jnp.exp(a_log[di].astype(F32)))
    scratch = pltpu.VMEM((SCAN_BATCH, GD_WIDTH, GD_WIDTH), F32)
    return _scan_call(functools.partial(_gdn_kernel, rev=rev), u,
                      [(GD_WIDTH, GQ_OFF), (GD_WIDTH, GK_OFF), (GD_WIDTH, GV_OFF), (LANES, SM_OFF)], [bias, arow],
                      GD_WIDTH, scratch, rev=rev, n_ctx=n_ctx, name="gdn_scan_bwd" if rev else "gdn_scan_fwd")


def _pack_rows(x):
    half = x.shape[1] // 2
    q = half // 2
    xr = x.astype(BF16).astype(F32)
    lo = lax.bitcast_convert_type(xr[:, :half], jnp.uint32)
    hi = lax.bitcast_convert_type(xr[:, half:], jnp.uint32)
    word = (lo >> 16) | (hi & jnp.uint32(0xFFFF0000))
    return word[:, :q], word[:, q:]


def _unpack_rows(wa, wb):
    mask = jnp.uint32(0xFFFF0000)
    f = lambda w: lax.bitcast_convert_type(w, F32)
    return f(wa << 16), f(wb << 16), f(wa & mask), f(wb & mask)


def _group_rms(y, group, bd):
    ms = _dot3_exact_rhs(y * y, bd) * (1.0 / group)
    return y * lax.rsqrt(ms + EPS)


def _outproj_kernel(ysf, ysb, yhf, yhb, ygf, ygb, z_ref, hg_ref, gg_ref, h_ref, gm_ref, scf_ref, shf_ref, gf_ref,
                    nrm_ref, wout_ref, nf_ref, rw_ref, rb_ref, sg_ref, su_ref, sd_ref,
                    h2_ref, xpa_ref, xpb_ref, idx_ref, gate_ref):
    tm = h_ref.shape[1]
    gw = SSD_WIDTH // SSD_GROUPS
    r, c = _iota((SSD_WIDTH, SSD_WIDTH), 0), _iota((SSD_WIDTH, SSD_WIDTH), 1)
    bd_s = _ones_where(_shr(r, gw) == _shr(c, gw))
    bd_h = _ones_where(_head_blocks(HG_WIDTH))
    ys = (ysf[0] + ysb[0]) * _silu(z_ref[0])
    ys = _group_rms(ys, gw, bd_s) * nrm_ref[:, 0:SSD_WIDTH]
    yh = _group_rms(yhf[0] + yhb[0], HEAD, bd_h) * nrm_ref[:, SSD_WIDTH:SSD_WIDTH + HG_WIDTH] * _silu(hg_ref[0])
    yg = _group_rms(ygf[0] + ygb[0], HEAD, bd_h) * nrm_ref[:, SSD_WIDTH + HG_WIDTH:] * _silu(gg_ref[0])
    y = jnp.concatenate([ys, yh, yg], axis=1).astype(BF16)
    h1 = h_ref[0] + gm_ref[0] * _dot(y, wout_ref[...])
    xf = h1 * lax.rsqrt(jnp.mean(h1 * h1, axis=-1, keepdims=True) + EPS) * nf_ref[...]
    xf = xf * (1.0 + scf_ref[0]) + shf_ref[0]
    xb = xf.astype(BF16)
    xpa_ref[0], xpb_ref[0] = _pack_rows(xf)
    hid = _silu(_dot(xb, sg_ref[...])) * _dot(xb, su_ref[...])
    h2_ref[0] = h1 + gf_ref[0] * _dot(hid.astype(BF16), sd_ref[...])
    scores = jax.nn.sigmoid(_dot3(xf, rw_ref[...]))
    lane = _iota((tm, LANES), 1)
    sel = jnp.where(lane < N_EXPERTS, scores + rb_ref[...], -jnp.inf)
    rank_code = (LANES - lane).astype(F32)
    idx_f = jnp.zeros((tm, LANES), F32)
    gate_out = jnp.zeros((tm, LANES), F32)
    gsum = jnp.zeros((tm, 1), F32)
    for j in range(TOP_K):
        m = jnp.max(sel, axis=-1, keepdims=True)
        code = jnp.max(jnp.where(sel == m, rank_code, 0.0), axis=-1, keepdims=True)
        hit = rank_code == code
        gv = jnp.sum(jnp.where(hit, scores, 0.0), axis=-1, keepdims=True)
        idx_f = jnp.where(lane == j, LANES - code, idx_f)
        gate_out = jnp.where(lane == j, gv, gate_out)
        gsum = gsum + gv
        sel = jnp.where(hit, -jnp.inf, sel)
    idx_ref[0] = idx_f.astype(jnp.int32)
    gate_ref[0] = gate_out / gsum * ROUTED_SCALE


def _outproj(ys, yh, yg, u, h, mod, nrm, w_out, norm_ffn, router_wp, router_bp, sg, su, sd, *, ctx):
    bsz, t_all, d = h.shape
    tm = TOKEN_TILE
    ctx_tiles = ctx // tm
    rows = mod.shape[0] // 6
    mrow = lambda k: pl.BlockSpec((1, 1, d), _mod_row(k, ctx_tiles, rows))
    tok = lambda w, j=0: pl.BlockSpec((1, tm, w), lambda b, t: (b, t, j))
    full = lambda a: pl.BlockSpec(a.shape, lambda b, t: (0,) * a.ndim)
    in_specs = [tok(SSD_WIDTH), tok(SSD_WIDTH), tok(HG_WIDTH), tok(HG_WIDTH), tok(GD_WIDTH), tok(GD_WIDTH),
                tok(SSD_WIDTH, Z_OFF // SSD_WIDTH), tok(HG_WIDTH, HGATE_OFF // HG_WIDTH),
                tok(GD_WIDTH, GGATE_OFF // GD_WIDTH),
                tok(d), mrow(2), mrow(4), mrow(3), mrow(5),
                full(nrm), full(w_out), full(norm_ffn), full(router_wp), full(router_bp), full(sg), full(su), full(sd)]
    pw = d // 4
    out_shape = (jax.ShapeDtypeStruct((bsz, t_all, d), F32),
                 jax.ShapeDtypeStruct((bsz, t_all, pw), jnp.uint32), jax.ShapeDtypeStruct((bsz, t_all, pw), jnp.uint32),
                 jax.ShapeDtypeStruct((bsz, t_all, LANES), jnp.int32), jax.ShapeDtypeStruct((bsz, t_all, LANES), F32))
    out_specs = (tok(d), tok(pw), tok(pw), tok(LANES), tok(LANES))
    return pl.pallas_call(
        _outproj_kernel,
        grid=(bsz, t_all // tm),
        in_specs=in_specs,
        out_specs=out_specs,
        out_shape=out_shape,
        compiler_params=_params(("parallel", "parallel")),
        name="out_projection_router",
    )(ys[0], ys[1], yh[0], yh[1], yg[0], yg[1], u, u, u, h, mod, mod, mod, mod,
      nrm, w_out, norm_ffn, router_wp, router_bp, sg, su, sd)


def _rank_kernel(idx_ref, rank_ref, cnt_ref, base_ref):
    tm = idx_ref.shape[0]

    @pl.when(pl.program_id(0) == 0)
    def _():
        base_ref[...] = jnp.zeros_like(base_ref)

    idx = idx_ref[...]
    lane = _iota((tm, LANES), 1)
    hits = [lane == idx[:, j:j + 1] for j in range(TOP_K)]
    m = jnp.zeros((tm, LANES), F32)
    for hit in hits:
        m = m + jnp.where(hit, 1.0, 0.0)
    before = _ones_where(_iota((tm, tm), 1) < _iota((tm, tm), 0))
    base = base_ref[...]
    val = _dot(before, m.astype(BF16)) + base
    out = jnp.zeros((tm, LANES), jnp.int32)
    for j, hit in enumerate(hits):
        rj = jnp.sum(jnp.where(hit, val, 0.0), axis=-1, keepdims=True)
        out = jnp.where(lane == j, rj.astype(jnp.int32), out)
    rank_ref[...] = out
    total = base + jnp.sum(m, axis=0, keepdims=True)
    base_ref[...] = total
    cnt_ref[...] = total


def _route_ranks(idx2d):
    n_tok = idx2d.shape[0]
    tm = TOKEN_TILE
    return pl.pallas_call(
        _rank_kernel,
        grid=(n_tok // tm,),
        in_specs=[pl.BlockSpec((tm, LANES), lambda i: (i, 0))],
        out_specs=(pl.BlockSpec((tm, LANES), lambda i: (i, 0)), pl.BlockSpec((1, LANES), lambda i: (0, 0))),
        out_shape=(jax.ShapeDtypeStruct((n_tok, LANES), jnp.int32), jax.ShapeDtypeStruct((1, LANES), F32)),
        scratch_shapes=[pltpu.VMEM((1, LANES), F32)],
        compiler_params=_params(("arbitrary",)),
        name="route_ranks",
    )(idx2d)


def _expert_kernel(be_ref, xa_ref, xb_ref, wg_ref, wu_ref, wd_ref, oa_ref, ob_ref, wgb_ref, wub_ref, wdb_ref):
    q = xa_ref.shape[1]
    i = pl.program_id(0)
    n_live = be_ref[be_ref.shape[0] - 1]

    @pl.when((i == 0) | (be_ref[i] != be_ref[jnp.maximum(i - 1, 0)]))
    def _():
        wgb_ref[...] = wg_ref[0].astype(BF16)
        wub_ref[...] = wu_ref[0].astype(BF16)
        wdb_ref[...] = wd_ref[0].astype(BF16)

    @pl.when(i < n_live)
    def _():
        parts = [p.astype(BF16) for p in _unpack_rows(xa_ref[...], xb_ref[...])]

        def proj(w_ref):
            acc = _dot(parts[0], w_ref[0:q, :])
            for j in range(1, 4):
                acc = acc + _dot(parts[j], w_ref[j * q:(j + 1) * q, :])
            return acc

        hid = _silu(proj(wgb_ref)) * proj(wub_ref)
        oa_ref[...], ob_ref[...] = _pack_rows(_dot(hid.astype(BF16), wdb_ref[...]))

    @pl.when(i >= n_live)
    def _():
        oa_ref[...] = jnp.zeros_like(oa_ref)
        ob_ref[...] = jnp.zeros_like(ob_ref)


def _expert_blocks(xsa, xsb, block_expert, w_gate, w_up, w_down):
    n_rows, pw = xsa.shape
    bm = EXPERT_ROWS
    d, e_dim = w_gate.shape[1], w_gate.shape[2]
    assert block_expert.shape[0] == n_rows // bm + 1
    row = lambda: pl.BlockSpec((bm, pw), lambda i, be: (i, 0))
    grid_spec = pltpu.PrefetchScalarGridSpec(
        num_scalar_prefetch=1,
        grid=(n_rows // bm,),
        in_specs=[row(), row(),
                  pl.BlockSpec((1, d, e_dim), lambda i, be: (be[i], 0, 0)),
                  pl.BlockSpec((1, d, e_dim), lambda i, be: (be[i], 0, 0)),
                  pl.BlockSpec((1, e_dim, d), lambda i, be: (be[i], 0, 0))],
        out_specs=(row(), row()),
        scratch_shapes=[pltpu.VMEM((d, e_dim), BF16), pltpu.VMEM((d, e_dim), BF16), pltpu.VMEM((e_dim, d), BF16)],
    )
    return pl.pallas_call(
        _expert_kernel,
        grid_spec=grid_spec,
        out_shape=(jax.ShapeDtypeStruct((n_rows, pw), jnp.uint32), jax.ShapeDtypeStruct((n_rows, pw), jnp.uint32)),
        compiler_params=_params(("arbitrary",)),
        name="routed_experts",
    )(block_expert, xsa, xsb, w_gate, w_up, w_down)


def _combine_kernel(h_ref, ra_ref, rb_ref, gate_ref, gf_ref, *rest):
    nw_ref, o_ref = (rest[0], rest[1]) if len(rest) == 2 else (None, rest[0])
    tm, q = ra_ref.shape[1], ra_ref.shape[2]
    gate = gate_ref[0]
    acc = [jnp.zeros((tm, q), F32) for _ in range(4)]
    for j in range(TOP_K):
        gj = gate[:, j:j + 1]
        for blk, part in enumerate(_unpack_rows(ra_ref[j], rb_ref[j])):
            acc[blk] = acc[blk] + part * gj
    out = [h_ref[0, :, blk * q:(blk + 1) * q] + gf_ref[0, :, blk * q:(blk + 1) * q] * acc[blk] for blk in range(4)]
    if nw_ref is not None:
        ms = sum(jnp.sum(o * o, axis=-1, keepdims=True) for o in out) * (1.0 / (4 * q))
        scale = lax.rsqrt(ms + EPS)
        out = [o * scale * nw_ref[:, blk * q:(blk + 1) * q] for blk, o in enumerate(out)]
    for blk in range(4):
        o_ref[0, :, blk * q:(blk + 1) * q] = out[blk]


def _combine(h2, ra, rb, gate, mod, final_w, *, ctx):
    bsz, t_all, d = h2.shape
    tm = TOKEN_TILE
    tiles = t_all // tm
    ctx_tiles = ctx // tm
    rows = mod.shape[0] // 6
    pw = ra.shape[-1]
    skip = ctx_tiles if final_w is not None else 0
    per = lambda: pl.BlockSpec((TOP_K, tm, pw), lambda b, t: (0, b * tiles + t + skip, 0))
    in_specs = [pl.BlockSpec((1, tm, d), lambda b, t: (b, t + skip, 0)), per(), per(),
                pl.BlockSpec((1, tm, LANES), lambda b, t: (b, t + skip, 0)),
                pl.BlockSpec((1, 1, d), lambda b, t: _mod_row(5, ctx_tiles, rows)(b, t + skip))]
    args = [h2, ra, rb, gate, mod]
    if final_w is not None:
        in_specs.append(pl.BlockSpec((1, d), lambda b, t: (0, 0)))
        args.append(final_w)
    return pl.pallas_call(
        _combine_kernel,
        grid=(bsz, tiles - skip),
        in_specs=in_specs,
        out_specs=pl.BlockSpec((1, tm, d), lambda b, t: (b, t, 0)),
        out_shape=jax.ShapeDtypeStruct((bsz, t_all - skip * tm, d), F32),
        compiler_params=_params(("parallel", "parallel")),
        name="moe_combine_final" if final_w is not None else "moe_combine",
    )(*args)


def _gather_rows(xa, xb, indices):
    n = indices.shape[0]
    q = xa.shape[1]
    win = GATHER_WINDOW
    mesh = plsc.VectorSubcoreMesh(core_axis_name="core", subcore_axis_name="subcore")
    workers = mesh.num_cores * mesh.num_subcores
    per = n // (win * workers)
    assert per * win * workers == n, (n, win, workers)
    out = jax.ShapeDtypeStruct((n, q), xa.dtype)
    scratch = [pltpu.VMEM((per, win), jnp.int32), pltpu.VMEM((win, q), xa.dtype), pltpu.VMEM((win, q), xa.dtype),
               pltpu.SemaphoreType.DMA((4,))]

    @functools.partial(pl.kernel, out_type=(out, out), mesh=mesh, scratch_types=scratch)
    def gather(xa_hbm, xb_hbm, i_hbm, oa_hbm, ob_hbm, idx_vmem, buf_a, buf_b, sems):
        wid = lax.axis_index("core") * mesh.num_subcores + lax.axis_index("subcore")
        pltpu.sync_copy(i_hbm.at[wid], idx_vmem)

        @pl.loop(0, per)
        def _(s):
            rows = pl.ds(pl.multiple_of((wid * per + s) * win, win), win)
            ga = pltpu.async_copy(xa_hbm.at[idx_vmem.at[s]], buf_a, sems.at[0])
            gb = pltpu.async_copy(xb_hbm.at[idx_vmem.at[s]], buf_b, sems.at[1])
            ga.wait()
            wa = pltpu.async_copy(buf_a, oa_hbm.at[rows], sems.at[2])
            gb.wait()
            wb = pltpu.async_copy(buf_b, ob_hbm.at[rows], sems.at[3])
            wa.wait()
            wb.wait()

    return gather(xa, xb, indices.reshape(workers, per, win))


def _routed(xpa, xpb, idx, w_gate, w_up, w_down, layer):
    bsz, t_all, pw = xpa.shape
    n_tok = bsz * t_all
    n_assign = n_tok * TOP_K
    bm = EXPERT_ROWS
    n_blocks = -(-n_assign // bm) + N_EXPERTS
    n_slots = n_blocks * bm
    idx2d = idx.reshape(n_tok, LANES)
    rank, cnt = _route_ranks(idx2d)
    counts = cnt[0, :N_EXPERTS].astype(jnp.int32)
    padded = (counts + bm - 1) // bm * bm
    pad_end = jnp.cumsum(padded)
    offset = pad_end - padded
    first = jnp.cumsum(counts) - counts
    ek = idx2d[:, :TOP_K]
    slot = jnp.take(offset, ek) + rank[:, :TOP_K]
    block_expert = jnp.minimum(
        jnp.sum(pad_end[None, :] <= (jnp.arange(n_blocks, dtype=jnp.int32) * bm)[:, None], axis=1),
        N_EXPERTS - 1).astype(jnp.int32)
    token_sorted = jnp.argsort(ek.reshape(-1)) // TOP_K
    e_slot = jnp.repeat(block_expert, bm)
    r_slot = jnp.arange(n_slots, dtype=jnp.int32) - jnp.take(offset, e_slot)
    live = r_slot < jnp.take(counts, e_slot)
    filler = jnp.arange(n_slots, dtype=jnp.int32) % n_tok
    src = jnp.where(live, jnp.take(token_sorted, jnp.clip(jnp.take(first, e_slot) + r_slot, 0, n_assign - 1)), filler)
    xsa, xsb = _gather_rows(xpa.reshape(n_tok, pw), xpb.reshape(n_tok, pw), src.astype(jnp.int32))
    stack = lambda w: w.reshape((-1,) + w.shape[2:])
    table = jnp.concatenate([block_expert + layer * N_EXPERTS, (pad_end[-1:] // bm).astype(jnp.int32)])
    oa, ob = _expert_blocks(xsa, xsb, table, stack(w_gate), stack(w_up), stack(w_down))
    ra, rb = _gather_rows(oa, ob, slot.T.reshape(-1))
    return ra.reshape(TOP_K, n_tok, pw), rb.reshape(TOP_K, n_tok, pw)


def kernel(x, c, ctx, c_ctx, w_mod, b_mod, norm_mix, w_in, ssd_conv_w, ssd_conv_b, ssd_dt_bias, ssd_a_log, ssd_d,
           ssd_norm, hgrn_lb, hgrn_norm, gdn_conv_w, gdn_dt_bias, gdn_a_log, gdn_norm, w_out, norm_ffn, router_w,
           router_bias, exp_gate, exp_up, exp_down, sh_gate, sh_up, sh_down, norm_final):
    bsz, seq, d = x.shape
    n_ctx_tok = ctx.shape[1]
    depth = w_in.shape[0]
    assert d == D_MODEL and n_ctx_tok == TOKEN_TILE and seq % TOKEN_TILE == 0 and TOKEN_TILE % SCAN_CHUNK == 0
    assert TOKEN_TILE % GRID_W == 0 and seq % GRID_W == 0
    n_ctx = n_ctx_tok // SCAN_CHUNK

    p_lb = jax.nn.softmax(hgrn_lb.astype(F32), axis=0)
    lower_all = jnp.cumsum(p_lb, axis=0) - p_lb[0]

    layers = []
    for l in range(depth):
        nrm = jnp.concatenate([ssd_norm[l], jnp.tile(hgrn_norm[l], HG_WIDTH // HEAD),
                               jnp.tile(gdn_norm[l], GD_WIDTH // HEAD)]).astype(F32).reshape(1, d)
        layers.append(dict(
            w_p=_permute_w_in(w_in[l]).astype(BF16), nrm=nrm,
            rwp=jnp.pad(router_w[l].astype(F32), ((0, 0), (0, LANES - N_EXPERTS))),
            rbp=jnp.pad(router_bias[l].astype(F32), (0, LANES - N_EXPERTS)).reshape(1, LANES),
            w_out=w_out[l].astype(BF16), sg=sh_gate[l].astype(BF16), su=sh_up[l].astype(BF16),
            sd=sh_down[l].astype(BF16)))

    def forward(xg, cg, ctxg):
        nb = xg.shape[0]
        rows = -(-(nb + 1) // 8) * 8
        cond = jnp.zeros((rows, d), F32).at[:nb].set(cg).at[rows - 1].set(c_ctx)
        h = jnp.concatenate([ctxg, xg], axis=1)
        for l in range(depth):
            p = layers[l]
            mod = _modulation(cond, w_mod[l], b_mod[l]).reshape(rows * 6, 1, d)
            u = _inproj(h, mod, norm_mix[l], p["w_p"], ssd_conv_w[l], ssd_conv_b[l], gdn_conv_w[l], ctx=n_ctx_tok)
            ys = [_ssd_scan(u, ssd_dt_bias[l], ssd_a_log[l], ssd_d[l], rev=r, n_ctx=n_ctx) for r in (False, True)]
            yh = [_hgrn_scan(u, lower_all[l], rev=r, n_ctx=n_ctx) for r in (False, True)]
            yg = [_gdn_scan(u, gdn_dt_bias[l], gdn_a_log[l], rev=r, n_ctx=n_ctx) for r in (False, True)]
            h2, xpa, xpb, idx, gate = _outproj(ys, yh, yg, u, h, mod, p["nrm"], p["w_out"],
                                               norm_ffn[l].reshape(1, d), p["rwp"], p["rbp"], p["sg"], p["su"],
                                               p["sd"], ctx=n_ctx_tok)
            ra, rb = _routed(xpa, xpb, idx, exp_gate, exp_up, exp_down, l)
            last = l == depth - 1
            h = _combine(h2, ra, rb, gate, mod, norm_final.reshape(1, d) if last else None, ctx=n_ctx_tok)
        return h

    groups = SAMPLE_GROUPS if bsz % (SAMPLE_GROUPS * SCAN_BATCH) == 0 else 1
    gs = bsz // groups
    outs = [forward(x[g * gs:(g + 1) * gs], c[g * gs:(g + 1) * gs], ctx[g * gs:(g + 1) * gs]) for g in range(groups)]
    return outs[0] if groups == 1 else jnp.concatenate(outs, axis=0)
```

```python
import functools

import numpy as np
import jax
import jax.numpy as jnp
from jax import lax
from jax.experimental import pallas as pl
from jax.experimental.pallas import tpu as pltpu
from jax.experimental.pallas import tpu_sc as plsc

F32 = jnp.float32
BF16 = jnp.bfloat16

D_MODEL = 1024
GRID_W = 64
CONV_W = 5
EPS = 1e-6
MIN_LOWER = 1e-30
MASKED_EXPONENT = -1e30
HEAD = 64
SSD_HEADS = 8
SSD_WIDTH = 512
SSD_STATE = 128
SSD_GROUPS = 2
HG_WIDTH = 256
GD_WIDTH = 256
GD_HEADS = 4
N_EXPERTS = 64
TOP_K = 8
EXPERT_DIM = 256
ROUTED_SCALE = 2.5

LANES = 128
SCAN_CHUNK = 128
SAMPLE_GROUPS = 1
SCAN_BATCH = 4
BASE = 16
TOKEN_TILE = 256
EXPERT_ROWS = 1024
GATHER_WINDOW = 128
VMEM_LIMIT = 56 * 1024 * 1024

Z_OFF, X_OFF, BC_OFF = 0, 512, 1024
HQ_OFF, HFF_OFF, HFB_OFF, HI_OFF, HGATE_OFF = 1536, 1792, 2048, 2304, 2560
GQ_OFF, GK_OFF, GV_OFF, GGATE_OFF = 2816, 3072, 3328, 3584
SM_OFF = 3840
NCOLS = 3968
SSD_CONV = 1024
GDN_CONV = 768
SM_DT = (0, 8)
SM_A = (16, 20)
SM_B = (24, 28)


def _permute_w_in(w):
    d = w.shape[0]
    parts = [w[:, 0:1536],
             w[:, 1552:2832],
             w[:, 2832:3600],
             w[:, 3616:3872],
             w[:, 1536:1552],
             w[:, 3600:3616],
             jnp.zeros((d, LANES - 32), w.dtype)]
    return jnp.concatenate(parts, axis=1)


def _dot(a, b):
    return lax.dot_general(a, b, (((1,), (0,)), ((), ())), preferred_element_type=F32)


def _dot_nt(a, b):
    return lax.dot_general(a, b, (((1,), (1,)), ((), ())), preferred_element_type=F32)


def _split(a):
    hi = a.astype(BF16)
    lo = (a - hi.astype(F32)).astype(BF16)
    return hi, lo


def _dot3(a, b):
    ah, al = _split(a)
    bh, bl = _split(b)
    return _dot(ah, bh) + (_dot(ah, bl) + _dot(al, bh))


def _dot3_exact_rhs(a, b_bf16):
    ah, al = _split(a)
    return _dot(ah, b_bf16) + _dot(al, b_bf16)


def _silu(x):
    return x * jax.nn.sigmoid(x)


def _softplus(x):
    return jnp.maximum(x, 0.0) + jnp.log1p(jnp.exp(-jnp.abs(x)))


def _log_sigmoid(x):
    return jnp.minimum(x, 0.0) - jnp.log1p(jnp.exp(-jnp.abs(x)))


def _params(sem):
    return pltpu.CompilerParams(dimension_semantics=sem, vmem_limit_bytes=VMEM_LIMIT)


def _iota(shape, dim):
    return lax.broadcasted_iota(jnp.int32, shape, dim)


def _ones_where(mask):
    return jnp.where(mask, 1.0, 0.0).astype(BF16)


def _shr(x, div):
    return jnp.right_shift(x, int(np.log2(div)))


def _scan_mask(n, rev):
    r, c = _iota((n, n), 0), _iota((n, n), 1)
    return (c >= r) if rev else (c <= r)


def _expand_matrix(lane0, width):
    r, c = _iota((LANES, width), 0), _iota((LANES, width), 1)
    return _ones_where(r == lane0 + _shr(c, HEAD))


def _head_blocks(n):
    r, c = _iota((n, n), 0), _iota((n, n), 1)
    return _shr(r, HEAD) == _shr(c, HEAD)


def _cumsum_rows(x, rev):
    tri = _ones_where(_scan_mask(x.shape[0], rev))
    xh, xl = _split(x)
    return _dot(tri, xh) + _dot(tri, xl)


def _mod_kernel(s_ref, w_ref, b_ref, o_ref):
    s = _silu(s_ref[...])
    o_ref[...] = _dot(s.astype(BF16), w_ref[...].astype(BF16)) + b_ref[...]


def _modulation(cond, w, b):
    rows, d = cond.shape
    n = w.shape[1]
    bn = d
    return pl.pallas_call(
        _mod_kernel,
        grid=(n // bn,),
        in_specs=[pl.BlockSpec((rows, d), lambda j: (0, 0)),
                  pl.BlockSpec((d, bn), lambda j: (0, j)),
                  pl.BlockSpec((1, bn), lambda j: (0, j))],
        out_specs=pl.BlockSpec((rows, bn), lambda j: (0, j)),
        out_shape=jax.ShapeDtypeStruct((rows, n), F32),
        compiler_params=_params(("parallel",)),
        name="modulation",
    )(cond, w, b.reshape(1, n))


def _inproj_kernel(h_ref, sc_ref, sh_ref, nw_ref, w_ref, cws_ref, cbs_ref, cwg_ref, u_ref, pad_ref, *, ctx_tiles):
    tm = h_ref.shape[1]
    t = pl.program_id(1)
    h = h_ref[0]
    a = h * lax.rsqrt(jnp.mean(h * h, axis=-1, keepdims=True) + EPS) * nw_ref[...]
    a = a * (1.0 + sc_ref[0]) + sh_ref[0]
    ab = a.astype(BF16)
    seg = jnp.where(t < ctx_tiles, tm, GRID_W)
    pos = _iota((tm, 1), 0) & (seg - 1)
    half = CONV_W // 2
    masks = {o: (pos + o >= 0) & (pos + o < seg) for o in range(-half, half + 1) if o}

    halo = 8
    pad_ref[0:halo, :] = jnp.zeros((halo, pad_ref.shape[1]), F32)
    pad_ref[halo + tm:2 * halo + tm, :] = jnp.zeros((halo, pad_ref.shape[1]), F32)

    def conv(x, cw_ref, c0, wd, cb_ref):
        pad_ref[halo:halo + tm, 0:wd] = x
        acc = x * cw_ref[half:half + 1, c0:c0 + wd]
        for j in range(CONV_W):
            o = j - half
            if o == 0:
                continue
            shifted = pad_ref[halo + o:halo + o + tm, 0:wd]
            acc = acc + jnp.where(masks[o], shifted, 0.0) * cw_ref[j:j + 1, c0:c0 + wd]
        if cb_ref is not None:
            acc = acc + cb_ref[:, c0:c0 + wd]
        return _silu(acc)

    step = 2 * LANES
    for c0 in range(0, NCOLS, step):
        wd = min(step, NCOLS - c0)
        u = _dot(ab, w_ref[:, c0:c0 + wd])
        if X_OFF <= c0 < X_OFF + SSD_CONV:
            u = conv(u, cws_ref, c0 - X_OFF, wd, cbs_ref)
        elif GQ_OFF <= c0 < GQ_OFF + GDN_CONV:
            u = conv(u, cwg_ref, c0 - GQ_OFF, wd, None)
        u_ref[0, :, c0:c0 + wd] = u


def _mod_row(k, ctx_tiles, rows):
    return lambda b, t: (jnp.where(t < ctx_tiles, rows - 1, b) * 6 + k, 0, 0)


def _inproj(h, mod, norm_w, w_p, conv_s, bias_s, conv_g, *, ctx):
    bsz, t_all, d = h.shape
    tm = TOKEN_TILE
    ctx_tiles = ctx // tm
    rows = mod.shape[0] // 6
    return pl.pallas_call(
        functools.partial(_inproj_kernel, ctx_tiles=ctx_tiles),
        grid=(bsz, t_all // tm),
        in_specs=[pl.BlockSpec((1, tm, d), lambda b, t: (b, t, 0)),
                  pl.BlockSpec((1, 1, d), _mod_row(1, ctx_tiles, rows)),
                  pl.BlockSpec((1, 1, d), _mod_row(0, ctx_tiles, rows)),
                  pl.BlockSpec((1, d), lambda b, t: (0, 0)),
                  pl.BlockSpec((d, NCOLS), lambda b, t: (0, 0)),
                  pl.BlockSpec((CONV_W, SSD_CONV), lambda b, t: (0, 0)),
                  pl.BlockSpec((1, SSD_CONV), lambda b, t: (0, 0)),
                  pl.BlockSpec((CONV_W, GDN_CONV), lambda b, t: (0, 0))],
        out_specs=pl.BlockSpec((1, tm, NCOLS), lambda b, t: (b, t, 0)),
        out_shape=jax.ShapeDtypeStruct((bsz, t_all, NCOLS), F32),
        scratch_shapes=[pltpu.VMEM((tm + 16, 2 * LANES), F32)],
        compiler_params=_params(("parallel", "parallel")),
        name="in_projection",
    )(h, mod, mod, norm_w.reshape(1, d), w_p, conv_s, bias_s.reshape(1, -1), conv_g)


def _chunk_index(rev, n_ctx, n_all):
    if not rev:
        return lambda c: c
    return lambda c: jnp.where(c < n_ctx, n_ctx - 1 - c, n_all + n_ctx - 1 - c)


def _scan_call(body, u, col_blocks, extra, out_width, scratch, *, rev, n_ctx, name):
    bsz, t_all, _ = u.shape
    cn, bb = SCAN_CHUNK, SCAN_BATCH
    assert bsz % bb == 0
    n_all = t_all // cn
    cidx = _chunk_index(rev, n_ctx, n_all)

    def tok(width, off):
        return pl.BlockSpec((bb, cn, width), lambda b, c: (b, cidx(c), off // width))

    in_specs = [tok(w, off) for w, off in col_blocks]
    in_specs += [pl.BlockSpec(a.shape, lambda b, c: (0, 0)) for a in extra]
    return pl.pallas_call(
        body,
        grid=(bsz // bb, n_all),
        in_specs=in_specs,
        out_specs=tok(out_width, 0),
        out_shape=jax.ShapeDtypeStruct((bsz, t_all, out_width), F32),
        scratch_shapes=[scratch],
        compiler_params=_params(("parallel", "arbitrary")),
        name=name,
    )(*([u] * len(col_blocks)), *extra)


def _reset_state(st_ref):
    @pl.when(pl.program_id(1) == 0)
    def _():
        st_ref[...] = jnp.zeros_like(st_ref)


def _ssd_kernel(x_ref, bc_ref, sm_ref, bias_ref, arow_ref, drow_ref, y_ref, st_ref, *, rev):
    bb, cn = x_ref.shape[0], x_ref.shape[1]
    _reset_state(st_ref)
    lane0 = SM_DT[1] if rev else SM_DT[0]
    mask = _scan_mask(cn, rev)
    ex = _expand_matrix(lane0, SSD_WIDTH)
    gw = SSD_WIDTH // SSD_GROUPS
    hpg = SSD_HEADS // SSD_GROUPS
    lane_head = _shr(_iota((1, gw), 1), HEAD)
    last = 0 if rev else cn - 1
    for i in range(bb):
        xs = x_ref[i]
        bc = bc_ref[i]
        dt = _softplus(sm_ref[i] + bias_ref[...])
        da = dt * arow_ref[...]
        acs = _cumsum_rows(da, rev)
        acs_r = acs.T
        acs_e = _dot3_exact_rhs(acs, ex)
        dt_e = _dot3_exact_rhs(dt, ex)
        tot_e = acs_e[last:last + 1]
        xdt = xs * dt_e
        xdt_b = xdt.astype(BF16)
        wst = (xdt * jnp.exp(tot_e - acs_e)).astype(BF16)
        eacs = jnp.exp(acs_e)
        etot = jnp.exp(tot_e)
        for g in range(SSD_GROUPS):
            bm = bc[:, g * SSD_STATE:(g + 1) * SSD_STATE]
            cm = bc[:, (SSD_GROUPS + g) * SSD_STATE:(SSD_GROUPS + g + 1) * SSD_STATE]
            bmb, cmb = bm.astype(BF16), cm.astype(BF16)
            cb = _dot_nt(cmb, bmb)
            xg = xdt_b[:, g * gw:(g + 1) * gw]
            yd = jnp.zeros((cn, gw), F32)
            for r in range(hpg):
                ln = lane0 + g * hpg + r
                diff = acs[:, ln:ln + 1] - acs_r[ln:ln + 1, :]
                dec = jnp.where(mask, jnp.exp(jnp.where(mask, diff, 0.0)), 0.0)
                yh = _dot((cb * dec).astype(BF16), xg)
                yd = jnp.where(lane_head == r, yh, yd)
            st = st_ref[i * SSD_GROUPS + g]
            yo = _dot(cmb, st.astype(BF16)) * eacs[:, g * gw:(g + 1) * gw]
            y = yd + yo
            if drow_ref is not None:
                y = y + drow_ref[:, g * gw:(g + 1) * gw] * xs[:, g * gw:(g + 1) * gw]
            y_ref[i, :, g * gw:(g + 1) * gw] = y
            st_ref[i * SSD_GROUPS + g] = (st * etot[:, g * gw:(g + 1) * gw]
                                          + _dot(bm.T.astype(BF16), wst[:, g * gw:(g + 1) * gw]))


def _ssd_kernel_no_skip(x_ref, bc_ref, sm_ref, bias_ref, arow_ref, y_ref, st_ref, *, rev):
    _ssd_kernel(x_ref, bc_ref, sm_ref, bias_ref, arow_ref, None, y_ref, st_ref, rev=rev)


def _ssd_scan(u, dt_bias, a_log, d_skip, *, rev, n_ctx):
    di = 1 if rev else 0
    lane0 = SM_DT[di]
    bias = jnp.zeros((1, LANES), F32).at[0, lane0:lane0 + SSD_HEADS].set(dt_bias[di])
    arow = jnp.zeros((1, LANES), F32).at[0, lane0:lane0 + SSD_HEADS].set(-jnp.exp(a_log[di].astype(F32)))
    extra = [bias, arow]
    if rev:
        body = functools.partial(_ssd_kernel_no_skip, rev=rev)
    else:
        extra.append(jnp.repeat(d_skip.astype(F32), HEAD).reshape(1, SSD_WIDTH))
        body = functools.partial(_ssd_kernel, rev=rev)
    scratch = pltpu.VMEM((SCAN_BATCH * SSD_GROUPS, SSD_STATE, SSD_WIDTH // SSD_GROUPS), F32)
    return _scan_call(body, u, [(SSD_WIDTH, X_OFF), (512, BC_OFF), (LANES, SM_OFF)], extra, SSD_WIDTH, scratch,
                      rev=rev, n_ctx=n_ctx, name="ssd_scan_bwd" if rev else "ssd_scan_fwd")


def _hgrn_kernel(q_ref, f_ref, i_ref, low_ref, y_ref, st_ref, *, rev):
    bb, cn = q_ref.shape[0], q_ref.shape[1]
    wdt = HG_WIDTH
    heads = wdt // HEAD
    _reset_state(st_ref)
    lower = low_ref[0:1]
    log_lower = low_ref[1:2]
    hb = _head_blocks(wdt)
    bd = _ones_where(hb)
    lane_head = _shr(_iota((1, wdt), 1), HEAD)
    last = 0 if rev else cn - 1

    for i in range(bb):
        fr = f_ref[i]
        qs = _silu(q_ref[i])
        v = i_ref[i]
        c = log_lower - fr
        tail = lambda z: jnp.log(1.0 + jnp.exp(-jnp.abs(z)))
        logf = (jnp.minimum(fr, 0.0) - tail(fr)) + (jnp.maximum(c, 0.0) + tail(c))
        kg = (1.0 - lower) * jax.nn.sigmoid(-fr)
        b = _cumsum_rows(logf, rev)
        vb = v.astype(BF16)

        st = st_ref[i]
        y_ref[i] = _dot_nt((qs * jnp.exp(b)).astype(BF16), st.astype(BF16))
        b_last = b[last:last + 1]
        kend = (kg * jnp.exp(b_last - b)).astype(BF16)
        st_ref[i] = st * jnp.exp(b_last) + jnp.where(hb, _dot(v.T.astype(BF16), kend), 0.0)

        def offdiag(t0, t1, s0, s1, r):
            br = b[r:r + 1]
            qp = qs[t0:t1] * jnp.exp(b[t0:t1] - br)
            kp = (kg[s0:s1] * jnp.exp(br - b[s0:s1])).astype(BF16)
            nt = t1 - t0
            qstack = jnp.concatenate([jnp.where(lane_head == h, qp, 0.0) for h in range(heads)], axis=0)
            att = _dot_nt(qstack.astype(BF16), kp)
            res = _dot(att.astype(BF16), vb[s0:s1])
            out = jnp.zeros((nt, wdt), F32)
            for h in range(heads):
                out = jnp.where(lane_head == h, res[h * nt:(h + 1) * nt], out)
            y_ref[i, t0:t1, :] += out

        def diag(t0, t1):
            n = t1 - t0
            bt = b[t0:t1]
            si, ti = _iota((n, n, wdt), 0), _iota((n, n, wdt), 1)
            m3 = (si >= ti) if rev else (si <= ti)
            diff = bt[None, :, :] - bt[:, None, :]
            w = jnp.exp(jnp.where(m3, diff, MASKED_EXPONENT))
            p = w * qs[t0:t1][None, :, :] * kg[t0:t1][:, None, :]
            r2 = _dot(p.reshape(n * n, wdt).astype(BF16), bd)
            y_ref[i, t0:t1, :] += jnp.sum(r2.reshape(n, n, wdt) * v[t0:t1][:, None, :], axis=0)

        def block(lo, hi):
            if hi - lo <= BASE:
                diag(lo, hi)
                return
            mid = (lo + hi) // 2
            if rev:
                offdiag(lo, mid, mid, hi, mid)
            else:
                offdiag(mid, hi, lo, mid, mid - 1)
            block(lo, mid)
            block(mid, hi)

        block(0, cn)


def _hgrn_scan(u, lower, *, rev, n_ctx):
    f_off = HFB_OFF if rev else HFF_OFF
    low = jnp.stack([lower, jnp.log(jnp.maximum(lower, MIN_LOWER))]).astype(F32)
    low = jnp.concatenate([low, jnp.zeros((6, HG_WIDTH), F32)], axis=0)
    scratch = pltpu.VMEM((SCAN_BATCH, HG_WIDTH, HG_WIDTH), F32)
    return _scan_call(functools.partial(_hgrn_kernel, rev=rev), u,
                      [(HG_WIDTH, HQ_OFF), (HG_WIDTH, f_off), (HG_WIDTH, HI_OFF)], [low], HG_WIDTH, scratch,
                      rev=rev, n_ctx=n_ctx, name="hgrn_scan_bwd" if rev else "hgrn_scan_fwd")


def _bdot(a, b):
    return lax.dot_general(a, b, (((2,), (1,)), ((0,), (0,))), preferred_element_type=F32)


def _unit_inverse_delta(a):
    n_rows = a.shape[-1]
    r, c = _iota((n_rows, n_rows), 0), _iota((n_rows, n_rows), 1)
    d = jnp.where((_shr(r, BASE) == _shr(c, BASE))[None], a, 0.0)
    db = d.astype(BF16)
    p = _bdot(db, db)
    n = p - d - _bdot(db, p.astype(BF16))
    e = 4
    while e < BASE:
        pb = p.astype(BF16)
        p = _bdot(pb, pb)
        n = n + p + _bdot(n.astype(BF16), p.astype(BF16))
        e *= 2
    size = BASE
    while size < n_rows:
        big = 2 * size
        off = (_shr(r, big) == _shr(c, big)) & (_shr(r, size) != _shr(c, size))
        a_off = jnp.where(off[None], a, 0.0)
        nb = n.astype(BF16)
        m = a_off + _bdot(nb, a_off.astype(BF16))
        n = n - (m + _bdot(m.astype(BF16), nb))
        size = big
    return n


def _gdn_kernel(q_ref, k_ref, v_ref, sm_ref, bias_ref, arow_ref, y_ref, st_ref, *, rev):
    bb, cn = q_ref.shape[0], q_ref.shape[1]
    wdt = GD_WIDTH
    _reset_state(st_ref)
    hb = _head_blocks(wdt)
    bd = _ones_where(hb)
    lane_head = _shr(_iota((1, wdt), 1), HEAD)
    la = SM_A[1] if rev else SM_A[0]
    lb = SM_B[1] if rev else SM_B[0]
    ex_a, ex_b = _expand_matrix(la, wdt), _expand_matrix(lb, wdt)
    mask = _scan_mask(cn, rev)
    strict = mask & (_iota((cn, cn), 0) != _iota((cn, cn), 1))
    last = 0 if rev else cn - 1
    pre, a_all = [], []
    for i in range(bb):
        q, k, v = q_ref[i], k_ref[i], v_ref[i]
        q = q * lax.rsqrt(_dot3_exact_rhs(q * q, bd) + 1e-6) * (HEAD ** -0.5)
        k = k * lax.rsqrt(_dot3_exact_rhs(k * k, bd) + 1e-6)
        sm = sm_ref[i]
        g = arow_ref[...] * _softplus(sm + bias_ref[...])
        beta = jax.nn.sigmoid(sm)
        gam = _cumsum_rows(g, rev)
        gam_r = gam.T
        gam_e = _dot3_exact_rhs(gam, ex_a)
        beta_e = _dot3_exact_rhs(beta, ex_b)
        tot_e = gam_e[last:last + 1]
        egam = jnp.exp(gam_e)
        kb = k * beta_e
        kbf = k.astype(BF16)
        rhs = jnp.concatenate([v * beta_e, kb * egam], axis=1)
        qks = []
        for h in range(GD_HEADS):
            ln = la + h
            diff = gam[:, ln:ln + 1] - gam_r[ln:ln + 1, :]
            dec = jnp.where(mask, jnp.exp(jnp.where(mask, diff, 0.0)), 0.0)
            hm = lane_head == h
            kk = _dot_nt(jnp.where(hm, kb, 0.0).astype(BF16), kbf)
            a_all.append(jnp.where(strict, kk * dec, 0.0))
            qks.append((_dot_nt(jnp.where(hm, q, 0.0).astype(BF16), kbf) * dec).astype(BF16))
        pre.append((q, k, rhs, qks, egam, gam_e, tot_e))
    n_all = _unit_inverse_delta(jnp.stack(a_all)).astype(BF16)
    for i in range(bb):
        q, k, rhs, qks, egam, gam_e, tot_e = pre[i]
        rhs_b = rhs.astype(BF16)
        u_all = jnp.zeros((cn, wdt), F32)
        w_all = jnp.zeros((cn, wdt), F32)
        for h in range(GD_HEADS):
            sol = rhs + _dot(n_all[i * GD_HEADS + h], rhs_b)
            hm = lane_head == h
            u_all = jnp.where(hm, sol[:, :wdt], u_all)
            w_all = jnp.where(hm, sol[:, wdt:], w_all)
        st = st_ref[i]
        stb = st.astype(BF16)
        v_new = u_all - _dot(w_all.astype(BF16), stb)
        vnb = v_new.astype(BF16)
        o = _dot((q * egam).astype(BF16), stb)
        for h in range(GD_HEADS):
            o = o + jnp.where(lane_head == h, _dot(qks[h], vnb), 0.0)
        y_ref[i] = o
        kend = k * jnp.exp(tot_e - gam_e)
        st_ref[i] = st * jnp.exp(tot_e) + jnp.where(hb, _dot(kend.T.astype(BF16), vnb), 0.0)


def _gdn_scan(u, dt_bias, a_log, *, rev, n_ctx):
    di = 1 if rev else 0
    la = SM_A[di]
    bias = jnp.zeros((1, LANES), F32).at[0, la:la + GD_HEADS].set(dt_bias[di])
    arow = jnp.zeros((1, LANES), F32).at[0, la:la + GD_HEADS].set(-jnp.exp(a_log[di].astype(F32)))
    scratch = pltpu.VMEM((SCAN_BATCH, GD_WIDTH, GD_WIDTH), F32)
    return _scan_call(functools.partial(_gdn_kernel, rev=rev), u,
                      [(GD_WIDTH, GQ_OFF), (GD_WIDTH, GK_OFF), (GD_WIDTH, GV_OFF), (LANES, SM_OFF)], [bias, arow],
                      GD_WIDTH, scratch, rev=rev, n_ctx=n_ctx, name="gdn_scan_bwd" if rev else "gdn_scan_fwd")


def _pack_rows(x):
    half = x.shape[1] // 2
    q = half // 2
    xr = x.astype(BF16).astype(F32)
    lo = lax.bitcast_convert_type(xr[:, :half], jnp.uint32)
    hi = lax.bitcast_convert_type(xr[:, half:], jnp.uint32)
    word = (lo >> 16) | (hi & jnp.uint32(0xFFFF0000))
    return word[:, :q], word[:, q:]


def _unpack_rows(wa, wb):
    mask = jnp.uint32(0xFFFF0000)
    f = lambda w: lax.bitcast_convert_type(w, F32)
    return f(wa << 16), f(wb << 16), f(wa & mask), f(wb & mask)


def _group_rms(y, group, bd):
    ms = _dot3_exact_rhs(y * y, bd) * (1.0 / group)
    return y * lax.rsqrt(ms + EPS)


def _outproj_kernel(ysf, ysb, yhf, yhb, ygf, ygb, z_ref, hg_ref, gg_ref, h_ref, gm_ref, scf_ref, shf_ref, gf_ref,
                    nrm_ref, wout_ref, nf_ref, rw_ref, rb_ref, sg_ref, su_ref, sd_ref,
                    h2_ref, xpa_ref, xpb_ref, idx_ref, gate_ref):
    tm = h_ref.shape[1]
    gw = SSD_WIDTH // SSD_GROUPS
    r, c = _iota((SSD_WIDTH, SSD_WIDTH), 0), _iota((SSD_WIDTH, SSD_WIDTH), 1)
    bd_s = _ones_where(_shr(r, gw) == _shr(c, gw))
    bd_h = _ones_where(_head_blocks(HG_WIDTH))
    ys = (ysf[0] + ysb[0]) * _silu(z_ref[0])
    ys = _group_rms(ys, gw, bd_s) * nrm_ref[:, 0:SSD_WIDTH]
    yh = _group_rms(yhf[0] + yhb[0], HEAD, bd_h) * nrm_ref[:, SSD_WIDTH:SSD_WIDTH + HG_WIDTH] * _silu(hg_ref[0])
    yg = _group_rms(ygf[0] + ygb[0], HEAD, bd_h) * nrm_ref[:, SSD_WIDTH + HG_WIDTH:] * _silu(gg_ref[0])
    y = jnp.concatenate([ys, yh, yg], axis=1).astype(BF16)
    h1 = h_ref[0] + gm_ref[0] * _dot(y, wout_ref[...])
    xf = h1 * lax.rsqrt(jnp.mean(h1 * h1, axis=-1, keepdims=True) + EPS) * nf_ref[...]
    xf = xf * (1.0 + scf_ref[0]) + shf_ref[0]
    xb = xf.astype(BF16)
    xpa_ref[0], xpb_ref[0] = _pack_rows(xf)
    hid = _silu(_dot(xb, sg_ref[...])) * _dot(xb, su_ref[...])
    h2_ref[0] = h1 + gf_ref[0] * _dot(hid.astype(BF16), sd_ref[...])
    scores = jax.nn.sigmoid(_dot3(xf, rw_ref[...]))
    lane = _iota((tm, LANES), 1)
    sel = jnp.where(lane < N_EXPERTS, scores + rb_ref[...], -jnp.inf)
    rank_code = (LANES - lane).astype(F32)
    idx_f = jnp.zeros((tm, LANES), F32)
    gate_out = jnp.zeros((tm, LANES), F32)
    gsum = jnp.zeros((tm, 1), F32)
    for j in range(TOP_K):
        m = jnp.max(sel, axis=-1, keepdims=True)
        code = jnp.max(jnp.where(sel == m, rank_code, 0.0), axis=-1, keepdims=True)
        hit = rank_code == code
        gv = jnp.sum(jnp.where(hit, scores, 0.0), axis=-1, keepdims=True)
        idx_f = jnp.where(lane == j, LANES - code, idx_f)
        gate_out = jnp.where(lane == j, gv, gate_out)
        gsum = gsum + gv
        sel = jnp.where(hit, -jnp.inf, sel)
    idx_ref[0] = idx_f.astype(jnp.int32)
    gate_ref[0] = gate_out / gsum * ROUTED_SCALE


def _outproj(ys, yh, yg, u, h, mod, nrm, w_out, norm_ffn, router_wp, router_bp, sg, su, sd, *, ctx):
    bsz, t_all, d = h.shape
    tm = TOKEN_TILE
    ctx_tiles = ctx // tm
    rows = mod.shape[0] // 6
    mrow = lambda k: pl.BlockSpec((1, 1, d), _mod_row(k, ctx_tiles, rows))
    tok = lambda w, j=0: pl.BlockSpec((1, tm, w), lambda b, t: (b, t, j))
    full = lambda a: pl.BlockSpec(a.shape, lambda b, t: (0,) * a.ndim)
    in_specs = [tok(SSD_WIDTH), tok(SSD_WIDTH), tok(HG_WIDTH), tok(HG_WIDTH), tok(GD_WIDTH), tok(GD_WIDTH),
                tok(SSD_WIDTH, Z_OFF // SSD_WIDTH), tok(HG_WIDTH, HGATE_OFF // HG_WIDTH),
                tok(GD_WIDTH, GGATE_OFF // GD_WIDTH),
                tok(d), mrow(2), mrow(4), mrow(3), mrow(5),
                full(nrm), full(w_out), full(norm_ffn), full(router_wp), full(router_bp), full(sg), full(su), full(sd)]
    pw = d // 4
    out_shape = (jax.ShapeDtypeStruct((bsz, t_all, d), F32),
                 jax.ShapeDtypeStruct((bsz, t_all, pw), jnp.uint32), jax.ShapeDtypeStruct((bsz, t_all, pw), jnp.uint32),
                 jax.ShapeDtypeStruct((bsz, t_all, LANES), jnp.int32), jax.ShapeDtypeStruct((bsz, t_all, LANES), F32))
    out_specs = (tok(d), tok(pw), tok(pw), tok(LANES), tok(LANES))
    return pl.pallas_call(
        _outproj_kernel,
        grid=(bsz, t_all // tm),
        in_specs=in_specs,
        out_specs=out_specs,
        out_shape=out_shape,
        compiler_params=_params(("parallel", "parallel")),
        name="out_projection_router",
    )(ys[0], ys[1], yh[0], yh[1], yg[0], yg[1], u, u, u, h, mod, mod, mod, mod,
      nrm, w_out, norm_ffn, router_wp, router_bp, sg, su, sd)


def _rank_kernel(idx_ref, rank_ref, exp_ref, cnt_ref, base_ref):
    tm = idx_ref.shape[0]

    @pl.when(pl.program_id(0) == 0)
    def _():
        base_ref[...] = jnp.zeros_like(base_ref)

    idx = idx_ref[...]
    lane = _iota((tm, LANES), 1)
    hits = [lane == idx[:, j:j + 1] for j in range(TOP_K)]
    m = jnp.zeros((tm, LANES), F32)
    for hit in hits:
        m = m + jnp.where(hit, 1.0, 0.0)
    before = _ones_where(_iota((tm, tm), 1) < _iota((tm, tm), 0))
    base = base_ref[...]
    val = _dot(before, m.astype(BF16)) + base
    out = jnp.zeros((tm, LANES), F32)
    for j, hit in enumerate(hits):
        rj = jnp.sum(jnp.where(hit, val, 0.0), axis=-1, keepdims=True)
        out = jnp.where(lane == j, rj, out)
    rank_ref[...] = out.T[0:TOP_K].astype(jnp.int32)
    exp_ref[...] = idx.astype(F32).T[0:TOP_K].astype(jnp.int32)
    total = base + jnp.sum(m, axis=0, keepdims=True)
    base_ref[...] = total
    cnt_ref[...] = total


def _route_ranks(idx2d):
    n_tok = idx2d.shape[0]
    tm = TOKEN_TILE
    per_choice = lambda: pl.BlockSpec((TOP_K, tm), lambda i: (0, i))
    return pl.pallas_call(
        _rank_kernel,
        grid=(n_tok // tm,),
        in_specs=[pl.BlockSpec((tm, LANES), lambda i: (i, 0))],
        out_specs=(per_choice(), per_choice(), pl.BlockSpec((1, LANES), lambda i: (0, 0))),
        out_shape=(jax.ShapeDtypeStruct((TOP_K, n_tok), jnp.int32), jax.ShapeDtypeStruct((TOP_K, n_tok), jnp.int32),
                   jax.ShapeDtypeStruct((1, LANES), F32)),
        scratch_shapes=[pltpu.VMEM((1, LANES), F32)],
        compiler_params=_params(("arbitrary",)),
        name="route_ranks",
    )(idx2d)


def _expert_kernel(be_ref, xa_ref, xb_ref, wg_ref, wu_ref, wd_ref, oa_ref, ob_ref, wgb_ref, wub_ref, wdb_ref):
    q = xa_ref.shape[1]
    i = pl.program_id(0)
    n_live = be_ref[be_ref.shape[0] - 1]

    @pl.when((i == 0) | (be_ref[i] != be_ref[jnp.maximum(i - 1, 0)]))
    def _():
        wgb_ref[...] = wg_ref[0].astype(BF16)
        wub_ref[...] = wu_ref[0].astype(BF16)
        wdb_ref[...] = wd_ref[0].astype(BF16)

    @pl.when(i < n_live)
    def _():
        parts = [p.astype(BF16) for p in _unpack_rows(xa_ref[...], xb_ref[...])]

        def proj(w_ref):
            acc = _dot(parts[0], w_ref[0:q, :])
            for j in range(1, 4):
                acc = acc + _dot(parts[j], w_ref[j * q:(j + 1) * q, :])
            return acc

        hid = _silu(proj(wgb_ref)) * proj(wub_ref)
        oa_ref[...], ob_ref[...] = _pack_rows(_dot(hid.astype(BF16), wdb_ref[...]))

    @pl.when(i >= n_live)
    def _():
        oa_ref[...] = jnp.zeros_like(oa_ref)
        ob_ref[...] = jnp.zeros_like(ob_ref)


def _expert_blocks(xsa, xsb, block_expert, w_gate, w_up, w_down):
    n_rows, pw = xsa.shape
    bm = EXPERT_ROWS
    d, e_dim = w_gate.shape[1], w_gate.shape[2]
    assert block_expert.shape[0] == n_rows // bm + 1
    row = lambda: pl.BlockSpec((bm, pw), lambda i, be: (i, 0))
    grid_spec = pltpu.PrefetchScalarGridSpec(
        num_scalar_prefetch=1,
        grid=(n_rows // bm,),
        in_specs=[row(), row(),
                  pl.BlockSpec((1, d, e_dim), lambda i, be: (be[i], 0, 0)),
                  pl.BlockSpec((1, d, e_dim), lambda i, be: (be[i], 0, 0)),
                  pl.BlockSpec((1, e_dim, d), lambda i, be: (be[i], 0, 0))],
        out_specs=(row(), row()),
        scratch_shapes=[pltpu.VMEM((d, e_dim), BF16), pltpu.VMEM((d, e_dim), BF16), pltpu.VMEM((e_dim, d), BF16)],
    )
    return pl.pallas_call(
        _expert_kernel,
        grid_spec=grid_spec,
        out_shape=(jax.ShapeDtypeStruct((n_rows, pw), jnp.uint32), jax.ShapeDtypeStruct((n_rows, pw), jnp.uint32)),
        compiler_params=_params(("arbitrary",)),
        name="routed_experts",
    )(block_expert, xsa, xsb, w_gate, w_up, w_down)


def _combine_kernel(h_ref, ra_ref, rb_ref, gate_ref, gf_ref, *rest):
    nw_ref, o_ref = (rest[0], rest[1]) if len(rest) == 2 else (None, rest[0])
    tm, q = ra_ref.shape[1], ra_ref.shape[2]
    gate = gate_ref[0]
    acc = [jnp.zeros((tm, q), F32) for _ in range(4)]
    for j in range(TOP_K):
        gj = gate[:, j:j + 1]
        for blk, part in enumerate(_unpack_rows(ra_ref[j], rb_ref[j])):
            acc[blk] = acc[blk] + part * gj
    out = [h_ref[0, :, blk * q:(blk + 1) * q] + gf_ref[0, :, blk * q:(blk + 1) * q] * acc[blk] for blk in range(4)]
    if nw_ref is not None:
        ms = sum(jnp.sum(o * o, axis=-1, keepdims=True) for o in out) * (1.0 / (4 * q))
        scale = lax.rsqrt(ms + EPS)
        out = [o * scale * nw_ref[:, blk * q:(blk + 1) * q] for blk, o in enumerate(out)]
    for blk in range(4):
        o_ref[0, :, blk * q:(blk + 1) * q] = out[blk]


def _combine(h2, ra, rb, gate, mod, final_w, *, ctx):
    bsz, t_all, d = h2.shape
    tm = TOKEN_TILE
    tiles = t_all // tm
    ctx_tiles = ctx // tm
    rows = mod.shape[0] // 6
    pw = ra.shape[-1]
    skip = ctx_tiles if final_w is not None else 0
    per = lambda: pl.BlockSpec((TOP_K, tm, pw), lambda b, t: (0, b * tiles + t + skip, 0))
    in_specs = [pl.BlockSpec((1, tm, d), lambda b, t: (b, t + skip, 0)), per(), per(),
                pl.BlockSpec((1, tm, LANES), lambda b, t: (b, t + skip, 0)),
                pl.BlockSpec((1, 1, d), lambda b, t: _mod_row(5, ctx_tiles, rows)(b, t + skip))]
    args = [h2, ra, rb, gate, mod]
    if final_w is not None:
        in_specs.append(pl.BlockSpec((1, d), lambda b, t: (0, 0)))
        args.append(final_w)
    return pl.pallas_call(
        _combine_kernel,
        grid=(bsz, tiles - skip),
        in_specs=in_specs,
        out_specs=pl.BlockSpec((1, tm, d), lambda b, t: (b, t, 0)),
        out_shape=jax.ShapeDtypeStruct((bsz, t_all - skip * tm, d), F32),
        compiler_params=_params(("parallel", "parallel")),
        name="moe_combine_final" if final_w is not None else "moe_combine",
    )(*args)


def _gather_rows(xa, xb, indices):
    n = indices.shape[0]
    q = xa.shape[1]
    win = GATHER_WINDOW
    mesh = plsc.VectorSubcoreMesh(core_axis_name="core", subcore_axis_name="subcore")
    workers = mesh.num_cores * mesh.num_subcores
    per = n // (win * workers)
    assert per * win * workers == n, (n, win, workers)
    out = jax.ShapeDtypeStruct((n, q), xa.dtype)
    scratch = [pltpu.VMEM((per, win), jnp.int32), pltpu.VMEM((win, q), xa.dtype), pltpu.VMEM((win, q), xa.dtype),
               pltpu.SemaphoreType.DMA((4,))]

    @functools.partial(pl.kernel, out_type=(out, out), mesh=mesh, scratch_types=scratch)
    def gather(xa_hbm, xb_hbm, i_hbm, oa_hbm, ob_hbm, idx_vmem, buf_a, buf_b, sems):
        wid = lax.axis_index("core") * mesh.num_subcores + lax.axis_index("subcore")
        pltpu.sync_copy(i_hbm.at[wid], idx_vmem)

        @pl.loop(0, per)
        def _(s):
            rows = pl.ds(pl.multiple_of((wid * per + s) * win, win), win)
            ga = pltpu.async_copy(xa_hbm.at[idx_vmem.at[s]], buf_a, sems.at[0])
            gb = pltpu.async_copy(xb_hbm.at[idx_vmem.at[s]], buf_b, sems.at[1])
            ga.wait()
            wa = pltpu.async_copy(buf_a, oa_hbm.at[rows], sems.at[2])
            gb.wait()
            wb = pltpu.async_copy(buf_b, ob_hbm.at[rows], sems.at[3])
            wa.wait()
            wb.wait()

    return gather(xa, xb, indices.reshape(workers, per, win))


def _routed(xpa, xpb, idx, w_gate, w_up, w_down, layer):
    bsz, t_all, pw = xpa.shape
    n_tok = bsz * t_all
    n_assign = n_tok * TOP_K
    bm = EXPERT_ROWS
    n_blocks = -(-n_assign // bm) + N_EXPERTS
    n_slots = n_blocks * bm
    idx2d = idx.reshape(n_tok, LANES)
    rank_t, exp_t, cnt = _route_ranks(idx2d)
    counts = cnt[0, :N_EXPERTS].astype(jnp.int32)
    padded = (counts + bm - 1) // bm * bm
    pad_end = jnp.cumsum(padded)
    offset = pad_end - padded
    first = jnp.cumsum(counts) - counts
    ek = idx2d[:, :TOP_K]
    slot_t = jnp.take(offset, exp_t) + rank_t
    block_expert = jnp.minimum(
        jnp.sum(pad_end[None, :] <= (jnp.arange(n_blocks, dtype=jnp.int32) * bm)[:, None], axis=1),
        N_EXPERTS - 1).astype(jnp.int32)
    token_sorted = jnp.argsort(ek.reshape(-1)) // TOP_K
    e_slot = jnp.repeat(block_expert, bm)
    r_slot = jnp.arange(n_slots, dtype=jnp.int32) - jnp.take(offset, e_slot)
    live = r_slot < jnp.take(counts, e_slot)
    filler = jnp.arange(n_slots, dtype=jnp.int32) % n_tok
    src = jnp.where(live, jnp.take(token_sorted, jnp.clip(jnp.take(first, e_slot) + r_slot, 0, n_assign - 1)), filler)
    xsa, xsb = _gather_rows(xpa.reshape(n_tok, pw), xpb.reshape(n_tok, pw), src.astype(jnp.int32))
    stack = lambda w: w.reshape((-1,) + w.shape[2:])
    table = jnp.concatenate([block_expert + layer * N_EXPERTS, (pad_end[-1:] // bm).astype(jnp.int32)])
    oa, ob = _expert_blocks(xsa, xsb, table, stack(w_gate), stack(w_up), stack(w_down))
    ra, rb = _gather_rows(oa, ob, slot_t.reshape(-1))
    return ra.reshape(TOP_K, n_tok, pw), rb.reshape(TOP_K, n_tok, pw)


def kernel(x, c, ctx, c_ctx, w_mod, b_mod, norm_mix, w_in, ssd_conv_w, ssd_conv_b, ssd_dt_bias, ssd_a_log, ssd_d,
           ssd_norm, hgrn_lb, hgrn_norm, gdn_conv_w, gdn_dt_bias, gdn_a_log, gdn_norm, w_out, norm_ffn, router_w,
           router_bias, exp_gate, exp_up, exp_down, sh_gate, sh_up, sh_down, norm_final):
    bsz, seq, d = x.shape
    n_ctx_tok = ctx.shape[1]
    depth = w_in.shape[0]
    assert d == D_MODEL and n_ctx_tok == TOKEN_TILE and seq % TOKEN_TILE == 0 and TOKEN_TILE % SCAN_CHUNK == 0
    assert TOKEN_TILE % GRID_W == 0 and seq % GRID_W == 0
    n_ctx = n_ctx_tok // SCAN_CHUNK

    p_lb = jax.nn.softmax(hgrn_lb.astype(F32), axis=0)
    lower_all = jnp.cumsum(p_lb, axis=0) - p_lb[0]

    layers = []
    for l in range(depth):
        nrm = jnp.concatenate([ssd_norm[l], jnp.tile(hgrn_norm[l], HG_WIDTH // HEAD),
                               jnp.tile(gdn_norm[l], GD_WIDTH // HEAD)]).astype(F32).reshape(1, d)
        layers.append(dict(
            w_p=_permute_w_in(w_in[l]).astype(BF16), nrm=nrm,
            rwp=jnp.pad(router_w[l].astype(F32), ((0, 0), (0, LANES - N_EXPERTS))),
            rbp=jnp.pad(router_bias[l].astype(F32), (0, LANES - N_EXPERTS)).reshape(1, LANES),
            w_out=w_out[l].astype(BF16), sg=sh_gate[l].astype(BF16), su=sh_up[l].astype(BF16),
            sd=sh_down[l].astype(BF16)))

    def forward(xg, cg, ctxg):
        nb = xg.shape[0]
        rows = -(-(nb + 1) // 8) * 8
        cond = jnp.zeros((rows, d), F32).at[:nb].set(cg).at[rows - 1].set(c_ctx)
        h = jnp.concatenate([ctxg, xg], axis=1)
        for l in range(depth):
            p = layers[l]
            mod = _modulation(cond, w_mod[l], b_mod[l]).reshape(rows * 6, 1, d)
            u = _inproj(h, mod, norm_mix[l], p["w_p"], ssd_conv_w[l], ssd_conv_b[l], gdn_conv_w[l], ctx=n_ctx_tok)
            ys = [_ssd_scan(u, ssd_dt_bias[l], ssd_a_log[l], ssd_d[l], rev=r, n_ctx=n_ctx) for r in (False, True)]
            yh = [_hgrn_scan(u, lower_all[l], rev=r, n_ctx=n_ctx) for r in (False, True)]
            yg = [_gdn_scan(u, gdn_dt_bias[l], gdn_a_log[l], rev=r, n_ctx=n_ctx) for r in (False, True)]
            h2, xpa, xpb, idx, gate = _outproj(ys, yh, yg, u, h, mod, p["nrm"], p["w_out"],
                                               norm_ffn[l].reshape(1, d), p["rwp"], p["rbp"], p["sg"], p["su"],
                                               p["sd"], ctx=n_ctx_tok)
            ra, rb = _routed(xpa, xpb, idx, exp_gate, exp_up, exp_down, l)
            last = l == depth - 1
            h = _combine(h2, ra, rb, gate, mod, norm_final.reshape(1, d) if last else None, ctx=n_ctx_tok)
        return h

    groups = SAMPLE_GROUPS if bsz % (SAMPLE_GROUPS * SCAN_BATCH) == 0 else 1
    gs = bsz // groups
    outs = [forward(x[g * gs:(g + 1) * gs], c[g * gs:(g + 1) * gs], ctx[g * gs:(g + 1) * gs]) for g in range(groups)]
    return outs[0] if groups == 1 else jnp.concatenate(outs, axis=0)
```

```python
import functools

import numpy as np
import jax
import jax.numpy as jnp
from jax import lax
from jax.experimental import pallas as pl
from jax.experimental.pallas import tpu as pltpu
from jax.experimental.pallas import tpu_sc as plsc

F32 = jnp.float32
BF16 = jnp.bfloat16

D_MODEL = 1024
GRID_W = 64
CONV_W = 5
EPS = 1e-6
MIN_LOWER = 1e-30
MASKED_EXPONENT = -1e30
HEAD = 64
SSD_HEADS = 8
SSD_WIDTH = 512
SSD_STATE = 128
SSD_GROUPS = 2
HG_WIDTH = 256
GD_WIDTH = 256
GD_HEADS = 4
N_EXPERTS = 64
TOP_K = 8
EXPERT_DIM = 256
ROUTED_SCALE = 2.5

LANES = 128
SCAN_CHUNK = 128
SAMPLE_GROUPS = 1
SCAN_BATCH = 4
BASE = 16
TOKEN_TILE = 256
EXPERT_ROWS = 1024
GATHER_WINDOW = 128
VMEM_LIMIT = 56 * 1024 * 1024

Z_OFF, X_OFF, BC_OFF = 0, 512, 1024
HQ_OFF, HFF_OFF, HFB_OFF, HI_OFF, HGATE_OFF = 1536, 1792, 2048, 2304, 2560
GQ_OFF, GK_OFF, GV_OFF, GGATE_OFF = 2816, 3072, 3328, 3584
SM_OFF = 3840
NCOLS = 3968
SSD_CONV = 1024
GDN_CONV = 768
SM_DT = (0, 8)
SM_A = (16, 20)
SM_B = (24, 28)


def _permute_w_in(w):
    d = w.shape[0]
    parts = [w[:, 0:1536],
             w[:, 1552:2832],
             w[:, 2832:3600],
             w[:, 3616:3872],
             w[:, 1536:1552],
             w[:, 3600:3616],
             jnp.zeros((d, LANES - 32), w.dtype)]
    return jnp.concatenate(parts, axis=1)


def _dot(a, b):
    return lax.dot_general(a, b, (((1,), (0,)), ((), ())), preferred_element_type=F32)


def _dot_nt(a, b):
    return lax.dot_general(a, b, (((1,), (1,)), ((), ())), preferred_element_type=F32)


def _split(a):
    hi = a.astype(BF16)
    lo = (a - hi.astype(F32)).astype(BF16)
    return hi, lo


def _dot3(a, b):
    ah, al = _split(a)
    bh, bl = _split(b)
    return _dot(ah, bh) + (_dot(ah, bl) + _dot(al, bh))


def _dot3_exact_rhs(a, b_bf16):
    ah, al = _split(a)
    return _dot(ah, b_bf16) + _dot(al, b_bf16)


def _silu(x):
    return x * jax.nn.sigmoid(x)


def _softplus(x):
    return jnp.maximum(x, 0.0) + jnp.log1p(jnp.exp(-jnp.abs(x)))


def _log_sigmoid(x):
    return jnp.minimum(x, 0.0) - jnp.log1p(jnp.exp(-jnp.abs(x)))


def _params(sem):
    return pltpu.CompilerParams(dimension_semantics=sem, vmem_limit_bytes=VMEM_LIMIT)


def _iota(shape, dim):
    return lax.broadcasted_iota(jnp.int32, shape, dim)


def _ones_where(mask):
    return jnp.where(mask, 1.0, 0.0).astype(BF16)


def _shr(x, div):
    return jnp.right_shift(x, int(np.log2(div)))


def _scan_mask(n, rev):
    r, c = _iota((n, n), 0), _iota((n, n), 1)
    return (c >= r) if rev else (c <= r)


def _expand_matrix(lane0, width):
    r, c = _iota((LANES, width), 0), _iota((LANES, width), 1)
    return _ones_where(r == lane0 + _shr(c, HEAD))


def _head_blocks(n):
    r, c = _iota((n, n), 0), _iota((n, n), 1)
    return _shr(r, HEAD) == _shr(c, HEAD)


def _cumsum_rows(x, rev):
    tri = _ones_where(_scan_mask(x.shape[0], rev))
    xh, xl = _split(x)
    return _dot(tri, xh) + _dot(tri, xl)


def _mod_kernel(s_ref, w_ref, b_ref, o_ref):
    s = _silu(s_ref[...])
    o_ref[...] = _dot(s.astype(BF16), w_ref[...].astype(BF16)) + b_ref[...]


def _modulation(cond, w, b):
    rows, d = cond.shape
    n = w.shape[1]
    bn = d
    return pl.pallas_call(
        _mod_kernel,
        grid=(n // bn,),
        in_specs=[pl.BlockSpec((rows, d), lambda j: (0, 0)),
                  pl.BlockSpec((d, bn), lambda j: (0, j)),
                  pl.BlockSpec((1, bn), lambda j: (0, j))],
        out_specs=pl.BlockSpec((rows, bn), lambda j: (0, j)),
        out_shape=jax.ShapeDtypeStruct((rows, n), F32),
        compiler_params=_params(("parallel",)),
        name="modulation",
    )(cond, w, b.reshape(1, n))


def _inproj_kernel(h_ref, sc_ref, sh_ref, nw_ref, w_ref, cws_ref, cbs_ref, cwg_ref, u_ref, pad_ref, *, ctx_tiles):
    tm = h_ref.shape[1]
    t = pl.program_id(1)
    h = h_ref[0]
    a = h * lax.rsqrt(jnp.mean(h * h, axis=-1, keepdims=True) + EPS) * nw_ref[...]
    a = a * (1.0 + sc_ref[0]) + sh_ref[0]
    ab = a.astype(BF16)
    seg = jnp.where(t < ctx_tiles, tm, GRID_W)
    pos = _iota((tm, 1), 0) & (seg - 1)
    half = CONV_W // 2
    masks = {o: (pos + o >= 0) & (pos + o < seg) for o in range(-half, half + 1) if o}

    halo = 8
    pad_ref[0:halo, :] = jnp.zeros((halo, pad_ref.shape[1]), F32)
    pad_ref[halo + tm:2 * halo + tm, :] = jnp.zeros((halo, pad_ref.shape[1]), F32)

    def conv(x, cw_ref, c0, wd, cb_ref):
        pad_ref[halo:halo + tm, 0:wd] = x
        acc = x * cw_ref[half:half + 1, c0:c0 + wd]
        for j in range(CONV_W):
            o = j - half
            if o == 0:
                continue
            shifted = pad_ref[halo + o:halo + o + tm, 0:wd]
            acc = acc + jnp.where(masks[o], shifted, 0.0) * cw_ref[j:j + 1, c0:c0 + wd]
        if cb_ref is not None:
            acc = acc + cb_ref[:, c0:c0 + wd]
        return _silu(acc)

    step = 2 * LANES
    for c0 in range(0, NCOLS, step):
        wd = min(step, NCOLS - c0)
        u = _dot(ab, w_ref[:, c0:c0 + wd])
        if X_OFF <= c0 < X_OFF + SSD_CONV:
            u = conv(u, cws_ref, c0 - X_OFF, wd, cbs_ref)
        elif GQ_OFF <= c0 < GQ_OFF + GDN_CONV:
            u = conv(u, cwg_ref, c0 - GQ_OFF, wd, None)
        u_ref[0, :, c0:c0 + wd] = u


def _mod_row(k, ctx_tiles, rows):
    return lambda b, t: (jnp.where(t < ctx_tiles, rows - 1, b) * 6 + k, 0, 0)


def _inproj(h, mod, norm_w, w_p, conv_s, bias_s, conv_g, *, ctx):
    bsz, t_all, d = h.shape
    tm = TOKEN_TILE
    ctx_tiles = ctx // tm
    rows = mod.shape[0] // 6
    return pl.pallas_call(
        functools.partial(_inproj_kernel, ctx_tiles=ctx_tiles),
        grid=(bsz, t_all // tm),
        in_specs=[pl.BlockSpec((1, tm, d), lambda b, t: (b, t, 0)),
                  pl.BlockSpec((1, 1, d), _mod_row(1, ctx_tiles, rows)),
                  pl.BlockSpec((1, 1, d), _mod_row(0, ctx_tiles, rows)),
                  pl.BlockSpec((1, d), lambda b, t: (0, 0)),
                  pl.BlockSpec((d, NCOLS), lambda b, t: (0, 0)),
                  pl.BlockSpec((CONV_W, SSD_CONV), lambda b, t: (0, 0)),
                  pl.BlockSpec((1, SSD_CONV), lambda b, t: (0, 0)),
                  pl.BlockSpec((CONV_W, GDN_CONV), lambda b, t: (0, 0))],
        out_specs=pl.BlockSpec((1, tm, NCOLS), lambda b, t: (b, t, 0)),
        out_shape=jax.ShapeDtypeStruct((bsz, t_all, NCOLS), F32),
        scratch_shapes=[pltpu.VMEM((tm + 16, 2 * LANES), F32)],
        compiler_params=_params(("parallel", "parallel")),
        name="in_projection",
    )(h, mod, mod, norm_w.reshape(1, d), w_p, conv_s, bias_s.reshape(1, -1), conv_g)


def _chunk_index(rev, n_ctx, n_all):
    if not rev:
        return lambda c: c
    return lambda c: jnp.where(c < n_ctx, n_ctx - 1 - c, n_all + n_ctx - 1 - c)


def _scan_call(body, u, col_blocks, extra, out_width, scratch, *, rev, n_ctx, name):
    bsz, t_all, _ = u.shape
    cn, bb = SCAN_CHUNK, SCAN_BATCH
    assert bsz % bb == 0
    n_all = t_all // cn
    cidx = _chunk_index(rev, n_ctx, n_all)

    def tok(width, off):
        return pl.BlockSpec((bb, cn, width), lambda b, c: (b, cidx(c), off // width))

    in_specs = [tok(w, off) for w, off in col_blocks]
    in_specs += [pl.BlockSpec(a.shape, lambda b, c: (0, 0)) for a in extra]
    return pl.pallas_call(
        body,
        grid=(bsz // bb, n_all),
        in_specs=in_specs,
        out_specs=tok(out_width, 0),
        out_shape=jax.ShapeDtypeStruct((bsz, t_all, out_width), F32),
        scratch_shapes=[scratch],
        compiler_params=_params(("parallel", "arbitrary")),
        name=name,
    )(*([u] * len(col_blocks)), *extra)


def _reset_state(st_ref):
    @pl.when(pl.program_id(1) == 0)
    def _():
        st_ref[...] = jnp.zeros_like(st_ref)


def _ssd_kernel(x_ref, bc_ref, sm_ref, bias_ref, arow_ref, drow_ref, y_ref, st_ref, *, rev):
    bb, cn = x_ref.shape[0], x_ref.shape[1]
    _reset_state(st_ref)
    lane0 = SM_DT[1] if rev else SM_DT[0]
    mask = _scan_mask(cn, rev)
    ex = _expand_matrix(lane0, SSD_WIDTH)
    gw = SSD_WIDTH // SSD_GROUPS
    hpg = SSD_HEADS // SSD_GROUPS
    lane_head = _shr(_iota((1, gw), 1), HEAD)
    last = 0 if rev else cn - 1
    for i in range(bb):
        xs = x_ref[i]
        bc = bc_ref[i]
        dt = _softplus(sm_ref[i] + bias_ref[...])
        da = dt * arow_ref[...]
        acs = _cumsum_rows(da, rev)
        acs_r = acs.T
        acs_e = _dot3_exact_rhs(acs, ex)
        dt_e = _dot3_exact_rhs(dt, ex)
        tot_e = acs_e[last:last + 1]
        xdt = xs * dt_e
        xdt_b = xdt.astype(BF16)
        wst = (xdt * jnp.exp(tot_e - acs_e)).astype(BF16)
        eacs = jnp.exp(acs_e)
        etot = jnp.exp(tot_e)
        for g in range(SSD_GROUPS):
            bm = bc[:, g * SSD_STATE:(g + 1) * SSD_STATE]
            cm = bc[:, (SSD_GROUPS + g) * SSD_STATE:(SSD_GROUPS + g + 1) * SSD_STATE]
            bmb, cmb = bm.astype(BF16), cm.astype(BF16)
            cb = _dot_nt(cmb, bmb)
            xg = xdt_b[:, g * gw:(g + 1) * gw]
            yd = jnp.zeros((cn, gw), F32)
            for r in range(hpg):
                ln = lane0 + g * hpg + r
                diff = acs[:, ln:ln + 1] - acs_r[ln:ln + 1, :]
                dec = jnp.where(mask, jnp.exp(jnp.where(mask, diff, 0.0)), 0.0)
                yh = _dot((cb * dec).astype(BF16), xg)
                yd = jnp.where(lane_head == r, yh, yd)
            st = st_ref[i * SSD_GROUPS + g]
            yo = _dot(cmb, st.astype(BF16)) * eacs[:, g * gw:(g + 1) * gw]
            y = yd + yo
            if drow_ref is not None:
                y = y + drow_ref[:, g * gw:(g + 1) * gw] * xs[:, g * gw:(g + 1) * gw]
            y_ref[i, :, g * gw:(g + 1) * gw] = y
            st_ref[i * SSD_GROUPS + g] = (st * etot[:, g * gw:(g + 1) * gw]
                                          + _dot(bm.T.astype(BF16), wst[:, g * gw:(g + 1) * gw]))


def _ssd_kernel_no_skip(x_ref, bc_ref, sm_ref, bias_ref, arow_ref, y_ref, st_ref, *, rev):
    _ssd_kernel(x_ref, bc_ref, sm_ref, bias_ref, arow_ref, None, y_ref, st_ref, rev=rev)


def _ssd_scan(u, dt_bias, a_log, d_skip, *, rev, n_ctx):
    di = 1 if rev else 0
    lane0 = SM_DT[di]
    bias = jnp.zeros((1, LANES), F32).at[0, lane0:lane0 + SSD_HEADS].set(dt_bias[di])
    arow = jnp.zeros((1, LANES), F32).at[0, lane0:lane0 + SSD_HEADS].set(-jnp.exp(a_log[di].astype(F32)))
    extra = [bias, arow]
    if rev:
        body = functools.partial(_ssd_kernel_no_skip, rev=rev)
    else:
        extra.append(jnp.repeat(d_skip.astype(F32), HEAD).reshape(1, SSD_WIDTH))
        body = functools.partial(_ssd_kernel, rev=rev)
    scratch = pltpu.VMEM((SCAN_BATCH * SSD_GROUPS, SSD_STATE, SSD_WIDTH // SSD_GROUPS), F32)
    return _scan_call(body, u, [(SSD_WIDTH, X_OFF), (512, BC_OFF), (LANES, SM_OFF)], extra, SSD_WIDTH, scratch,
                      rev=rev, n_ctx=n_ctx, name="ssd_scan_bwd" if rev else "ssd_scan_fwd")


def _hgrn_kernel(q_ref, f_ref, i_ref, low_ref, y_ref, st_ref, *, rev):
    bb, cn = q_ref.shape[0], q_ref.shape[1]
    wdt = HG_WIDTH
    heads = wdt // HEAD
    _reset_state(st_ref)
    lower = low_ref[0:1]
    log_lower = low_ref[1:2]
    hb = _head_blocks(wdt)
    bd = _ones_where(hb)
    lane_head = _shr(_iota((1, wdt), 1), HEAD)
    last = 0 if rev else cn - 1
    si, ti = _iota((BASE, BASE, wdt), 0), _iota((BASE, BASE, wdt), 1)
    pair_mask = (si >= ti) if rev else (si <= ti)

    for i in range(bb):
        fr = f_ref[i]
        qs = _silu(q_ref[i])
        v = i_ref[i]
        c = log_lower - fr
        tail = lambda z: jnp.log(1.0 + jnp.exp(-jnp.abs(z)))
        logf = (jnp.minimum(fr, 0.0) - tail(fr)) + (jnp.maximum(c, 0.0) + tail(c))
        kg = (1.0 - lower) * jax.nn.sigmoid(-fr)
        b = _cumsum_rows(logf, rev)
        vb = v.astype(BF16)

        st = st_ref[i]
        y_ref[i] = _dot_nt((qs * jnp.exp(b)).astype(BF16), st.astype(BF16))
        b_last = b[last:last + 1]
        kend = (kg * jnp.exp(b_last - b)).astype(BF16)
        st_ref[i] = st * jnp.exp(b_last) + jnp.where(hb, _dot(v.T.astype(BF16), kend), 0.0)

        def offdiag(t0, t1, s0, s1, r):
            br = b[r:r + 1]
            qp = qs[t0:t1] * jnp.exp(b[t0:t1] - br)
            kp = (kg[s0:s1] * jnp.exp(br - b[s0:s1])).astype(BF16)
            nt = t1 - t0
            qstack = jnp.concatenate([jnp.where(lane_head == h, qp, 0.0) for h in range(heads)], axis=0)
            att = _dot_nt(qstack.astype(BF16), kp)
            res = _dot(att.astype(BF16), vb[s0:s1])
            out = jnp.zeros((nt, wdt), F32)
            for h in range(heads):
                out = jnp.where(lane_head == h, res[h * nt:(h + 1) * nt], out)
            y_ref[i, t0:t1, :] += out

        def diag(t0, t1):
            n = t1 - t0
            bt = b[t0:t1]
            m3 = pair_mask
            diff = bt[None, :, :] - bt[:, None, :]
            w = jnp.exp(jnp.where(m3, diff, MASKED_EXPONENT))
            p = w * qs[t0:t1][None, :, :] * kg[t0:t1][:, None, :]
            r2 = _dot(p.reshape(n * n, wdt).astype(BF16), bd)
            y_ref[i, t0:t1, :] += jnp.sum(r2.reshape(n, n, wdt) * v[t0:t1][:, None, :], axis=0)

        def block(lo, hi):
            if hi - lo <= BASE:
                diag(lo, hi)
                return
            mid = (lo + hi) // 2
            if rev:
                offdiag(lo, mid, mid, hi, mid)
            else:
                offdiag(mid, hi, lo, mid, mid - 1)
            block(lo, mid)
            block(mid, hi)

        block(0, cn)


def _hgrn_scan(u, lower, *, rev, n_ctx):
    f_off = HFB_OFF if rev else HFF_OFF
    low = jnp.stack([lower, jnp.log(jnp.maximum(lower, MIN_LOWER))]).astype(F32)
    low = jnp.concatenate([low, jnp.zeros((6, HG_WIDTH), F32)], axis=0)
    scratch = pltpu.VMEM((SCAN_BATCH, HG_WIDTH, HG_WIDTH), F32)
    return _scan_call(functools.partial(_hgrn_kernel, rev=rev), u,
                      [(HG_WIDTH, HQ_OFF), (HG_WIDTH, f_off), (HG_WIDTH, HI_OFF)], [low], HG_WIDTH, scratch,
                      rev=rev, n_ctx=n_ctx, name="hgrn_scan_bwd" if rev else "hgrn_scan_fwd")


def _bdot(a, b):
    return lax.dot_general(a, b, (((2,), (1,)), ((0,), (0,))), preferred_element_type=F32)


def _unit_inverse_delta(a):
    n_rows = a.shape[-1]
    r, c = _iota((n_rows, n_rows), 0), _iota((n_rows, n_rows), 1)
    d = jnp.where((_shr(r, BASE) == _shr(c, BASE))[None], a, 0.0)
    db = d.astype(BF16)
    p = _bdot(db, db)
    n = p - d - _bdot(db, p.astype(BF16))
    e = 4
    while e < BASE:
        pb = p.astype(BF16)
        p = _bdot(pb, pb)
        n = n + p + _bdot(n.astype(BF16), p.astype(BF16))
        e *= 2
    size = BASE
    while size < n_rows:
        big = 2 * size
        off = (_shr(r, big) == _shr(c, big)) & (_shr(r, size) != _shr(c, size))
        a_off = jnp.where(off[None], a, 0.0)
        nb = n.astype(BF16)
        m = a_off + _bdot(nb, a_off.astype(BF16))
        n = n - (m + _bdot(m.astype(BF16), nb))
        size = big
    return n


def _gdn_kernel(q_ref, k_ref, v_ref, sm_ref, bias_ref, arow_ref, y_ref, st_ref, *, rev):
    bb, cn = q_ref.shape[0], q_ref.shape[1]
    wdt = GD_WIDTH
    _reset_state(st_ref)
    hb = _head_blocks(wdt)
    bd = _ones_where(hb)
    lane_head = _shr(_iota((1, wdt), 1), HEAD)
    la = SM_A[1] if rev else SM_A[0]
    lb = SM_B[1] if rev else SM_B[0]
    ex_a, ex_b = _expand_matrix(la, wdt), _expand_matrix(lb, wdt)
    mask = _scan_mask(cn, rev)
    strict = mask & (_iota((cn, cn), 0) != _iota((cn, cn), 1))
    last = 0 if rev else cn - 1
    pre, a_all = [], []
    for i in range(bb):
        q, k, v = q_ref[i], k_ref[i], v_ref[i]
        q = q * lax.rsqrt(_dot3_exact_rhs(q * q, bd) + 1e-6) * (HEAD ** -0.5)
        k = k * lax.rsqrt(_dot3_exact_rhs(k * k, bd) + 1e-6)
        sm = sm_ref[i]
        g = arow_ref[...] * _softplus(sm + bias_ref[...])
        beta = jax.nn.sigmoid(sm)
        gam = _cumsum_rows(g, rev)
        gam_r = gam.T
        gam_e = _dot3_exact_rhs(gam, ex_a)
        beta_e = _dot3_exact_rhs(beta, ex_b)
        tot_e = gam_e[last:last + 1]
        egam = jnp.exp(gam_e)
        kb = k * beta_e
        kbf = k.astype(BF16)
        rhs = jnp.concatenate([v * beta_e, kb * egam], axis=1)
        qks = []
        for h in range(GD_HEADS):
            ln = la + h
            diff = gam[:, ln:ln + 1] - gam_r[ln:ln + 1, :]
            dec = jnp.where(mask, jnp.exp(jnp.where(mask, diff, 0.0)), 0.0)
            hm = lane_head == h
            kk = _dot_nt(jnp.where(hm, kb, 0.0).astype(BF16), kbf)
            a_all.append(jnp.where(strict, kk * dec, 0.0))
            qks.append((_dot_nt(jnp.where(hm, q, 0.0).astype(BF16), kbf) * dec).astype(BF16))
        pre.append((q, k, rhs, qks, egam, gam_e, tot_e))
    n_all = _unit_inverse_delta(jnp.stack(a_all)).astype(BF16)
    for i in range(bb):
        q, k, rhs, qks, egam, gam_e, tot_e = pre[i]
        rhs_b = rhs.astype(BF16)
        u_all = jnp.zeros((cn, wdt), F32)
        w_all = jnp.zeros((cn, wdt), F32)
        for h in range(GD_HEADS):
            sol = rhs + _dot(n_all[i * GD_HEADS + h], rhs_b)
            hm = lane_head == h
            u_all = jnp.where(hm, sol[:, :wdt], u_all)
            w_all = jnp.where(hm, sol[:, wdt:], w_all)
        st = st_ref[i]
        stb = st.astype(BF16)
        v_new = u_all - _dot(w_all.astype(BF16), stb)
        vnb = v_new.astype(BF16)
        o = _dot((q * egam).astype(BF16), stb)
        for h in range(GD_HEADS):
            o = o + jnp.where(lane_head == h, _dot(qks[h], vnb), 0.0)
        y_ref[i] = o
        kend = k * jnp.exp(tot_e - gam_e)
        st_ref[i] = st * jnp.exp(tot_e) + jnp.where(hb, _dot(kend.T.astype(BF16), vnb), 0.0)


def _gdn_scan(u, dt_bias, a_log, *, rev, n_ctx):
    di = 1 if rev else 0
    la = SM_A[di]
    bias = jnp.zeros((1, LANES), F32).at[0, la:la + GD_HEADS].set(dt_bias[di])
    arow = jnp.zeros((1, LANES), F32).at[0, la:la + GD_HEADS].set(-jnp.exp(a_log[di].astype(F32)))
    scratch = pltpu.VMEM((SCAN_BATCH, GD_WIDTH, GD_WIDTH), F32)
    return _scan_call(functools.partial(_gdn_kernel, rev=rev), u,
                      [(GD_WIDTH, GQ_OFF), (GD_WIDTH, GK_OFF), (GD_WIDTH, GV_OFF), (LANES, SM_OFF)], [bias, arow],
                      GD_WIDTH, scratch, rev=rev, n_ctx=n_ctx, name="gdn_scan_bwd" if rev else "gdn_scan_fwd")


def _pack_rows(x):
    half = x.shape[1] // 2
    q = half // 2
    xr = x.astype(BF16).astype(F32)
    lo = lax.bitcast_convert_type(xr[:, :half], jnp.uint32)
    hi = lax.bitcast_convert_type(xr[:, half:], jnp.uint32)
    word = (lo >> 16) | (hi & jnp.uint32(0xFFFF0000))
    return word[:, :q], word[:, q:]


def _unpack_rows(wa, wb):
    mask = jnp.uint32(0xFFFF0000)
    f = lambda w: lax.bitcast_convert_type(w, F32)
    return f(wa << 16), f(wb << 16), f(wa & mask), f(wb & mask)


def _group_rms(y, group, bd):
    ms = _dot3_exact_rhs(y * y, bd) * (1.0 / group)
    return y * lax.rsqrt(ms + EPS)


def _outproj_kernel(ysf, ysb, yhf, yhb, ygf, ygb, z_ref, hg_ref, gg_ref, h_ref, gm_ref, scf_ref, shf_ref, gf_ref,
                    nrm_ref, wout_ref, nf_ref, rw_ref, rb_ref, sg_ref, su_ref, sd_ref, bds_ref, bdh_ref,
                    h2_ref, xpa_ref, xpb_ref, idx_ref, gate_ref):
    tm = h_ref.shape[1]
    gw = SSD_WIDTH // SSD_GROUPS
    bd_s = bds_ref[...]
    bd_h = bdh_ref[...]
    ys = (ysf[0] + ysb[0]) * _silu(z_ref[0])
    ys = _group_rms(ys, gw, bd_s) * nrm_ref[:, 0:SSD_WIDTH]
    yh = _group_rms(yhf[0] + yhb[0], HEAD, bd_h) * nrm_ref[:, SSD_WIDTH:SSD_WIDTH + HG_WIDTH] * _silu(hg_ref[0])
    yg = _group_rms(ygf[0] + ygb[0], HEAD, bd_h) * nrm_ref[:, SSD_WIDTH + HG_WIDTH:] * _silu(gg_ref[0])
    y = jnp.concatenate([ys, yh, yg], axis=1).astype(BF16)
    h1 = h_ref[0] + gm_ref[0] * _dot(y, wout_ref[...])
    xf = h1 * lax.rsqrt(jnp.mean(h1 * h1, axis=-1, keepdims=True) + EPS) * nf_ref[...]
    xf = xf * (1.0 + scf_ref[0]) + shf_ref[0]
    xb = xf.astype(BF16)
    xpa_ref[0], xpb_ref[0] = _pack_rows(xf)
    hid = _silu(_dot(xb, sg_ref[...])) * _dot(xb, su_ref[...])
    h2_ref[0] = h1 + gf_ref[0] * _dot(hid.astype(BF16), sd_ref[...])
    scores = jax.nn.sigmoid(_dot3(xf, rw_ref[...]))
    lane = _iota((tm, LANES), 1)
    sel = jnp.where(lane < N_EXPERTS, scores + rb_ref[...], -jnp.inf)
    rank_code = (LANES - lane).astype(F32)
    idx_f = jnp.zeros((tm, LANES), F32)
    gate_out = jnp.zeros((tm, LANES), F32)
    gsum = jnp.zeros((tm, 1), F32)
    for j in range(TOP_K):
        m = jnp.max(sel, axis=-1, keepdims=True)
        code = jnp.max(jnp.where(sel == m, rank_code, 0.0), axis=-1, keepdims=True)
        hit = rank_code == code
        gv = jnp.sum(jnp.where(hit, scores, 0.0), axis=-1, keepdims=True)
        idx_f = jnp.where(lane == j, LANES - code, idx_f)
        gate_out = jnp.where(lane == j, gv, gate_out)
        gsum = gsum + gv
        sel = jnp.where(hit, -jnp.inf, sel)
    idx_ref[0] = idx_f.astype(jnp.int32)
    gate_ref[0] = gate_out / gsum * ROUTED_SCALE


def _outproj(ys, yh, yg, u, h, mod, nrm, w_out, norm_ffn, router_wp, router_bp, sg, su, sd, *, ctx):
    bsz, t_all, d = h.shape
    tm = TOKEN_TILE
    ctx_tiles = ctx // tm
    rows = mod.shape[0] // 6
    mrow = lambda k: pl.BlockSpec((1, 1, d), _mod_row(k, ctx_tiles, rows))
    tok = lambda w, j=0: pl.BlockSpec((1, tm, w), lambda b, t: (b, t, j))
    full = lambda a: pl.BlockSpec(a.shape, lambda b, t: (0,) * a.ndim)
    in_specs = [tok(SSD_WIDTH), tok(SSD_WIDTH), tok(HG_WIDTH), tok(HG_WIDTH), tok(GD_WIDTH), tok(GD_WIDTH),
                tok(SSD_WIDTH, Z_OFF // SSD_WIDTH), tok(HG_WIDTH, HGATE_OFF // HG_WIDTH),
                tok(GD_WIDTH, GGATE_OFF // GD_WIDTH),
                tok(d), mrow(2), mrow(4), mrow(3), mrow(5),
                full(nrm), full(w_out), full(norm_ffn), full(router_wp), full(router_bp), full(sg), full(su), full(sd)]
    group_matrix = lambda n, g: (jnp.arange(n)[:, None] // g == jnp.arange(n)[None, :] // g).astype(BF16)
    bd_s = group_matrix(SSD_WIDTH, SSD_WIDTH // SSD_GROUPS)
    bd_h = group_matrix(HG_WIDTH, HEAD)
    in_specs += [full(bd_s), full(bd_h)]
    pw = d // 4
    out_shape = (jax.ShapeDtypeStruct((bsz, t_all, d), F32),
                 jax.ShapeDtypeStruct((bsz, t_all, pw), jnp.uint32), jax.ShapeDtypeStruct((bsz, t_all, pw), jnp.uint32),
                 jax.ShapeDtypeStruct((bsz, t_all, LANES), jnp.int32), jax.ShapeDtypeStruct((bsz, t_all, LANES), F32))
    out_specs = (tok(d), tok(pw), tok(pw), tok(LANES), tok(LANES))
    return pl.pallas_call(
        _outproj_kernel,
        grid=(bsz, t_all // tm),
        in_specs=in_specs,
        out_specs=out_specs,
        out_shape=out_shape,
        compiler_params=_params(("parallel", "parallel")),
        name="out_projection_router",
    )(ys[0], ys[1], yh[0], yh[1], yg[0], yg[1], u, u, u, h, mod, mod, mod, mod,
      nrm, w_out, norm_ffn, router_wp, router_bp, sg, su, sd, bd_s, bd_h)


def _rank_kernel(idx_ref, rank_ref, exp_ref, cnt_ref, base_ref):
    tm = idx_ref.shape[0]

    @pl.when(pl.program_id(0) == 0)
    def _():
        base_ref[...] = jnp.zeros_like(base_ref)

    idx = idx_ref[...]
    lane = _iota((tm, LANES), 1)
    hits = [lane == idx[:, j:j + 1] for j in range(TOP_K)]
    m = jnp.zeros((tm, LANES), F32)
    for hit in hits:
        m = m + jnp.where(hit, 1.0, 0.0)
    before = _ones_where(_iota((tm, tm), 1) < _iota((tm, tm), 0))
    base = base_ref[...]
    val = _dot(before, m.astype(BF16)) + base
    out = jnp.zeros((tm, LANES), F32)
    for j, hit in enumerate(hits):
        rj = jnp.sum(jnp.where(hit, val, 0.0), axis=-1, keepdims=True)
        out = jnp.where(lane == j, rj, out)
    rank_ref[...] = out.T[0:TOP_K].astype(jnp.int32)
    exp_ref[...] = idx.astype(F32).T[0:TOP_K].astype(jnp.int32)
    total = base + jnp.sum(m, axis=0, keepdims=True)
    base_ref[...] = total
    cnt_ref[...] = total


def _route_ranks(idx2d):
    n_tok = idx2d.shape[0]
    tm = TOKEN_TILE
    per_choice = lambda: pl.BlockSpec((TOP_K, tm), lambda i: (0, i))
    return pl.pallas_call(
        _rank_kernel,
        grid=(n_tok // tm,),
        in_specs=[pl.BlockSpec((tm, LANES), lambda i: (i, 0))],
        out_specs=(per_choice(), per_choice(), pl.BlockSpec((1, LANES), lambda i: (0, 0))),
        out_shape=(jax.ShapeDtypeStruct((TOP_K, n_tok), jnp.int32), jax.ShapeDtypeStruct((TOP_K, n_tok), jnp.int32),
                   jax.ShapeDtypeStruct((1, LANES), F32)),
        scratch_shapes=[pltpu.VMEM((1, LANES), F32)],
        compiler_params=_params(("arbitrary",)),
        name="route_ranks",
    )(idx2d)


def _expert_kernel(be_ref, xa_ref, xb_ref, wg_ref, wu_ref, wd_ref, oa_ref, ob_ref, wgb_ref, wub_ref, wdb_ref):
    q = xa_ref.shape[1]
    i = pl.program_id(0)
    n_live = be_ref[be_ref.shape[0] - 1]

    @pl.when((i == 0) | (be_ref[i] != be_ref[jnp.maximum(i - 1, 0)]))
    def _():
        wgb_ref[...] = wg_ref[0].astype(BF16)
        wub_ref[...] = wu_ref[0].astype(BF16)
        wdb_ref[...] = wd_ref[0].astype(BF16)

    @pl.when(i < n_live)
    def _():
        parts = [p.astype(BF16) for p in _unpack_rows(xa_ref[...], xb_ref[...])]

        def proj(w_ref):
            acc = _dot(parts[0], w_ref[0:q, :])
            for j in range(1, 4):
                acc = acc + _dot(parts[j], w_ref[j * q:(j + 1) * q, :])
            return acc

        hid = _silu(proj(wgb_ref)) * proj(wub_ref)
        oa_ref[...], ob_ref[...] = _pack_rows(_dot(hid.astype(BF16), wdb_ref[...]))

    @pl.when(i >= n_live)
    def _():
        oa_ref[...] = jnp.zeros_like(oa_ref)
        ob_ref[...] = jnp.zeros_like(ob_ref)


def _expert_blocks(xsa, xsb, block_expert, w_gate, w_up, w_down):
    n_rows, pw = xsa.shape
    bm = EXPERT_ROWS
    d, e_dim = w_gate.shape[1], w_gate.shape[2]
    assert block_expert.shape[0] == n_rows // bm + 1
    row = lambda: pl.BlockSpec((bm, pw), lambda i, be: (i, 0))
    grid_spec = pltpu.PrefetchScalarGridSpec(
        num_scalar_prefetch=1,
        grid=(n_rows // bm,),
        in_specs=[row(), row(),
                  pl.BlockSpec((1, d, e_dim), lambda i, be: (be[i], 0, 0)),
                  pl.BlockSpec((1, d, e_dim), lambda i, be: (be[i], 0, 0)),
                  pl.BlockSpec((1, e_dim, d), lambda i, be: (be[i], 0, 0))],
        out_specs=(row(), row()),
        scratch_shapes=[pltpu.VMEM((d, e_dim), BF16), pltpu.VMEM((d, e_dim), BF16), pltpu.VMEM((e_dim, d), BF16)],
    )
    return pl.pallas_call(
        _expert_kernel,
        grid_spec=grid_spec,
        out_shape=(jax.ShapeDtypeStruct((n_rows, pw), jnp.uint32), jax.ShapeDtypeStruct((n_rows, pw), jnp.uint32)),
        compiler_params=_params(("arbitrary",)),
        name="routed_experts",
    )(block_expert, xsa, xsb, w_gate, w_up, w_down)


def _combine_kernel(h_ref, ra_ref, rb_ref, gate_ref, gf_ref, *rest):
    nw_ref, o_ref = (rest[0], rest[1]) if len(rest) == 2 else (None, rest[0])
    tm, q = ra_ref.shape[1], ra_ref.shape[2]
    gate = gate_ref[0]
    acc = [jnp.zeros((tm, q), F32) for _ in range(4)]
    for j in range(TOP_K):
        gj = gate[:, j:j + 1]
        for blk, part in enumerate(_unpack_rows(ra_ref[j], rb_ref[j])):
            acc[blk] = acc[blk] + part * gj
    out = [h_ref[0, :, blk * q:(blk + 1) * q] + gf_ref[0, :, blk * q:(blk + 1) * q] * acc[blk] for blk in range(4)]
    if nw_ref is not None:
        ms = sum(jnp.sum(o * o, axis=-1, keepdims=True) for o in out) * (1.0 / (4 * q))
        scale = lax.rsqrt(ms + EPS)
        out = [o * scale * nw_ref[:, blk * q:(blk + 1) * q] for blk, o in enumerate(out)]
    for blk in range(4):
        o_ref[0, :, blk * q:(blk + 1) * q] = out[blk]


def _combine(h2, ra, rb, gate, mod, final_w, *, ctx):
    bsz, t_all, d = h2.shape
    tm = TOKEN_TILE
    tiles = t_all // tm
    ctx_tiles = ctx // tm
    rows = mod.shape[0] // 6
    pw = ra.shape[-1]
    skip = ctx_tiles if final_w is not None else 0
    per = lambda: pl.BlockSpec((TOP_K, tm, pw), lambda b, t: (0, b * tiles + t + skip, 0))
    in_specs = [pl.BlockSpec((1, tm, d), lambda b, t: (b, t + skip, 0)), per(), per(),
                pl.BlockSpec((1, tm, LANES), lambda b, t: (b, t + skip, 0)),
                pl.BlockSpec((1, 1, d), lambda b, t: _mod_row(5, ctx_tiles, rows)(b, t + skip))]
    args = [h2, ra, rb, gate, mod]
    if final_w is not None:
        in_specs.append(pl.BlockSpec((1, d), lambda b, t: (0, 0)))
        args.append(final_w)
    return pl.pallas_call(
        _combine_kernel,
        grid=(bsz, tiles - skip),
        in_specs=in_specs,
        out_specs=pl.BlockSpec((1, tm, d), lambda b, t: (b, t, 0)),
        out_shape=jax.ShapeDtypeStruct((bsz, t_all - skip * tm, d), F32),
        compiler_params=_params(("parallel", "parallel")),
        name="moe_combine_final" if final_w is not None else "moe_combine",
    )(*args)


def _gather_rows(xa, xb, indices):
    n = indices.shape[0]
    q = xa.shape[1]
    win = GATHER_WINDOW
    mesh = plsc.VectorSubcoreMesh(core_axis_name="core", subcore_axis_name="subcore")
    workers = mesh.num_cores * mesh.num_subcores
    per = n // (win * workers)
    assert per * win * workers == n, (n, win, workers)
    out = jax.ShapeDtypeStruct((n, q), xa.dtype)
    scratch = [pltpu.VMEM((per, win), jnp.int32), pltpu.VMEM((win, q), xa.dtype), pltpu.VMEM((win, q), xa.dtype),
               pltpu.SemaphoreType.DMA((4,))]

    @functools.partial(pl.kernel, out_type=(out, out), mesh=mesh, scratch_types=scratch)
    def gather(xa_hbm, xb_hbm, i_hbm, oa_hbm, ob_hbm, idx_vmem, buf_a, buf_b, sems):
        wid = lax.axis_index("core") * mesh.num_subcores + lax.axis_index("subcore")
        pltpu.sync_copy(i_hbm.at[wid], idx_vmem)

        @pl.loop(0, per)
        def _(s):
            rows = pl.ds(pl.multiple_of((wid * per + s) * win, win), win)
            ga = pltpu.async_copy(xa_hbm.at[idx_vmem.at[s]], buf_a, sems.at[0])
            gb = pltpu.async_copy(xb_hbm.at[idx_vmem.at[s]], buf_b, sems.at[1])
            ga.wait()
            wa = pltpu.async_copy(buf_a, oa_hbm.at[rows], sems.at[2])
            gb.wait()
            wb = pltpu.async_copy(buf_b, ob_hbm.at[rows], sems.at[3])
            wa.wait()
            wb.wait()

    return gather(xa, xb, indices.reshape(workers, per, win))


def _scatter_rows(xa, xb, indices, n_out):
    n = indices.shape[0]
    n_src, q = xa.shape
    win = GATHER_WINDOW
    mesh = plsc.VectorSubcoreMesh(core_axis_name="core", subcore_axis_name="subcore")
    workers = mesh.num_cores * mesh.num_subcores
    per = n // (win * workers)
    assert per * win * workers == n and n_src % win == 0, (n, n_src, win, workers)
    src_windows = n_src // win
    out = jax.ShapeDtypeStruct((n_out, q), xa.dtype)
    scratch = [pltpu.VMEM((per, win), jnp.int32), pltpu.VMEM((win, q), xa.dtype), pltpu.VMEM((win, q), xa.dtype),
               pltpu.SemaphoreType.DMA((4,))]

    @functools.partial(pl.kernel, out_type=(out, out), mesh=mesh, scratch_types=scratch)
    def scatter(xa_hbm, xb_hbm, i_hbm, oa_hbm, ob_hbm, idx_vmem, buf_a, buf_b, sems):
        wid = lax.axis_index("core") * mesh.num_subcores + lax.axis_index("subcore")
        pltpu.sync_copy(i_hbm.at[wid], idx_vmem)

        @pl.loop(0, per)
        def _(s):
            src = lax.rem(wid * per + s, src_windows)
            rows = pl.ds(pl.multiple_of(src * win, win), win)
            la = pltpu.async_copy(xa_hbm.at[rows], buf_a, sems.at[0])
            lb = pltpu.async_copy(xb_hbm.at[rows], buf_b, sems.at[1])
            la.wait()
            sa = pltpu.async_copy(buf_a, oa_hbm.at[idx_vmem.at[s]], sems.at[2])
            lb.wait()
            sb = pltpu.async_copy(buf_b, ob_hbm.at[idx_vmem.at[s]], sems.at[3])
            sa.wait()
            sb.wait()

    return scatter(xa, xb, indices.reshape(workers, per, win))


def _routed(xpa, xpb, idx, w_gate, w_up, w_down, layer):
    bsz, t_all, pw = xpa.shape
    n_tok = bsz * t_all
    n_assign = n_tok * TOP_K
    bm = EXPERT_ROWS
    n_blocks = -(-n_assign // bm) + N_EXPERTS
    n_slots = n_blocks * bm
    idx2d = idx.reshape(n_tok, LANES)
    rank_t, exp_t, cnt = _route_ranks(idx2d)
    counts = cnt[0, :N_EXPERTS].astype(jnp.int32)
    padded = (counts + bm - 1) // bm * bm
    pad_end = jnp.cumsum(padded)
    offset = pad_end - padded
    slot_t = jnp.take(offset, exp_t.reshape(-1)) + rank_t.reshape(-1)
    block_expert = jnp.minimum(
        jnp.sum(pad_end[None, :] <= (jnp.arange(n_blocks, dtype=jnp.int32) * bm)[:, None], axis=1),
        N_EXPERTS - 1).astype(jnp.int32)
    xsa, xsb = _scatter_rows(xpa.reshape(n_tok, pw), xpb.reshape(n_tok, pw), slot_t, n_slots)
    stack = lambda w: w.reshape((-1,) + w.shape[2:])
    table = jnp.concatenate([block_expert + layer * N_EXPERTS, (pad_end[-1:] // bm).astype(jnp.int32)])
    oa, ob = _expert_blocks(xsa, xsb, table, stack(w_gate), stack(w_up), stack(w_down))
    ra, rb = _gather_rows(oa, ob, slot_t)
    return ra.reshape(TOP_K, n_tok, pw), rb.reshape(TOP_K, n_tok, pw)


def kernel(x, c, ctx, c_ctx, w_mod, b_mod, norm_mix, w_in, ssd_conv_w, ssd_conv_b, ssd_dt_bias, ssd_a_log, ssd_d,
           ssd_norm, hgrn_lb, hgrn_norm, gdn_conv_w, gdn_dt_bias, gdn_a_log, gdn_norm, w_out, norm_ffn, router_w,
           router_bias, exp_gate, exp_up, exp_down, sh_gate, sh_up, sh_down, norm_final):
    bsz, seq, d = x.shape
    n_ctx_tok = ctx.shape[1]
    depth = w_in.shape[0]
    assert d == D_MODEL and n_ctx_tok == TOKEN_TILE and seq % TOKEN_TILE == 0 and TOKEN_TILE % SCAN_CHUNK == 0
    assert TOKEN_TILE % GRID_W == 0 and seq % GRID_W == 0
    n_ctx = n_ctx_tok // SCAN_CHUNK

    p_lb = jax.nn.softmax(hgrn_lb.astype(F32), axis=0)
    lower_all = jnp.cumsum(p_lb, axis=0) - p_lb[0]

    layers = []
    for l in range(depth):
        nrm = jnp.concatenate([ssd_norm[l], jnp.tile(hgrn_norm[l], HG_WIDTH // HEAD),
                               jnp.tile(gdn_norm[l], GD_WIDTH // HEAD)]).astype(F32).reshape(1, d)
        layers.append(dict(
            w_p=_permute_w_in(w_in[l]).astype(BF16), nrm=nrm,
            rwp=jnp.pad(router_w[l].astype(F32), ((0, 0), (0, LANES - N_EXPERTS))),
            rbp=jnp.pad(router_bias[l].astype(F32), (0, LANES - N_EXPERTS)).reshape(1, LANES),
            w_out=w_out[l].astype(BF16), sg=sh_gate[l].astype(BF16), su=sh_up[l].astype(BF16),
            sd=sh_down[l].astype(BF16)))

    def forward(xg, cg, ctxg):
        nb = xg.shape[0]
        rows = -(-(nb + 1) // 8) * 8
        cond = jnp.zeros((rows, d), F32).at[:nb].set(cg).at[rows - 1].set(c_ctx)
        h = jnp.concatenate([ctxg, xg], axis=1)
        for l in range(depth):
            p = layers[l]
            mod = _modulation(cond, w_mod[l], b_mod[l]).reshape(rows * 6, 1, d)
            u = _inproj(h, mod, norm_mix[l], p["w_p"], ssd_conv_w[l], ssd_conv_b[l], gdn_conv_w[l], ctx=n_ctx_tok)
            ys = [_ssd_scan(u, ssd_dt_bias[l], ssd_a_log[l], ssd_d[l], rev=r, n_ctx=n_ctx) for r in (False, True)]
            yh = [_hgrn_scan(u, lower_all[l], rev=r, n_ctx=n_ctx) for r in (False, True)]
            yg = [_gdn_scan(u, gdn_dt_bias[l], gdn_a_log[l], rev=r, n_ctx=n_ctx) for r in (False, True)]
            h2, xpa, xpb, idx, gate = _outproj(ys, yh, yg, u, h, mod, p["nrm"], p["w_out"],
                                               norm_ffn[l].reshape(1, d), p["rwp"], p["rbp"], p["sg"], p["su"],
                                               p["sd"], ctx=n_ctx_tok)
            ra, rb = _routed(xpa, xpb, idx, exp_gate, exp_up, exp_down, l)
            last = l == depth - 1
            h = _combine(h2, ra, rb, gate, mod, norm_final.reshape(1, d) if last else None, ctx=n_ctx_tok)
        return h

    groups = SAMPLE_GROUPS if bsz % (SAMPLE_GROUPS * SCAN_BATCH) == 0 else 1
    gs = bsz // groups
    outs = [forward(x[g * gs:(g + 1) * gs], c[g * gs:(g + 1) * gs], ctx[g * gs:(g + 1) * gs]) for g in range(groups)]
    return outs[0] if groups == 1 else jnp.concatenate(outs, axis=0)
```

```python
import functools

import numpy as np
import jax
import jax.numpy as jnp
from jax import lax
from jax.experimental import pallas as pl
from jax.experimental.pallas import tpu as pltpu
from jax.experimental.pallas import tpu_sc as plsc

F32 = jnp.float32
BF16 = jnp.bfloat16

D_MODEL = 1024
GRID_W = 64
CONV_W = 5
EPS = 1e-6
MIN_LOWER = 1e-30
MASKED_EXPONENT = -1e30
HEAD = 64
SSD_HEADS = 8
SSD_WIDTH = 512
SSD_STATE = 128
SSD_GROUPS = 2
HG_WIDTH = 256
GD_WIDTH = 256
GD_HEADS = 4
N_EXPERTS = 64
TOP_K = 8
EXPERT_DIM = 256
ROUTED_SCALE = 2.5

LANES = 128
SCAN_CHUNK = 128
SAMPLE_GROUPS = 1
SCAN_BATCH = 4
BASE = 16
TOKEN_TILE = 256
EXPERT_ROWS = 1024
GATHER_WINDOW = 128
VMEM_LIMIT = 56 * 1024 * 1024

Z_OFF, X_OFF, BC_OFF = 0, 512, 1024
HQ_OFF, HFF_OFF, HFB_OFF, HI_OFF, HGATE_OFF = 1536, 1792, 2048, 2304, 2560
GQ_OFF, GK_OFF, GV_OFF, GGATE_OFF = 2816, 3072, 3328, 3584
SM_OFF = 3840
NCOLS = 3968
SSD_CONV = 1024
GDN_CONV = 768
SM_DT = (0, 8)
SM_A = (16, 20)
SM_B = (24, 28)


def _permute_w_in(w):
    d = w.shape[0]
    parts = [w[:, 0:1536],
             w[:, 1552:2832],
             w[:, 2832:3600],
             w[:, 3616:3872],
             w[:, 1536:1552],
             w[:, 3600:3616],
             jnp.zeros((d, LANES - 32), w.dtype)]
    return jnp.concatenate(parts, axis=1)


def _dot(a, b):
    return lax.dot_general(a, b, (((1,), (0,)), ((), ())), preferred_element_type=F32)


def _dot_nt(a, b):
    return lax.dot_general(a, b, (((1,), (1,)), ((), ())), preferred_element_type=F32)


def _split(a):
    hi = a.astype(BF16)
    lo = (a - hi.astype(F32)).astype(BF16)
    return hi, lo


def _dot3(a, b):
    ah, al = _split(a)
    bh, bl = _split(b)
    return _dot(ah, bh) + (_dot(ah, bl) + _dot(al, bh))


def _dot3_exact_rhs(a, b_bf16):
    ah, al = _split(a)
    return _dot(ah, b_bf16) + _dot(al, b_bf16)


def _silu(x):
    return x * jax.nn.sigmoid(x)


def _softplus(x):
    return jnp.maximum(x, 0.0) + jnp.log1p(jnp.exp(-jnp.abs(x)))


def _log_sigmoid(x):
    return jnp.minimum(x, 0.0) - jnp.log1p(jnp.exp(-jnp.abs(x)))


def _params(sem):
    return pltpu.CompilerParams(dimension_semantics=sem, vmem_limit_bytes=VMEM_LIMIT)


def _iota(shape, dim):
    return lax.broadcasted_iota(jnp.int32, shape, dim)


def _ones_where(mask):
    return jnp.where(mask, 1.0, 0.0).astype(BF16)


def _shr(x, div):
    return jnp.right_shift(x, int(np.log2(div)))


def _scan_mask(n, rev):
    r, c = _iota((n, n), 0), _iota((n, n), 1)
    return (c >= r) if rev else (c <= r)


def _expand_matrix(lane0, width):
    r, c = _iota((LANES, width), 0), _iota((LANES, width), 1)
    return _ones_where(r == lane0 + _shr(c, HEAD))


def _head_blocks(n):
    r, c = _iota((n, n), 0), _iota((n, n), 1)
    return _shr(r, HEAD) == _shr(c, HEAD)


def _cumsum_rows(x, rev):
    tri = _ones_where(_scan_mask(x.shape[0], rev))
    xh, xl = _split(x)
    return _dot(tri, xh) + _dot(tri, xl)


def _mod_kernel(s_ref, w_ref, b_ref, o_ref):
    s = _silu(s_ref[...])
    o_ref[...] = _dot(s.astype(BF16), w_ref[...].astype(BF16)) + b_ref[...]


def _modulation(cond, w, b):
    rows, d = cond.shape
    n = w.shape[1]
    bn = d
    return pl.pallas_call(
        _mod_kernel,
        grid=(n // bn,),
        in_specs=[pl.BlockSpec((rows, d), lambda j: (0, 0)),
                  pl.BlockSpec((d, bn), lambda j: (0, j)),
                  pl.BlockSpec((1, bn), lambda j: (0, j))],
        out_specs=pl.BlockSpec((rows, bn), lambda j: (0, j)),
        out_shape=jax.ShapeDtypeStruct((rows, n), F32),
        compiler_params=_params(("parallel",)),
        name="modulation",
    )(cond, w, b.reshape(1, n))


def _inproj_kernel(h_ref, sc_ref, sh_ref, nw_ref, w_ref, cws_ref, cbs_ref, cwg_ref, u_ref, pad_ref, *, ctx_tiles):
    tm = h_ref.shape[1]
    t = pl.program_id(1)
    h = h_ref[0]
    a = h * lax.rsqrt(jnp.mean(h * h, axis=-1, keepdims=True) + EPS) * nw_ref[...]
    a = a * (1.0 + sc_ref[0]) + sh_ref[0]
    ab = a.astype(BF16)
    seg = jnp.where(t < ctx_tiles, tm, GRID_W)
    pos = _iota((tm, 1), 0) & (seg - 1)
    half = CONV_W // 2
    masks = {o: (pos + o >= 0) & (pos + o < seg) for o in range(-half, half + 1) if o}

    halo = 8
    pad_ref[0:halo, :] = jnp.zeros((halo, pad_ref.shape[1]), F32)
    pad_ref[halo + tm:2 * halo + tm, :] = jnp.zeros((halo, pad_ref.shape[1]), F32)

    def conv(x, cw_ref, c0, wd, cb_ref):
        pad_ref[halo:halo + tm, 0:wd] = x
        acc = x * cw_ref[half:half + 1, c0:c0 + wd]
        for j in range(CONV_W):
            o = j - half
            if o == 0:
                continue
            shifted = pad_ref[halo + o:halo + o + tm, 0:wd]
            acc = acc + jnp.where(masks[o], shifted, 0.0) * cw_ref[j:j + 1, c0:c0 + wd]
        if cb_ref is not None:
            acc = acc + cb_ref[:, c0:c0 + wd]
        return _silu(acc)

    step = 2 * LANES
    for c0 in range(0, NCOLS, step):
        wd = min(step, NCOLS - c0)
        u = _dot(ab, w_ref[:, c0:c0 + wd])
        if X_OFF <= c0 < X_OFF + SSD_CONV:
            u = conv(u, cws_ref, c0 - X_OFF, wd, cbs_ref)
        elif GQ_OFF <= c0 < GQ_OFF + GDN_CONV:
            u = conv(u, cwg_ref, c0 - GQ_OFF, wd, None)
        u_ref[0, :, c0:c0 + wd] = u


def _mod_row(k, ctx_tiles, rows):
    return lambda b, t: (jnp.where(t < ctx_tiles, rows - 1, b) * 6 + k, 0, 0)


def _inproj(h, mod, norm_w, w_p, conv_s, bias_s, conv_g, *, ctx):
    bsz, t_all, d = h.shape
    tm = TOKEN_TILE
    ctx_tiles = ctx // tm
    rows = mod.shape[0] // 6
    return pl.pallas_call(
        functools.partial(_inproj_kernel, ctx_tiles=ctx_tiles),
        grid=(bsz, t_all // tm),
        in_specs=[pl.BlockSpec((1, tm, d), lambda b, t: (b, t, 0)),
                  pl.BlockSpec((1, 1, d), _mod_row(1, ctx_tiles, rows)),
                  pl.BlockSpec((1, 1, d), _mod_row(0, ctx_tiles, rows)),
                  pl.BlockSpec((1, d), lambda b, t: (0, 0)),
                  pl.BlockSpec((d, NCOLS), lambda b, t: (0, 0)),
                  pl.BlockSpec((CONV_W, SSD_CONV), lambda b, t: (0, 0)),
                  pl.BlockSpec((1, SSD_CONV), lambda b, t: (0, 0)),
                  pl.BlockSpec((CONV_W, GDN_CONV), lambda b, t: (0, 0))],
        out_specs=pl.BlockSpec((1, tm, NCOLS), lambda b, t: (b, t, 0)),
        out_shape=jax.ShapeDtypeStruct((bsz, t_all, NCOLS), F32),
        scratch_shapes=[pltpu.VMEM((tm + 16, 2 * LANES), F32)],
        compiler_params=_params(("parallel", "parallel")),
        name="in_projection",
    )(h, mod, mod, norm_w.reshape(1, d), w_p, conv_s, bias_s.reshape(1, -1), conv_g)


def _chunk_index(rev, n_ctx, n_all):
    if not rev:
        return lambda c: c
    return lambda c: jnp.where(c < n_ctx, n_ctx - 1 - c, n_all + n_ctx - 1 - c)


def _scan_call(body, u, col_blocks, extra, out_width, scratch, *, rev, n_ctx, name):
    bsz, t_all, _ = u.shape
    cn, bb = SCAN_CHUNK, SCAN_BATCH
    assert bsz % bb == 0
    n_all = t_all // cn
    cidx = _chunk_index(rev, n_ctx, n_all)

    def tok(width, off):
        return pl.BlockSpec((bb, cn, width), lambda b, c: (b, cidx(c), off // width))

    in_specs = [tok(w, off) for w, off in col_blocks]
    in_specs += [pl.BlockSpec(a.shape, lambda b, c: (0, 0)) for a in extra]
    return pl.pallas_call(
        body,
        grid=(bsz // bb, n_all),
        in_specs=in_specs,
        out_specs=tok(out_width, 0),
        out_shape=jax.ShapeDtypeStruct((bsz, t_all, out_width), F32),
        scratch_shapes=[scratch],
        compiler_params=_params(("parallel", "arbitrary")),
        name=name,
    )(*([u] * len(col_blocks)), *extra)


def _reset_state(st_ref):
    @pl.when(pl.program_id(1) == 0)
    def _():
        st_ref[...] = jnp.zeros_like(st_ref)


def _ssd_kernel(x_ref, bc_ref, sm_ref, bias_ref, arow_ref, drow_ref, y_ref, st_ref, *, rev):
    bb, cn = x_ref.shape[0], x_ref.shape[1]
    _reset_state(st_ref)
    lane0 = SM_DT[1] if rev else SM_DT[0]
    mask = _scan_mask(cn, rev)
    ex = _expand_matrix(lane0, SSD_WIDTH)
    gw = SSD_WIDTH // SSD_GROUPS
    hpg = SSD_HEADS // SSD_GROUPS
    lane_head = _shr(_iota((1, gw), 1), HEAD)
    last = 0 if rev else cn - 1
    for i in range(bb):
        xs = x_ref[i]
        bc = bc_ref[i]
        dt = _softplus(sm_ref[i] + bias_ref[...])
        da = dt * arow_ref[...]
        acs = _cumsum_rows(da, rev)
        acs_r = acs.T
        acs_e = _dot3_exact_rhs(acs, ex)
        dt_e = _dot3_exact_rhs(dt, ex)
        tot_e = acs_e[last:last + 1]
        xdt = xs * dt_e
        xdt_b = xdt.astype(BF16)
        wst = (xdt * jnp.exp(tot_e - acs_e)).astype(BF16)
        eacs = jnp.exp(acs_e)
        etot = jnp.exp(tot_e)
        for g in range(SSD_GROUPS):
            bm = bc[:, g * SSD_STATE:(g + 1) * SSD_STATE]
            cm = bc[:, (SSD_GROUPS + g) * SSD_STATE:(SSD_GROUPS + g + 1) * SSD_STATE]
            bmb, cmb = bm.astype(BF16), cm.astype(BF16)
            cb = _dot_nt(cmb, bmb)
            xg = xdt_b[:, g * gw:(g + 1) * gw]
            yd = jnp.zeros((cn, gw), F32)
            for r in range(hpg):
                ln = lane0 + g * hpg + r
                diff = acs[:, ln:ln + 1] - acs_r[ln:ln + 1, :]
                dec = jnp.where(mask, jnp.exp(jnp.where(mask, diff, 0.0)), 0.0)
                yh = _dot((cb * dec).astype(BF16), xg)
                yd = jnp.where(lane_head == r, yh, yd)
            st = st_ref[i * SSD_GROUPS + g]
            yo = _dot(cmb, st.astype(BF16)) * eacs[:, g * gw:(g + 1) * gw]
            y = yd + yo
            if drow_ref is not None:
                y = y + drow_ref[:, g * gw:(g + 1) * gw] * xs[:, g * gw:(g + 1) * gw]
            y_ref[i, :, g * gw:(g + 1) * gw] = y
            st_ref[i * SSD_GROUPS + g] = (st * etot[:, g * gw:(g + 1) * gw]
                                          + _dot(bm.T.astype(BF16), wst[:, g * gw:(g + 1) * gw]))


def _ssd_kernel_no_skip(x_ref, bc_ref, sm_ref, bias_ref, arow_ref, y_ref, st_ref, *, rev):
    _ssd_kernel(x_ref, bc_ref, sm_ref, bias_ref, arow_ref, None, y_ref, st_ref, rev=rev)


def _ssd_scan(u, dt_bias, a_log, d_skip, *, rev, n_ctx):
    di = 1 if rev else 0
    lane0 = SM_DT[di]
    bias = jnp.zeros((1, LANES), F32).at[0, lane0:lane0 + SSD_HEADS].set(dt_bias[di])
    arow = jnp.zeros((1, LANES), F32).at[0, lane0:lane0 + SSD_HEADS].set(-jnp.exp(a_log[di].astype(F32)))
    extra = [bias, arow]
    if rev:
        body = functools.partial(_ssd_kernel_no_skip, rev=rev)
    else:
        extra.append(jnp.repeat(d_skip.astype(F32), HEAD).reshape(1, SSD_WIDTH))
        body = functools.partial(_ssd_kernel, rev=rev)
    scratch = pltpu.VMEM((SCAN_BATCH * SSD_GROUPS, SSD_STATE, SSD_WIDTH // SSD_GROUPS), F32)
    return _scan_call(body, u, [(SSD_WIDTH, X_OFF), (512, BC_OFF), (LANES, SM_OFF)], extra, SSD_WIDTH, scratch,
                      rev=rev, n_ctx=n_ctx, name="ssd_scan_bwd" if rev else "ssd_scan_fwd")


def _hgrn_kernel(q_ref, f_ref, i_ref, low_ref, y_ref, st_ref, *, rev):
    bb, cn = q_ref.shape[0], q_ref.shape[1]
    wdt = HG_WIDTH
    heads = wdt // HEAD
    _reset_state(st_ref)
    lower = low_ref[0:1]
    log_lower = low_ref[1:2]
    hb = _head_blocks(wdt)
    bd = _ones_where(hb)
    lane_head = _shr(_iota((1, wdt), 1), HEAD)
    last = 0 if rev else cn - 1
    si, ti = _iota((BASE, BASE, wdt), 0), _iota((BASE, BASE, wdt), 1)
    pair_mask = (si >= ti) if rev else (si <= ti)

    for i in range(bb):
        fr = f_ref[i]
        qs = _silu(q_ref[i])
        v = i_ref[i]
        c = log_lower - fr
        tail = lambda z: jnp.log(1.0 + jnp.exp(-jnp.abs(z)))
        logf = (jnp.minimum(fr, 0.0) - tail(fr)) + (jnp.maximum(c, 0.0) + tail(c))
        kg = (1.0 - lower) * jax.nn.sigmoid(-fr)
        b = _cumsum_rows(logf, rev)
        vb = v.astype(BF16)

        st = st_ref[i]
        y_ref[i] = _dot_nt((qs * jnp.exp(b)).astype(BF16), st.astype(BF16))
        b_last = b[last:last + 1]
        kend = (kg * jnp.exp(b_last - b)).astype(BF16)
        st_ref[i] = st * jnp.exp(b_last) + jnp.where(hb, _dot(v.T.astype(BF16), kend), 0.0)

        def offdiag(t0, t1, s0, s1, r):
            br = b[r:r + 1]
            qp = qs[t0:t1] * jnp.exp(b[t0:t1] - br)
            kp = (kg[s0:s1] * jnp.exp(br - b[s0:s1])).astype(BF16)
            nt = t1 - t0
            qstack = jnp.concatenate([jnp.where(lane_head == h, qp, 0.0) for h in range(heads)], axis=0)
            att = _dot_nt(qstack.astype(BF16), kp)
            res = _dot(att.astype(BF16), vb[s0:s1])
            out = jnp.zeros((nt, wdt), F32)
            for h in range(heads):
                out = jnp.where(lane_head == h, res[h * nt:(h + 1) * nt], out)
            y_ref[i, t0:t1, :] += out

        def diag(t0, t1):
            n = t1 - t0
            bt = b[t0:t1]
            m3 = pair_mask
            diff = bt[None, :, :] - bt[:, None, :]
            w = jnp.exp(jnp.where(m3, diff, MASKED_EXPONENT))
            p = w * qs[t0:t1][None, :, :] * kg[t0:t1][:, None, :]
            r2 = _dot(p.reshape(n * n, wdt).astype(BF16), bd)
            y_ref[i, t0:t1, :] += jnp.sum(r2.reshape(n, n, wdt) * v[t0:t1][:, None, :], axis=0)

        def block(lo, hi):
            if hi - lo <= BASE:
                diag(lo, hi)
                return
            mid = (lo + hi) // 2
            if rev:
                offdiag(lo, mid, mid, hi, mid)
            else:
                offdiag(mid, hi, lo, mid, mid - 1)
            block(lo, mid)
            block(mid, hi)

        block(0, cn)


def _hgrn_scan(u, lower, *, rev, n_ctx):
    f_off = HFB_OFF if rev else HFF_OFF
    low = jnp.stack([lower, jnp.log(jnp.maximum(lower, MIN_LOWER))]).astype(F32)
    low = jnp.concatenate([low, jnp.zeros((6, HG_WIDTH), F32)], axis=0)
    scratch = pltpu.VMEM((SCAN_BATCH, HG_WIDTH, HG_WIDTH), F32)
    return _scan_call(functools.partial(_hgrn_kernel, rev=rev), u,
                      [(HG_WIDTH, HQ_OFF), (HG_WIDTH, f_off), (HG_WIDTH, HI_OFF)], [low], HG_WIDTH, scratch,
                      rev=rev, n_ctx=n_ctx, name="hgrn_scan_bwd" if rev else "hgrn_scan_fwd")


def _bdot(a, b):
    return lax.dot_general(a, b, (((2,), (1,)), ((0,), (0,))), preferred_element_type=F32)


def _unit_inverse_delta(a):
    n_rows = a.shape[-1]
    r, c = _iota((n_rows, n_rows), 0), _iota((n_rows, n_rows), 1)
    d = jnp.where((_shr(r, BASE) == _shr(c, BASE))[None], a, 0.0)
    db = d.astype(BF16)
    p = _bdot(db, db)
    n = p - d - _bdot(db, p.astype(BF16))
    e = 4
    while e < BASE:
        pb = p.astype(BF16)
        p = _bdot(pb, pb)
        n = n + p + _bdot(n.astype(BF16), p.astype(BF16))
        e *= 2
    size = BASE
    while size < n_rows:
        big = 2 * size
        off = (_shr(r, big) == _shr(c, big)) & (_shr(r, size) != _shr(c, size))
        a_off = jnp.where(off[None], a, 0.0)
        nb = n.astype(BF16)
        m = a_off + _bdot(nb, a_off.astype(BF16))
        n = n - (m + _bdot(m.astype(BF16), nb))
        size = big
    return n


def _gdn_kernel(q_ref, k_ref, v_ref, sm_ref, bias_ref, arow_ref, y_ref, st_ref, *, rev):
    bb, cn = q_ref.shape[0], q_ref.shape[1]
    wdt = GD_WIDTH
    _reset_state(st_ref)
    hb = _head_blocks(wdt)
    bd = _ones_where(hb)
    lane_head = _shr(_iota((1, wdt), 1), HEAD)
    la = SM_A[1] if rev else SM_A[0]
    lb = SM_B[1] if rev else SM_B[0]
    ex_a, ex_b = _expand_matrix(la, wdt), _expand_matrix(lb, wdt)
    mask = _scan_mask(cn, rev)
    strict = mask & (_iota((cn, cn), 0) != _iota((cn, cn), 1))
    last = 0 if rev else cn - 1
    pre, a_all = [], []
    for i in range(bb):
        q, k, v = q_ref[i], k_ref[i], v_ref[i]
        q = q * lax.rsqrt(_dot3_exact_rhs(q * q, bd) + 1e-6) * (HEAD ** -0.5)
        k = k * lax.rsqrt(_dot3_exact_rhs(k * k, bd) + 1e-6)
        sm = sm_ref[i]
        g = arow_ref[...] * _softplus(sm + bias_ref[...])
        beta = jax.nn.sigmoid(sm)
        gam = _cumsum_rows(g, rev)
        gam_r = gam.T
        gam_e = _dot3_exact_rhs(gam, ex_a)
        beta_e = _dot3_exact_rhs(beta, ex_b)
        tot_e = gam_e[last:last + 1]
        egam = jnp.exp(gam_e)
        kb = k * beta_e
        kbf = k.astype(BF16)
        rhs = jnp.concatenate([v * beta_e, kb * egam], axis=1)
        qks = []
        for h in range(GD_HEADS):
            ln = la + h
            diff = gam[:, ln:ln + 1] - gam_r[ln:ln + 1, :]
            dec = jnp.where(mask, jnp.exp(jnp.where(mask, diff, 0.0)), 0.0)
            hm = lane_head == h
            kk = _dot_nt(jnp.where(hm, kb, 0.0).astype(BF16), kbf)
            a_all.append(jnp.where(strict, kk * dec, 0.0))
            qks.append((_dot_nt(jnp.where(hm, q, 0.0).astype(BF16), kbf) * dec).astype(BF16))
        pre.append((q, k, rhs, qks, egam, gam_e, tot_e))
    n_all = _unit_inverse_delta(jnp.stack(a_all)).astype(BF16)
    for i in range(bb):
        q, k, rhs, qks, egam, gam_e, tot_e = pre[i]
        rhs_b = rhs.astype(BF16)
        u_all = jnp.zeros((cn, wdt), F32)
        w_all = jnp.zeros((cn, wdt), F32)
        for h in range(GD_HEADS):
            sol = rhs + _dot(n_all[i * GD_HEADS + h], rhs_b)
            hm = lane_head == h
            u_all = jnp.where(hm, sol[:, :wdt], u_all)
            w_all = jnp.where(hm, sol[:, wdt:], w_all)
        st = st_ref[i]
        stb = st.astype(BF16)
        v_new = u_all - _dot(w_all.astype(BF16), stb)
        vnb = v_new.astype(BF16)
        o = _dot((q * egam).astype(BF16), stb)
        for h in range(GD_HEADS):
            o = o + jnp.where(lane_head == h, _dot(qks[h], vnb), 0.0)
        y_ref[i] = o
        kend = k * jnp.exp(tot_e - gam_e)
        st_ref[i] = st * jnp.exp(tot_e) + jnp.where(hb, _dot(kend.T.astype(BF16), vnb), 0.0)


def _gdn_scan(u, dt_bias, a_log, *, rev, n_ctx):
    di = 1 if rev else 0
    la = SM_A[di]
    bias = jnp.zeros((1, LANES), F32).at[0, la:la + GD_HEADS].set(dt_bias[di])
    arow = jnp.zeros((1, LANES), F32).at[0, la:la + GD_HEADS].set(-jnp.exp(a_log[di].astype(F32)))
    scratch = pltpu.VMEM((SCAN_BATCH, GD_WIDTH, GD_WIDTH), F32)
    return _scan_call(functools.partial(_gdn_kernel, rev=rev), u,
                      [(GD_WIDTH, GQ_OFF), (GD_WIDTH, GK_OFF), (GD_WIDTH, GV_OFF), (LANES, SM_OFF)], [bias, arow],
                      GD_WIDTH, scratch, rev=rev, n_ctx=n_ctx, name="gdn_scan_bwd" if rev else "gdn_scan_fwd")


def _pack_rows(x):
    half = x.shape[1] // 2
    q = half // 2
    xr = x.astype(BF16).astype(F32)
    lo = lax.bitcast_convert_type(xr[:, :half], jnp.uint32)
    hi = lax.bitcast_convert_type(xr[:, half:], jnp.uint32)
    word = (lo >> 16) | (hi & jnp.uint32(0xFFFF0000))
    return word[:, :q], word[:, q:]


def _unpack_rows(wa, wb):
    mask = jnp.uint32(0xFFFF0000)
    f = lambda w: lax.bitcast_convert_type(w, F32)
    return f(wa << 16), f(wb << 16), f(wa & mask), f(wb & mask)


def _group_rms(y, group, bd):
    ms = _dot3_exact_rhs(y * y, bd) * (1.0 / group)
    return y * lax.rsqrt(ms + EPS)


def _outproj_kernel(ysf, ysb, yhf, yhb, ygf, ygb, z_ref, hg_ref, gg_ref, h_ref, gm_ref, scf_ref, shf_ref, gf_ref,
                    nrm_ref, wout_ref, nf_ref, rw_ref, rb_ref, bds_ref, bdh_ref,
                    h1_ref, xpa_ref, xpb_ref, idx_ref, gate_ref):
    tm = h_ref.shape[1]
    gw = SSD_WIDTH // SSD_GROUPS
    bd_s = bds_ref[...]
    bd_h = bdh_ref[...]
    ys = (ysf[0] + ysb[0]) * _silu(z_ref[0])
    ys = _group_rms(ys, gw, bd_s) * nrm_ref[:, 0:SSD_WIDTH]
    yh = _group_rms(yhf[0] + yhb[0], HEAD, bd_h) * nrm_ref[:, SSD_WIDTH:SSD_WIDTH + HG_WIDTH] * _silu(hg_ref[0])
    yg = _group_rms(ygf[0] + ygb[0], HEAD, bd_h) * nrm_ref[:, SSD_WIDTH + HG_WIDTH:] * _silu(gg_ref[0])
    y = jnp.concatenate([ys, yh, yg], axis=1).astype(BF16)
    h1 = h_ref[0] + gm_ref[0] * _dot(y, wout_ref[...])
    xf = h1 * lax.rsqrt(jnp.mean(h1 * h1, axis=-1, keepdims=True) + EPS) * nf_ref[...]
    xf = xf * (1.0 + scf_ref[0]) + shf_ref[0]
    xpa_ref[0], xpb_ref[0] = _pack_rows(xf)
    h1_ref[0] = h1
    scores = jax.nn.sigmoid(_dot3(xf, rw_ref[...]))
    lane = _iota((tm, LANES), 1)
    sel = jnp.where(lane < N_EXPERTS, scores + rb_ref[...], -jnp.inf)
    rank_code = (LANES - lane).astype(F32)
    idx_f = jnp.zeros((tm, LANES), F32)
    gate_out = jnp.zeros((tm, LANES), F32)
    gsum = jnp.zeros((tm, 1), F32)
    for j in range(TOP_K):
        m = jnp.max(sel, axis=-1, keepdims=True)
        code = jnp.max(jnp.where(sel == m, rank_code, 0.0), axis=-1, keepdims=True)
        hit = rank_code == code
        gv = jnp.sum(jnp.where(hit, scores, 0.0), axis=-1, keepdims=True)
        idx_f = jnp.where(lane == j, LANES - code, idx_f)
        gate_out = jnp.where(lane == j, gv, gate_out)
        gsum = gsum + gv
        sel = jnp.where(hit, -jnp.inf, sel)
    idx_ref[0] = idx_f.astype(jnp.int32)
    gate_ref[0] = gate_out / gsum * ROUTED_SCALE


def _shared_kernel(h1_ref, xa_ref, xb_ref, gf_ref, sg_ref, su_ref, sd_ref, o_ref):
    q = xa_ref.shape[2]
    parts = [p.astype(BF16) for p in _unpack_rows(xa_ref[0], xb_ref[0])]

    def proj(w_ref):
        acc = _dot(parts[0], w_ref[0:q, :])
        for j in range(1, 4):
            acc = acc + _dot(parts[j], w_ref[j * q:(j + 1) * q, :])
        return acc

    hid = _silu(proj(sg_ref)) * proj(su_ref)
    o_ref[0] = h1_ref[0] + gf_ref[0] * _dot(hid.astype(BF16), sd_ref[...])


def _shared_expert(h1, xpa, xpb, mod, sg, su, sd, *, ctx):
    bsz, t_all, d = h1.shape
    tm = TOKEN_TILE
    ctx_tiles = ctx // tm
    rows = mod.shape[0] // 6
    pw = xpa.shape[-1]
    tok = lambda w: pl.BlockSpec((1, tm, w), lambda b, t: (b, t, 0))
    full = lambda a: pl.BlockSpec(a.shape, lambda b, t: (0,) * a.ndim)
    return pl.pallas_call(
        _shared_kernel,
        grid=(bsz, t_all // tm),
        in_specs=[tok(d), tok(pw), tok(pw), pl.BlockSpec((1, 1, d), _mod_row(5, ctx_tiles, rows)),
                  full(sg), full(su), full(sd)],
        out_specs=tok(d),
        out_shape=jax.ShapeDtypeStruct((bsz, t_all, d), F32),
        compiler_params=_params(("parallel", "parallel")),
        name="shared_expert",
    )(h1, xpa, xpb, mod, sg, su, sd)


def _outproj(ys, yh, yg, u, h, mod, nrm, w_out, norm_ffn, router_wp, router_bp, *, ctx):
    bsz, t_all, d = h.shape
    tm = TOKEN_TILE
    ctx_tiles = ctx // tm
    rows = mod.shape[0] // 6
    mrow = lambda k: pl.BlockSpec((1, 1, d), _mod_row(k, ctx_tiles, rows))
    tok = lambda w, j=0: pl.BlockSpec((1, tm, w), lambda b, t: (b, t, j))
    full = lambda a: pl.BlockSpec(a.shape, lambda b, t: (0,) * a.ndim)
    in_specs = [tok(SSD_WIDTH), tok(SSD_WIDTH), tok(HG_WIDTH), tok(HG_WIDTH), tok(GD_WIDTH), tok(GD_WIDTH),
                tok(SSD_WIDTH, Z_OFF // SSD_WIDTH), tok(HG_WIDTH, HGATE_OFF // HG_WIDTH),
                tok(GD_WIDTH, GGATE_OFF // GD_WIDTH),
                tok(d), mrow(2), mrow(4), mrow(3), mrow(5),
                full(nrm), full(w_out), full(norm_ffn), full(router_wp), full(router_bp)]
    group_matrix = lambda n, g: (jnp.arange(n)[:, None] // g == jnp.arange(n)[None, :] // g).astype(BF16)
    bd_s = group_matrix(SSD_WIDTH, SSD_WIDTH // SSD_GROUPS)
    bd_h = group_matrix(HG_WIDTH, HEAD)
    in_specs += [full(bd_s), full(bd_h)]
    pw = d // 4
    out_shape = (jax.ShapeDtypeStruct((bsz, t_all, d), F32),
                 jax.ShapeDtypeStruct((bsz, t_all, pw), jnp.uint32), jax.ShapeDtypeStruct((bsz, t_all, pw), jnp.uint32),
                 jax.ShapeDtypeStruct((bsz, t_all, LANES), jnp.int32), jax.ShapeDtypeStruct((bsz, t_all, LANES), F32))
    out_specs = (tok(d), tok(pw), tok(pw), tok(LANES), tok(LANES))
    return pl.pallas_call(
        _outproj_kernel,
        grid=(bsz, t_all // tm),
        in_specs=in_specs,
        out_specs=out_specs,
        out_shape=out_shape,
        compiler_params=_params(("parallel", "parallel")),
        name="out_projection_router",
    )(ys[0], ys[1], yh[0], yh[1], yg[0], yg[1], u, u, u, h, mod, mod, mod, mod,
      nrm, w_out, norm_ffn, router_wp, router_bp, bd_s, bd_h)


def _rank_kernel(idx_ref, rank_ref, exp_ref, cnt_ref, base_ref):
    tm = idx_ref.shape[0]

    @pl.when(pl.program_id(0) == 0)
    def _():
        base_ref[...] = jnp.zeros_like(base_ref)

    idx = idx_ref[...]
    lane = _iota((tm, LANES), 1)
    hits = [lane == idx[:, j:j + 1] for j in range(TOP_K)]
    m = jnp.zeros((tm, LANES), F32)
    for hit in hits:
        m = m + jnp.where(hit, 1.0, 0.0)
    before = _ones_where(_iota((tm, tm), 1) < _iota((tm, tm), 0))
    base = base_ref[...]
    val = _dot(before, m.astype(BF16)) + base
    out = jnp.zeros((tm, LANES), F32)
    for j, hit in enumerate(hits):
        rj = jnp.sum(jnp.where(hit, val, 0.0), axis=-1, keepdims=True)
        out = jnp.where(lane == j, rj, out)
    rank_ref[...] = out.T[0:TOP_K].astype(jnp.int32)
    exp_ref[...] = idx.astype(F32).T[0:TOP_K].astype(jnp.int32)
    total = base + jnp.sum(m, axis=0, keepdims=True)
    base_ref[...] = total
    cnt_ref[...] = total


def _route_ranks(idx2d):
    n_tok = idx2d.shape[0]
    tm = TOKEN_TILE
    per_choice = lambda: pl.BlockSpec((TOP_K, tm), lambda i: (0, i))
    return pl.pallas_call(
        _rank_kernel,
        grid=(n_tok // tm,),
        in_specs=[pl.BlockSpec((tm, LANES), lambda i: (i, 0))],
        out_specs=(per_choice(), per_choice(), pl.BlockSpec((1, LANES), lambda i: (0, 0))),
        out_shape=(jax.ShapeDtypeStruct((TOP_K, n_tok), jnp.int32), jax.ShapeDtypeStruct((TOP_K, n_tok), jnp.int32),
                   jax.ShapeDtypeStruct((1, LANES), F32)),
        scratch_shapes=[pltpu.VMEM((1, LANES), F32)],
        compiler_params=_params(("arbitrary",)),
        name="route_ranks",
    )(idx2d)


def _expert_kernel(be_ref, xa_ref, xb_ref, wg_ref, wu_ref, wd_ref, oa_ref, ob_ref, wgb_ref, wub_ref, wdb_ref):
    q = xa_ref.shape[1]
    i = pl.program_id(0)
    n_live = be_ref[be_ref.shape[0] - 1]

    @pl.when((i == 0) | (be_ref[i] != be_ref[jnp.maximum(i - 1, 0)]))
    def _():
        wgb_ref[...] = wg_ref[0].astype(BF16)
        wub_ref[...] = wu_ref[0].astype(BF16)
        wdb_ref[...] = wd_ref[0].astype(BF16)

    @pl.when(i < n_live)
    def _():
        parts = [p.astype(BF16) for p in _unpack_rows(xa_ref[...], xb_ref[...])]

        def proj(w_ref):
            acc = _dot(parts[0], w_ref[0:q, :])
            for j in range(1, 4):
                acc = acc + _dot(parts[j], w_ref[j * q:(j + 1) * q, :])
            return acc

        hid = _silu(proj(wgb_ref)) * proj(wub_ref)
        oa_ref[...], ob_ref[...] = _pack_rows(_dot(hid.astype(BF16), wdb_ref[...]))

    @pl.when(i >= n_live)
    def _():
        oa_ref[...] = jnp.zeros_like(oa_ref)
        ob_ref[...] = jnp.zeros_like(ob_ref)


def _expert_blocks(xsa, xsb, block_expert, w_gate, w_up, w_down):
    n_rows, pw = xsa.shape
    bm = EXPERT_ROWS
    d, e_dim = w_gate.shape[1], w_gate.shape[2]
    assert block_expert.shape[0] == n_rows // bm + 1
    row = lambda: pl.BlockSpec((bm, pw), lambda i, be: (i, 0))
    grid_spec = pltpu.PrefetchScalarGridSpec(
        num_scalar_prefetch=1,
        grid=(n_rows // bm,),
        in_specs=[row(), row(),
                  pl.BlockSpec((1, d, e_dim), lambda i, be: (be[i], 0, 0)),
                  pl.BlockSpec((1, d, e_dim), lambda i, be: (be[i], 0, 0)),
                  pl.BlockSpec((1, e_dim, d), lambda i, be: (be[i], 0, 0))],
        out_specs=(row(), row()),
        scratch_shapes=[pltpu.VMEM((d, e_dim), BF16), pltpu.VMEM((d, e_dim), BF16), pltpu.VMEM((e_dim, d), BF16)],
    )
    return pl.pallas_call(
        _expert_kernel,
        grid_spec=grid_spec,
        out_shape=(jax.ShapeDtypeStruct((n_rows, pw), jnp.uint32), jax.ShapeDtypeStruct((n_rows, pw), jnp.uint32)),
        compiler_params=_params(("arbitrary",)),
        name="routed_experts",
    )(block_expert, xsa, xsb, w_gate, w_up, w_down)


def _combine_kernel(h_ref, ra_ref, rb_ref, gate_ref, gf_ref, *rest):
    nw_ref, o_ref = (rest[0], rest[1]) if len(rest) == 2 else (None, rest[0])
    tm, q = ra_ref.shape[1], ra_ref.shape[2]
    gate = gate_ref[0]
    acc = [jnp.zeros((tm, q), F32) for _ in range(4)]
    for j in range(TOP_K):
        gj = gate[:, j:j + 1]
        for blk, part in enumerate(_unpack_rows(ra_ref[j], rb_ref[j])):
            acc[blk] = acc[blk] + part * gj
    out = [h_ref[0, :, blk * q:(blk + 1) * q] + gf_ref[0, :, blk * q:(blk + 1) * q] * acc[blk] for blk in range(4)]
    if nw_ref is not None:
        ms = sum(jnp.sum(o * o, axis=-1, keepdims=True) for o in out) * (1.0 / (4 * q))
        scale = lax.rsqrt(ms + EPS)
        out = [o * scale * nw_ref[:, blk * q:(blk + 1) * q] for blk, o in enumerate(out)]
    for blk in range(4):
        o_ref[0, :, blk * q:(blk + 1) * q] = out[blk]


def _combine(h2, ra, rb, gate, mod, final_w, *, ctx):
    bsz, t_all, d = h2.shape
    tm = TOKEN_TILE
    tiles = t_all // tm
    ctx_tiles = ctx // tm
    rows = mod.shape[0] // 6
    pw = ra.shape[-1]
    skip = ctx_tiles if final_w is not None else 0
    per = lambda: pl.BlockSpec((TOP_K, tm, pw), lambda b, t: (0, b * tiles + t + skip, 0))
    in_specs = [pl.BlockSpec((1, tm, d), lambda b, t: (b, t + skip, 0)), per(), per(),
                pl.BlockSpec((1, tm, LANES), lambda b, t: (b, t + skip, 0)),
                pl.BlockSpec((1, 1, d), lambda b, t: _mod_row(5, ctx_tiles, rows)(b, t + skip))]
    args = [h2, ra, rb, gate, mod]
    if final_w is not None:
        in_specs.append(pl.BlockSpec((1, d), lambda b, t: (0, 0)))
        args.append(final_w)
    return pl.pallas_call(
        _combine_kernel,
        grid=(bsz, tiles - skip),
        in_specs=in_specs,
        out_specs=pl.BlockSpec((1, tm, d), lambda b, t: (b, t, 0)),
        out_shape=jax.ShapeDtypeStruct((bsz, t_all - skip * tm, d), F32),
        compiler_params=_params(("parallel", "parallel")),
        name="moe_combine_final" if final_w is not None else "moe_combine",
    )(*args)


def _gather_rows(xa, xb, indices):
    n = indices.shape[0]
    q = xa.shape[1]
    win = GATHER_WINDOW
    mesh = plsc.VectorSubcoreMesh(core_axis_name="core", subcore_axis_name="subcore")
    workers = mesh.num_cores * mesh.num_subcores
    per = n // (win * workers)
    assert per * win * workers == n, (n, win, workers)
    out = jax.ShapeDtypeStruct((n, q), xa.dtype)
    scratch = [pltpu.VMEM((per, win), jnp.int32), pltpu.VMEM((win, q), xa.dtype), pltpu.VMEM((win, q), xa.dtype),
               pltpu.SemaphoreType.DMA((4,))]

    @functools.partial(pl.kernel, out_type=(out, out), mesh=mesh, scratch_types=scratch)
    def gather(xa_hbm, xb_hbm, i_hbm, oa_hbm, ob_hbm, idx_vmem, buf_a, buf_b, sems):
        wid = lax.axis_index("core") * mesh.num_subcores + lax.axis_index("subcore")
        pltpu.sync_copy(i_hbm.at[wid], idx_vmem)

        @pl.loop(0, per)
        def _(s):
            rows = pl.ds(pl.multiple_of((wid * per + s) * win, win), win)
            ga = pltpu.async_copy(xa_hbm.at[idx_vmem.at[s]], buf_a, sems.at[0])
            gb = pltpu.async_copy(xb_hbm.at[idx_vmem.at[s]], buf_b, sems.at[1])
            ga.wait()
            wa = pltpu.async_copy(buf_a, oa_hbm.at[rows], sems.at[2])
            gb.wait()
            wb = pltpu.async_copy(buf_b, ob_hbm.at[rows], sems.at[3])
            wa.wait()
            wb.wait()

    return gather(xa, xb, indices.reshape(workers, per, win))


def _scatter_rows(xa, xb, indices, n_out):
    n = indices.shape[0]
    n_src, q = xa.shape
    win = GATHER_WINDOW
    mesh = plsc.VectorSubcoreMesh(core_axis_name="core", subcore_axis_name="subcore")
    workers = mesh.num_cores * mesh.num_subcores
    per = n // (win * workers)
    assert per * win * workers == n and n_src % win == 0, (n, n_src, win, workers)
    src_windows = n_src // win
    out = jax.ShapeDtypeStruct((n_out, q), xa.dtype)
    scratch = [pltpu.VMEM((per, win), jnp.int32), pltpu.VMEM((win, q), xa.dtype), pltpu.VMEM((win, q), xa.dtype),
               pltpu.SemaphoreType.DMA((4,))]

    @functools.partial(pl.kernel, out_type=(out, out), mesh=mesh, scratch_types=scratch)
    def scatter(xa_hbm, xb_hbm, i_hbm, oa_hbm, ob_hbm, idx_vmem, buf_a, buf_b, sems):
        wid = lax.axis_index("core") * mesh.num_subcores + lax.axis_index("subcore")
        pltpu.sync_copy(i_hbm.at[wid], idx_vmem)

        @pl.loop(0, per)
        def _(s):
            src = lax.rem(wid * per + s, src_windows)
            rows = pl.ds(pl.multiple_of(src * win, win), win)
            la = pltpu.async_copy(xa_hbm.at[rows], buf_a, sems.at[0])
            lb = pltpu.async_copy(xb_hbm.at[rows], buf_b, sems.at[1])
            la.wait()
            sa = pltpu.async_copy(buf_a, oa_hbm.at[idx_vmem.at[s]], sems.at[2])
            lb.wait()
            sb = pltpu.async_copy(buf_b, ob_hbm.at[idx_vmem.at[s]], sems.at[3])
            sa.wait()
            sb.wait()

    return scatter(xa, xb, indices.reshape(workers, per, win))


def _routed(xpa, xpb, idx, w_gate, w_up, w_down, layer):
    bsz, t_all, pw = xpa.shape
    n_tok = bsz * t_all
    n_assign = n_tok * TOP_K
    bm = EXPERT_ROWS
    n_blocks = -(-n_assign // bm) + N_EXPERTS
    n_slots = n_blocks * bm
    idx2d = idx.reshape(n_tok, LANES)
    rank_t, exp_t, cnt = _route_ranks(idx2d)
    counts = cnt[0, :N_EXPERTS].astype(jnp.int32)
    padded = (counts + bm - 1) // bm * bm
    pad_end = jnp.cumsum(padded)
    offset = pad_end - padded
    slot_t = jnp.take(offset, exp_t.reshape(-1)) + rank_t.reshape(-1)
    block_expert = jnp.minimum(
        jnp.sum(pad_end[None, :] <= (jnp.arange(n_blocks, dtype=jnp.int32) * bm)[:, None], axis=1),
        N_EXPERTS - 1).astype(jnp.int32)
    xsa, xsb = _scatter_rows(xpa.reshape(n_tok, pw), xpb.reshape(n_tok, pw), slot_t, n_slots)
    stack = lambda w: w.reshape((-1,) + w.shape[2:])
    table = jnp.concatenate([block_expert + layer * N_EXPERTS, (pad_end[-1:] // bm).astype(jnp.int32)])
    oa, ob = _expert_blocks(xsa, xsb, table, stack(w_gate), stack(w_up), stack(w_down))
    ra, rb = _gather_rows(oa, ob, slot_t)
    return ra.reshape(TOP_K, n_tok, pw), rb.reshape(TOP_K, n_tok, pw)


def kernel(x, c, ctx, c_ctx, w_mod, b_mod, norm_mix, w_in, ssd_conv_w, ssd_conv_b, ssd_dt_bias, ssd_a_log, ssd_d,
           ssd_norm, hgrn_lb, hgrn_norm, gdn_conv_w, gdn_dt_bias, gdn_a_log, gdn_norm, w_out, norm_ffn, router_w,
           router_bias, exp_gate, exp_up, exp_down, sh_gate, sh_up, sh_down, norm_final):
    bsz, seq, d = x.shape
    n_ctx_tok = ctx.shape[1]
    depth = w_in.shape[0]
    assert d == D_MODEL and n_ctx_tok == TOKEN_TILE and seq % TOKEN_TILE == 0 and TOKEN_TILE % SCAN_CHUNK == 0
    assert TOKEN_TILE % GRID_W == 0 and seq % GRID_W == 0
    n_ctx = n_ctx_tok // SCAN_CHUNK

    p_lb = jax.nn.softmax(hgrn_lb.astype(F32), axis=0)
    lower_all = jnp.cumsum(p_lb, axis=0) - p_lb[0]

    layers = []
    for l in range(depth):
        nrm = jnp.concatenate([ssd_norm[l], jnp.tile(hgrn_norm[l], HG_WIDTH // HEAD),
                               jnp.tile(gdn_norm[l], GD_WIDTH // HEAD)]).astype(F32).reshape(1, d)
        layers.append(dict(
            w_p=_permute_w_in(w_in[l]).astype(BF16), nrm=nrm,
            rwp=jnp.pad(router_w[l].astype(F32), ((0, 0), (0, LANES - N_EXPERTS))),
            rbp=jnp.pad(router_bias[l].astype(F32), (0, LANES - N_EXPERTS)).reshape(1, LANES),
            w_out=w_out[l].astype(BF16), sg=sh_gate[l].astype(BF16), su=sh_up[l].astype(BF16),
            sd=sh_down[l].astype(BF16)))

    def forward(xg, cg, ctxg):
        nb = xg.shape[0]
        rows = -(-(nb + 1) // 8) * 8
        cond = jnp.zeros((rows, d), F32).at[:nb].set(cg).at[rows - 1].set(c_ctx)
        h = jnp.concatenate([ctxg, xg], axis=1)
        for l in range(depth):
            p = layers[l]
            mod = _modulation(cond, w_mod[l], b_mod[l]).reshape(rows * 6, 1, d)
            u = _inproj(h, mod, norm_mix[l], p["w_p"], ssd_conv_w[l], ssd_conv_b[l], gdn_conv_w[l], ctx=n_ctx_tok)
            ys = [_ssd_scan(u, ssd_dt_bias[l], ssd_a_log[l], ssd_d[l], rev=r, n_ctx=n_ctx) for r in (False, True)]
            yh = [_hgrn_scan(u, lower_all[l], rev=r, n_ctx=n_ctx) for r in (False, True)]
            yg = [_gdn_scan(u, gdn_dt_bias[l], gdn_a_log[l], rev=r, n_ctx=n_ctx) for r in (False, True)]
            h1, xpa, xpb, idx, gate = _outproj(ys, yh, yg, u, h, mod, p["nrm"], p["w_out"],
                                               norm_ffn[l].reshape(1, d), p["rwp"], p["rbp"], ctx=n_ctx_tok)
            ra, rb = _routed(xpa, xpb, idx, exp_gate, exp_up, exp_down, l)
            h2 = _shared_expert(h1, xpa, xpb, mod, p["sg"], p["su"], p["sd"], ctx=n_ctx_tok)
            last = l == depth - 1
            h = _combine(h2, ra, rb, gate, mod, norm_final.reshape(1, d) if last else None, ctx=n_ctx_tok)
        return h

    groups = SAMPLE_GROUPS if bsz % (SAMPLE_GROUPS * SCAN_BATCH) == 0 else 1
    gs = bsz // groups
    outs = [forward(x[g * gs:(g + 1) * gs], c[g * gs:(g + 1) * gs], ctx[g * gs:(g + 1) * gs]) for g in range(groups)]
    return outs[0] if groups == 1 else jnp.concatenate(outs, axis=0)
```

```python
import functools

import numpy as np
import jax
import jax.numpy as jnp
from jax import lax
from jax.experimental import pallas as pl
from jax.experimental.pallas import tpu as pltpu
from jax.experimental.pallas import tpu_sc as plsc

F32 = jnp.float32
BF16 = jnp.bfloat16

D_MODEL = 1024
GRID_W = 64
CONV_W = 5
EPS = 1e-6
MIN_LOWER = 1e-30
MASKED_EXPONENT = -1e30
HEAD = 64
SSD_HEADS = 8
SSD_WIDTH = 512
SSD_STATE = 128
SSD_GROUPS = 2
HG_WIDTH = 256
GD_WIDTH = 256
GD_HEADS = 4
N_EXPERTS = 64
TOP_K = 8
EXPERT_DIM = 256
ROUTED_SCALE = 2.5

LANES = 128
SCAN_CHUNK = 128
SAMPLE_GROUPS = 1
SCAN_BATCH = 4
BASE = 16
TOKEN_TILE = 256
EXPERT_ROWS = 1024
GATHER_WINDOW = 128
VMEM_LIMIT = 56 * 1024 * 1024

Z_OFF, X_OFF, BC_OFF = 0, 512, 1024
HQ_OFF, HFF_OFF, HFB_OFF, HI_OFF, HGATE_OFF = 1536, 1792, 2048, 2304, 2560
GQ_OFF, GK_OFF, GV_OFF, GGATE_OFF = 2816, 3072, 3328, 3584
SM_OFF = 3840
NCOLS = 3968
SSD_CONV = 1024
GDN_CONV = 768
SM_DT = (0, 8)
SM_A = (16, 20)
SM_B = (24, 28)


def _permute_w_in(w):
    d = w.shape[0]
    parts = [w[:, 0:1536],
             w[:, 1552:2832],
             w[:, 2832:3600],
             w[:, 3616:3872],
             w[:, 1536:1552],
             w[:, 3600:3616],
             jnp.zeros((d, LANES - 32), w.dtype)]
    return jnp.concatenate(parts, axis=1)


def _dot(a, b):
    return lax.dot_general(a, b, (((1,), (0,)), ((), ())), preferred_element_type=F32)


def _dot_nt(a, b):
    return lax.dot_general(a, b, (((1,), (1,)), ((), ())), preferred_element_type=F32)


def _split(a):
    hi = a.astype(BF16)
    lo = (a - hi.astype(F32)).astype(BF16)
    return hi, lo


def _dot3(a, b):
    ah, al = _split(a)
    bh, bl = _split(b)
    return _dot(ah, bh) + (_dot(ah, bl) + _dot(al, bh))


def _dot3_exact_rhs(a, b_bf16):
    ah, al = _split(a)
    return _dot(ah, b_bf16) + _dot(al, b_bf16)


def _silu(x):
    return x * jax.nn.sigmoid(x)


def _softplus(x):
    return jnp.maximum(x, 0.0) + jnp.log1p(jnp.exp(-jnp.abs(x)))


def _log_sigmoid(x):
    return jnp.minimum(x, 0.0) - jnp.log1p(jnp.exp(-jnp.abs(x)))


def _params(sem):
    return pltpu.CompilerParams(dimension_semantics=sem, vmem_limit_bytes=VMEM_LIMIT)


def _iota(shape, dim):
    return lax.broadcasted_iota(jnp.int32, shape, dim)


def _ones_where(mask):
    return jnp.where(mask, 1.0, 0.0).astype(BF16)


def _shr(x, div):
    return jnp.right_shift(x, int(np.log2(div)))


def _scan_mask(n, rev):
    r, c = _iota((n, n), 0), _iota((n, n), 1)
    return (c >= r) if rev else (c <= r)


def _expand_matrix(lane0, width):
    r, c = _iota((LANES, width), 0), _iota((LANES, width), 1)
    return _ones_where(r == lane0 + _shr(c, HEAD))


def _head_blocks(n):
    r, c = _iota((n, n), 0), _iota((n, n), 1)
    return _shr(r, HEAD) == _shr(c, HEAD)


def _cumsum_rows(x, rev):
    tri = _ones_where(_scan_mask(x.shape[0], rev))
    xh, xl = _split(x)
    return _dot(tri, xh) + _dot(tri, xl)


def _mod_kernel(s_ref, w_ref, b_ref, o_ref):
    s = _silu(s_ref[...])
    o_ref[...] = _dot(s.astype(BF16), w_ref[...].astype(BF16)) + b_ref[...]


def _modulation(cond, w, b):
    rows, d = cond.shape
    n = w.shape[1]
    bn = d
    return pl.pallas_call(
        _mod_kernel,
        grid=(n // bn,),
        in_specs=[pl.BlockSpec((rows, d), lambda j: (0, 0)),
                  pl.BlockSpec((d, bn), lambda j: (0, j)),
                  pl.BlockSpec((1, bn), lambda j: (0, j))],
        out_specs=pl.BlockSpec((rows, bn), lambda j: (0, j)),
        out_shape=jax.ShapeDtypeStruct((rows, n), F32),
        compiler_params=_params(("parallel",)),
        name="modulation",
    )(cond, w, b.reshape(1, n))


def _inproj_kernel(h_ref, sc_ref, sh_ref, nw_ref, w_ref, cws_ref, cbs_ref, cwg_ref, u_ref, pad_ref, *, ctx_tiles):
    tm = h_ref.shape[1]
    t = pl.program_id(1)
    h = h_ref[0]
    a = h * lax.rsqrt(jnp.mean(h * h, axis=-1, keepdims=True) + EPS) * nw_ref[...]
    a = a * (1.0 + sc_ref[0]) + sh_ref[0]
    ab = a.astype(BF16)
    seg = jnp.where(t < ctx_tiles, tm, GRID_W)
    pos = _iota((tm, 1), 0) & (seg - 1)
    half = CONV_W // 2
    masks = {o: (pos + o >= 0) & (pos + o < seg) for o in range(-half, half + 1) if o}

    halo = 8
    pad_ref[0:halo, :] = jnp.zeros((halo, pad_ref.shape[1]), F32)
    pad_ref[halo + tm:2 * halo + tm, :] = jnp.zeros((halo, pad_ref.shape[1]), F32)

    def conv(x, cw_ref, c0, wd, cb_ref):
        pad_ref[halo:halo + tm, 0:wd] = x
        acc = x * cw_ref[half:half + 1, c0:c0 + wd]
        for j in range(CONV_W):
            o = j - half
            if o == 0:
                continue
            shifted = pad_ref[halo + o:halo + o + tm, 0:wd]
            acc = acc + jnp.where(masks[o], shifted, 0.0) * cw_ref[j:j + 1, c0:c0 + wd]
        if cb_ref is not None:
            acc = acc + cb_ref[:, c0:c0 + wd]
        return _silu(acc)

    step = 2 * LANES
    for c0 in range(0, NCOLS, step):
        wd = min(step, NCOLS - c0)
        u = _dot(ab, w_ref[:, c0:c0 + wd])
        if X_OFF <= c0 < X_OFF + SSD_CONV:
            u = conv(u, cws_ref, c0 - X_OFF, wd, cbs_ref)
        elif GQ_OFF <= c0 < GQ_OFF + GDN_CONV:
            u = conv(u, cwg_ref, c0 - GQ_OFF, wd, None)
        u_ref[0, :, c0:c0 + wd] = u


def _mod_row(k, ctx_tiles, rows):
    return lambda b, t: (jnp.where(t < ctx_tiles, rows - 1, b) * 6 + k, 0, 0)


def _inproj(h, mod, norm_w, w_p, conv_s, bias_s, conv_g, *, ctx):
    bsz, t_all, d = h.shape
    tm = TOKEN_TILE
    ctx_tiles = ctx // tm
    rows = mod.shape[0] // 6
    return pl.pallas_call(
        functools.partial(_inproj_kernel, ctx_tiles=ctx_tiles),
        grid=(bsz, t_all // tm),
        in_specs=[pl.BlockSpec((1, tm, d), lambda b, t: (b, t, 0)),
                  pl.BlockSpec((1, 1, d), _mod_row(1, ctx_tiles, rows)),
                  pl.BlockSpec((1, 1, d), _mod_row(0, ctx_tiles, rows)),
                  pl.BlockSpec((1, d), lambda b, t: (0, 0)),
                  pl.BlockSpec((d, NCOLS), lambda b, t: (0, 0)),
                  pl.BlockSpec((CONV_W, SSD_CONV), lambda b, t: (0, 0)),
                  pl.BlockSpec((1, SSD_CONV), lambda b, t: (0, 0)),
                  pl.BlockSpec((CONV_W, GDN_CONV), lambda b, t: (0, 0))],
        out_specs=pl.BlockSpec((1, tm, NCOLS), lambda b, t: (b, t, 0)),
        out_shape=jax.ShapeDtypeStruct((bsz, t_all, NCOLS), F32),
        scratch_shapes=[pltpu.VMEM((tm + 16, 2 * LANES), F32)],
        compiler_params=_params(("parallel", "parallel")),
        name="in_projection",
    )(h, mod, mod, norm_w.reshape(1, d), w_p, conv_s, bias_s.reshape(1, -1), conv_g)


def _chunk_index(rev, n_ctx, n_all):
    if not rev:
        return lambda c: c
    return lambda c: jnp.where(c < n_ctx, n_ctx - 1 - c, n_all + n_ctx - 1 - c)


def _scan_call(body, u, col_blocks, extra, out_width, scratch, *, rev, n_ctx, name):
    bsz, t_all, _ = u.shape
    cn, bb = SCAN_CHUNK, SCAN_BATCH
    assert bsz % bb == 0
    n_all = t_all // cn
    cidx = _chunk_index(rev, n_ctx, n_all)

    def tok(width, off):
        return pl.BlockSpec((bb, cn, width), lambda b, c: (b, cidx(c), off // width))

    in_specs = [tok(w, off) for w, off in col_blocks]
    in_specs += [pl.BlockSpec(a.shape, lambda b, c: (0, 0)) for a in extra]
    return pl.pallas_call(
        body,
        grid=(bsz // bb, n_all),
        in_specs=in_specs,
        out_specs=tok(out_width, 0),
        out_shape=jax.ShapeDtypeStruct((bsz, t_all, out_width), F32),
        scratch_shapes=[scratch],
        compiler_params=_params(("parallel", "arbitrary")),
        name=name,
    )(*([u] * len(col_blocks)), *extra)


def _reset_state(st_ref):
    @pl.when(pl.program_id(1) == 0)
    def _():
        st_ref[...] = jnp.zeros_like(st_ref)


def _ssd_kernel(x_ref, bc_ref, sm_ref, bias_ref, arow_ref, drow_ref, y_ref, st_ref, *, rev):
    bb, cn = x_ref.shape[0], x_ref.shape[1]
    _reset_state(st_ref)
    lane0 = SM_DT[1] if rev else SM_DT[0]
    mask = _scan_mask(cn, rev)
    ex = _expand_matrix(lane0, SSD_WIDTH)
    gw = SSD_WIDTH // SSD_GROUPS
    hpg = SSD_HEADS // SSD_GROUPS
    lane_head = _shr(_iota((1, gw), 1), HEAD)
    last = 0 if rev else cn - 1
    for i in range(bb):
        xs = x_ref[i]
        bc = bc_ref[i]
        dt = _softplus(sm_ref[i] + bias_ref[...])
        da = dt * arow_ref[...]
        acs = _cumsum_rows(da, rev)
        acs_r = acs.T
        acs_e = _dot3_exact_rhs(acs, ex)
        dt_e = _dot3_exact_rhs(dt, ex)
        tot_e = acs_e[last:last + 1]
        xdt = xs * dt_e
        xdt_b = xdt.astype(BF16)
        wst = (xdt * jnp.exp(tot_e - acs_e)).astype(BF16)
        eacs = jnp.exp(acs_e)
        etot = jnp.exp(tot_e)
        for g in range(SSD_GROUPS):
            bm = bc[:, g * SSD_STATE:(g + 1) * SSD_STATE]
            cm = bc[:, (SSD_GROUPS + g) * SSD_STATE:(SSD_GROUPS + g + 1) * SSD_STATE]
            bmb, cmb = bm.astype(BF16), cm.astype(BF16)
            cb = _dot_nt(cmb, bmb)
            xg = xdt_b[:, g * gw:(g + 1) * gw]
            yd = jnp.zeros((cn, gw), F32)
            for r in range(hpg):
                ln = lane0 + g * hpg + r
                diff = acs[:, ln:ln + 1] - acs_r[ln:ln + 1, :]
                dec = jnp.where(mask, jnp.exp(jnp.where(mask, diff, 0.0)), 0.0)
                yh = _dot((cb * dec).astype(BF16), xg)
                yd = jnp.where(lane_head == r, yh, yd)
            st = st_ref[i * SSD_GROUPS + g]
            yo = _dot(cmb, st.astype(BF16)) * eacs[:, g * gw:(g + 1) * gw]
            y = yd + yo
            if drow_ref is not None:
                y = y + drow_ref[:, g * gw:(g + 1) * gw] * xs[:, g * gw:(g + 1) * gw]
            y_ref[i, :, g * gw:(g + 1) * gw] = y
            st_ref[i * SSD_GROUPS + g] = (st * etot[:, g * gw:(g + 1) * gw]
                                          + _dot(bm.T.astype(BF16), wst[:, g * gw:(g + 1) * gw]))


def _ssd_kernel_no_skip(x_ref, bc_ref, sm_ref, bias_ref, arow_ref, y_ref, st_ref, *, rev):
    _ssd_kernel(x_ref, bc_ref, sm_ref, bias_ref, arow_ref, None, y_ref, st_ref, rev=rev)


def _ssd_scan(u, dt_bias, a_log, d_skip, *, rev, n_ctx):
    di = 1 if rev else 0
    lane0 = SM_DT[di]
    bias = jnp.zeros((1, LANES), F32).at[0, lane0:lane0 + SSD_HEADS].set(dt_bias[di])
    arow = jnp.zeros((1, LANES), F32).at[0, lane0:lane0 + SSD_HEADS].set(-jnp.exp(a_log[di].astype(F32)))
    extra = [bias, arow]
    if rev:
        body = functools.partial(_ssd_kernel_no_skip, rev=rev)
    else:
        extra.append(jnp.repeat(d_skip.astype(F32), HEAD).reshape(1, SSD_WIDTH))
        body = functools.partial(_ssd_kernel, rev=rev)
    scratch = pltpu.VMEM((SCAN_BATCH * SSD_GROUPS, SSD_STATE, SSD_WIDTH // SSD_GROUPS), F32)
    return _scan_call(body, u, [(SSD_WIDTH, X_OFF), (512, BC_OFF), (LANES, SM_OFF)], extra, SSD_WIDTH, scratch,
                      rev=rev, n_ctx=n_ctx, name="ssd_scan_bwd" if rev else "ssd_scan_fwd")


def _hgrn_kernel(q_ref, f_ref, i_ref, low_ref, y_ref, st_ref, *, rev):
    bb, cn = q_ref.shape[0], q_ref.shape[1]
    wdt = HG_WIDTH
    heads = wdt // HEAD
    _reset_state(st_ref)
    lower = low_ref[0:1]
    log_lower = low_ref[1:2]
    hb = _head_blocks(wdt)
    bd = _ones_where(hb)
    lane_head = _shr(_iota((1, wdt), 1), HEAD)
    last = 0 if rev else cn - 1
    si, ti = _iota((BASE, BASE, wdt), 0), _iota((BASE, BASE, wdt), 1)
    pair_mask = (si >= ti) if rev else (si <= ti)

    for i in range(bb):
        fr = f_ref[i]
        qs = _silu(q_ref[i])
        v = i_ref[i]
        c = log_lower - fr
        tail = lambda z: jnp.log(1.0 + jnp.exp(-jnp.abs(z)))
        logf = (jnp.minimum(fr, 0.0) - tail(fr)) + (jnp.maximum(c, 0.0) + tail(c))
        kg = (1.0 - lower) * jax.nn.sigmoid(-fr)
        b = _cumsum_rows(logf, rev)
        vb = v.astype(BF16)

        st = st_ref[i]
        y_ref[i] = _dot_nt((qs * jnp.exp(b)).astype(BF16), st.astype(BF16))
        b_last = b[last:last + 1]
        kend = (kg * jnp.exp(b_last - b)).astype(BF16)
        st_ref[i] = st * jnp.exp(b_last) + jnp.where(hb, _dot(v.T.astype(BF16), kend), 0.0)

        def offdiag(t0, t1, s0, s1, r):
            br = b[r:r + 1]
            qp = qs[t0:t1] * jnp.exp(b[t0:t1] - br)
            kp = (kg[s0:s1] * jnp.exp(br - b[s0:s1])).astype(BF16)
            nt = t1 - t0
            qstack = jnp.concatenate([jnp.where(lane_head == h, qp, 0.0) for h in range(heads)], axis=0)
            att = _dot_nt(qstack.astype(BF16), kp)
            res = _dot(att.astype(BF16), vb[s0:s1])
            out = jnp.zeros((nt, wdt), F32)
            for h in range(heads):
                out = jnp.where(lane_head == h, res[h * nt:(h + 1) * nt], out)
            y_ref[i, t0:t1, :] += out

        def diag(t0, t1):
            n = t1 - t0
            bt = b[t0:t1]
            m3 = pair_mask
            diff = bt[None, :, :] - bt[:, None, :]
            w = jnp.exp(jnp.where(m3, diff, MASKED_EXPONENT))
            p = w * qs[t0:t1][None, :, :] * kg[t0:t1][:, None, :]
            r2 = _dot(p.reshape(n * n, wdt).astype(BF16), bd)
            y_ref[i, t0:t1, :] += jnp.sum(r2.reshape(n, n, wdt) * v[t0:t1][:, None, :], axis=0)

        def block(lo, hi):
            if hi - lo <= BASE:
                diag(lo, hi)
                return
            mid = (lo + hi) // 2
            if rev:
                offdiag(lo, mid, mid, hi, mid)
            else:
                offdiag(mid, hi, lo, mid, mid - 1)
            block(lo, mid)
            block(mid, hi)

        block(0, cn)


def _hgrn_scan(u, lower, *, rev, n_ctx):
    f_off = HFB_OFF if rev else HFF_OFF
    low = jnp.stack([lower, jnp.log(jnp.maximum(lower, MIN_LOWER))]).astype(F32)
    low = jnp.concatenate([low, jnp.zeros((6, HG_WIDTH), F32)], axis=0)
    scratch = pltpu.VMEM((SCAN_BATCH, HG_WIDTH, HG_WIDTH), F32)
    return _scan_call(functools.partial(_hgrn_kernel, rev=rev), u,
                      [(HG_WIDTH, HQ_OFF), (HG_WIDTH, f_off), (HG_WIDTH, HI_OFF)], [low], HG_WIDTH, scratch,
                      rev=rev, n_ctx=n_ctx, name="hgrn_scan_bwd" if rev else "hgrn_scan_fwd")


def _bdot(a, b):
    return lax.dot_general(a, b, (((2,), (1,)), ((0,), (0,))), preferred_element_type=F32)


def _unit_inverse_delta(a):
    n_rows = a.shape[-1]
    r, c = _iota((n_rows, n_rows), 0), _iota((n_rows, n_rows), 1)
    d = jnp.where((_shr(r, BASE) == _shr(c, BASE))[None], a, 0.0)
    db = d.astype(BF16)
    p = _bdot(db, db)
    n = p - d - _bdot(db, p.astype(BF16))
    e = 4
    while e < BASE:
        pb = p.astype(BF16)
        p = _bdot(pb, pb)
        n = n + p + _bdot(n.astype(BF16), p.astype(BF16))
        e *= 2
    size = BASE
    while size < n_rows:
        big = 2 * size
        off = (_shr(r, big) == _shr(c, big)) & (_shr(r, size) != _shr(c, size))
        a_off = jnp.where(off[None], a, 0.0)
        nb = n.astype(BF16)
        m = a_off + _bdot(nb, a_off.astype(BF16))
        n = n - (m + _bdot(m.astype(BF16), nb))
        size = big
    return n


def _gdn_kernel(q_ref, k_ref, v_ref, sm_ref, bias_ref, arow_ref, y_ref, st_ref, *, rev):
    bb, cn = q_ref.shape[0], q_ref.shape[1]
    wdt = GD_WIDTH
    _reset_state(st_ref)
    hb = _head_blocks(wdt)
    bd = _ones_where(hb)
    lane_head = _shr(_iota((1, wdt), 1), HEAD)
    la = SM_A[1] if rev else SM_A[0]
    lb = SM_B[1] if rev else SM_B[0]
    ex_a, ex_b = _expand_matrix(la, wdt), _expand_matrix(lb, wdt)
    mask = _scan_mask(cn, rev)
    strict = mask & (_iota((cn, cn), 0) != _iota((cn, cn), 1))
    last = 0 if rev else cn - 1
    pre, a_all = [], []
    for i in range(bb):
        q, k, v = q_ref[i], k_ref[i], v_ref[i]
        q = q * lax.rsqrt(_dot3_exact_rhs(q * q, bd) + 1e-6) * (HEAD ** -0.5)
        k = k * lax.rsqrt(_dot3_exact_rhs(k * k, bd) + 1e-6)
        sm = sm_ref[i]
        g = arow_ref[...] * _softplus(sm + bias_ref[...])
        beta = jax.nn.sigmoid(sm)
        gam = _cumsum_rows(g, rev)
        gam_r = gam.T
        gam_e = _dot3_exact_rhs(gam, ex_a)
        beta_e = _dot3_exact_rhs(beta, ex_b)
        tot_e = gam_e[last:last + 1]
        egam = jnp.exp(gam_e)
        kb = k * beta_e
        kbf = k.astype(BF16)
        rhs = jnp.concatenate([v * beta_e, kb * egam], axis=1)
        qks = []
        for h in range(GD_HEADS):
            ln = la + h
            diff = gam[:, ln:ln + 1] - gam_r[ln:ln + 1, :]
            dec = jnp.where(mask, jnp.exp(jnp.where(mask, diff, 0.0)), 0.0)
            hm = lane_head == h
            kk = _dot_nt(jnp.where(hm, kb, 0.0).astype(BF16), kbf)
            a_all.append(jnp.where(strict, kk * dec, 0.0))
            qks.append((_dot_nt(jnp.where(hm, q, 0.0).astype(BF16), kbf) * dec).astype(BF16))
        pre.append((q, k, rhs, qks, egam, gam_e, tot_e))
    n_all = _unit_inverse_delta(jnp.stack(a_all)).astype(BF16)
    for i in range(bb):
        q, k, rhs, qks, egam, gam_e, tot_e = pre[i]
        rhs_b = rhs.astype(BF16)
        u_all = jnp.zeros((cn, wdt), F32)
        w_all = jnp.zeros((cn, wdt), F32)
        for h in range(GD_HEADS):
            sol = rhs + _dot(n_all[i * GD_HEADS + h], rhs_b)
            hm = lane_head == h
            u_all = jnp.where(hm, sol[:, :wdt], u_all)
            w_all = jnp.where(hm, sol[:, wdt:], w_all)
        st = st_ref[i]
        stb = st.astype(BF16)
        v_new = u_all - _dot(w_all.astype(BF16), stb)
        vnb = v_new.astype(BF16)
        o = _dot((q * egam).astype(BF16), stb)
        for h in range(GD_HEADS):
            o = o + jnp.where(lane_head == h, _dot(qks[h], vnb), 0.0)
        y_ref[i] = o
        kend = k * jnp.exp(tot_e - gam_e)
        st_ref[i] = st * jnp.exp(tot_e) + jnp.where(hb, _dot(kend.T.astype(BF16), vnb), 0.0)


def _gdn_scan(u, dt_bias, a_log, *, rev, n_ctx):
    di = 1 if rev else 0
    la = SM_A[di]
    bias = jnp.zeros((1, LANES), F32).at[0, la:la + GD_HEADS].set(dt_bias[di])
    arow = jnp.zeros((1, LANES), F32).at[0, la:la + GD_HEADS].set(-jnp.exp(a_log[di].astype(F32)))
    scratch = pltpu.VMEM((SCAN_BATCH, GD_WIDTH, GD_WIDTH), F32)
    return _scan_call(functools.partial(_gdn_kernel, rev=rev), u,
                      [(GD_WIDTH, GQ_OFF), (GD_WIDTH, GK_OFF), (GD_WIDTH, GV_OFF), (LANES, SM_OFF)], [bias, arow],
                      GD_WIDTH, scratch, rev=rev, n_ctx=n_ctx, name="gdn_scan_bwd" if rev else "gdn_scan_fwd")


def _pack_rows(x):
    half = x.shape[1] // 2
    q = half // 2
    xr = x.astype(BF16).astype(F32)
    lo = lax.bitcast_convert_type(xr[:, :half], jnp.uint32)
    hi = lax.bitcast_convert_type(xr[:, half:], jnp.uint32)
    word = (lo >> 16) | (hi & jnp.uint32(0xFFFF0000))
    return word[:, :q], word[:, q:]


def _unpack_rows(wa, wb):
    mask = jnp.uint32(0xFFFF0000)
    f = lambda w: lax.bitcast_convert_type(w, F32)
    return f(wa << 16), f(wb << 16), f(wa & mask), f(wb & mask)


def _group_rms(y, group, bd):
    ms = _dot3_exact_rhs(y * y, bd) * (1.0 / group)
    return y * lax.rsqrt(ms + EPS)


def _outproj_kernel(ysf, ysb, yhf, yhb, ygf, ygb, z_ref, hg_ref, gg_ref, h_ref, gm_ref, scf_ref, shf_ref, gf_ref,
                    nrm_ref, wout_ref, nf_ref, rw_ref, rb_ref, bds_ref, bdh_ref,
                    h1_ref, xpa_ref, xpb_ref, idx_ref, gate_ref):
    tm = h_ref.shape[1]
    gw = SSD_WIDTH // SSD_GROUPS
    bd_s = bds_ref[...]
    bd_h = bdh_ref[...]
    ys = (ysf[0] + ysb[0]) * _silu(z_ref[0])
    ys = _group_rms(ys, gw, bd_s) * nrm_ref[:, 0:SSD_WIDTH]
    yh = _group_rms(yhf[0] + yhb[0], HEAD, bd_h) * nrm_ref[:, SSD_WIDTH:SSD_WIDTH + HG_WIDTH] * _silu(hg_ref[0])
    yg = _group_rms(ygf[0] + ygb[0], HEAD, bd_h) * nrm_ref[:, SSD_WIDTH + HG_WIDTH:] * _silu(gg_ref[0])
    y = jnp.concatenate([ys, yh, yg], axis=1).astype(BF16)
    h1 = h_ref[0] + gm_ref[0] * _dot(y, wout_ref[...])
    xf = h1 * lax.rsqrt(jnp.mean(h1 * h1, axis=-1, keepdims=True) + EPS) * nf_ref[...]
    xf = xf * (1.0 + scf_ref[0]) + shf_ref[0]
    xpa_ref[0], xpb_ref[0] = _pack_rows(xf)
    h1_ref[0] = h1
    scores = jax.nn.sigmoid(_dot3(xf, rw_ref[...]))
    lane = _iota((tm, LANES), 1)
    sel = jnp.where(lane < N_EXPERTS, scores + rb_ref[...], -jnp.inf)
    rank_code = (LANES - lane).astype(F32)
    idx_f = jnp.zeros((tm, LANES), F32)
    gate_out = jnp.zeros((tm, LANES), F32)
    gsum = jnp.zeros((tm, 1), F32)
    for j in range(TOP_K):
        m = jnp.max(sel, axis=-1, keepdims=True)
        code = jnp.max(jnp.where(sel == m, rank_code, 0.0), axis=-1, keepdims=True)
        hit = rank_code == code
        gv = jnp.sum(jnp.where(hit, scores, 0.0), axis=-1, keepdims=True)
        idx_f = jnp.where(lane == j, LANES - code, idx_f)
        gate_out = jnp.where(lane == j, gv, gate_out)
        gsum = gsum + gv
        sel = jnp.where(hit, -jnp.inf, sel)
    idx_ref[0] = idx_f.astype(jnp.int32)
    gate_ref[0] = gate_out / gsum * ROUTED_SCALE


def _shared_kernel(h1_ref, xa_ref, xb_ref, gf_ref, sg_ref, su_ref, sd_ref, o_ref):
    q = xa_ref.shape[2]
    parts = [p.astype(BF16) for p in _unpack_rows(xa_ref[0], xb_ref[0])]

    def proj(w_ref):
        acc = _dot(parts[0], w_ref[0:q, :])
        for j in range(1, 4):
            acc = acc + _dot(parts[j], w_ref[j * q:(j + 1) * q, :])
        return acc

    hid = _silu(proj(sg_ref)) * proj(su_ref)
    o_ref[0] = h1_ref[0] + gf_ref[0] * _dot(hid.astype(BF16), sd_ref[...])


def _shared_expert(h1, xpa, xpb, mod, sg, su, sd, *, ctx):
    bsz, t_all, d = h1.shape
    tm = TOKEN_TILE
    ctx_tiles = ctx // tm
    rows = mod.shape[0] // 6
    pw = xpa.shape[-1]
    tok = lambda w: pl.BlockSpec((1, tm, w), lambda b, t: (b, t, 0))
    full = lambda a: pl.BlockSpec(a.shape, lambda b, t: (0,) * a.ndim)
    return pl.pallas_call(
        _shared_kernel,
        grid=(bsz, t_all // tm),
        in_specs=[tok(d), tok(pw), tok(pw), pl.BlockSpec((1, 1, d), _mod_row(5, ctx_tiles, rows)),
                  full(sg), full(su), full(sd)],
        out_specs=tok(d),
        out_shape=jax.ShapeDtypeStruct((bsz, t_all, d), F32),
        compiler_params=_params(("parallel", "parallel")),
        name="shared_expert",
    )(h1, xpa, xpb, mod, sg, su, sd)


def _outproj(ys, yh, yg, u, h, mod, nrm, w_out, norm_ffn, router_wp, router_bp, *, ctx):
    bsz, t_all, d = h.shape
    tm = TOKEN_TILE
    ctx_tiles = ctx // tm
    rows = mod.shape[0] // 6
    mrow = lambda k: pl.BlockSpec((1, 1, d), _mod_row(k, ctx_tiles, rows))
    tok = lambda w, j=0: pl.BlockSpec((1, tm, w), lambda b, t: (b, t, j))
    full = lambda a: pl.BlockSpec(a.shape, lambda b, t: (0,) * a.ndim)
    in_specs = [tok(SSD_WIDTH), tok(SSD_WIDTH), tok(HG_WIDTH), tok(HG_WIDTH), tok(GD_WIDTH), tok(GD_WIDTH),
                tok(SSD_WIDTH, Z_OFF // SSD_WIDTH), tok(HG_WIDTH, HGATE_OFF // HG_WIDTH),
                tok(GD_WIDTH, GGATE_OFF // GD_WIDTH),
                tok(d), mrow(2), mrow(4), mrow(3), mrow(5),
                full(nrm), full(w_out), full(norm_ffn), full(router_wp), full(router_bp)]
    group_matrix = lambda n, g: (jnp.arange(n)[:, None] // g == jnp.arange(n)[None, :] // g).astype(BF16)
    bd_s = group_matrix(SSD_WIDTH, SSD_WIDTH // SSD_GROUPS)
    bd_h = group_matrix(HG_WIDTH, HEAD)
    in_specs += [full(bd_s), full(bd_h)]
    pw = d // 4
    out_shape = (jax.ShapeDtypeStruct((bsz, t_all, d), F32),
                 jax.ShapeDtypeStruct((bsz, t_all, pw), jnp.uint32), jax.ShapeDtypeStruct((bsz, t_all, pw), jnp.uint32),
                 jax.ShapeDtypeStruct((bsz, t_all, LANES), jnp.int32), jax.ShapeDtypeStruct((bsz, t_all, LANES), F32))
    out_specs = (tok(d), tok(pw), tok(pw), tok(LANES), tok(LANES))
    return pl.pallas_call(
        _outproj_kernel,
        grid=(bsz, t_all // tm),
        in_specs=in_specs,
        out_specs=out_specs,
        out_shape=out_shape,
        compiler_params=_params(("parallel", "parallel")),
        name="out_projection_router",
    )(ys[0], ys[1], yh[0], yh[1], yg[0], yg[1], u, u, u, h, mod, mod, mod, mod,
      nrm, w_out, norm_ffn, router_wp, router_bp, bd_s, bd_h)


def _rank_kernel(idx_ref, rank_ref, exp_ref, cnt_ref, base_ref):
    tm = idx_ref.shape[0]

    @pl.when(pl.program_id(0) == 0)
    def _():
        base_ref[...] = jnp.zeros_like(base_ref)

    idx = idx_ref[...]
    lane = _iota((tm, LANES), 1)
    hits = [lane == idx[:, j:j + 1] for j in range(TOP_K)]
    m = jnp.zeros((tm, LANES), F32)
    for hit in hits:
        m = m + jnp.where(hit, 1.0, 0.0)
    before = _ones_where(_iota((tm, tm), 1) < _iota((tm, tm), 0))
    base = base_ref[...]
    val = _dot(before, m.astype(BF16)) + base
    out = jnp.zeros((tm, LANES), F32)
    for j, hit in enumerate(hits):
        rj = jnp.sum(jnp.where(hit, val, 0.0), axis=-1, keepdims=True)
        out = jnp.where(lane == j, rj, out)
    rank_ref[...] = out.T[0:TOP_K].astype(jnp.int32)
    exp_ref[...] = idx.astype(F32).T[0:TOP_K].astype(jnp.int32)
    total = base + jnp.sum(m, axis=0, keepdims=True)
    base_ref[...] = total
    cnt_ref[...] = total


def _route_ranks(idx2d):
    n_tok = idx2d.shape[0]
    tm = TOKEN_TILE
    per_choice = lambda: pl.BlockSpec((TOP_K, tm), lambda i: (0, i))
    return pl.pallas_call(
        _rank_kernel,
        grid=(n_tok // tm,),
        in_specs=[pl.BlockSpec((tm, LANES), lambda i: (i, 0))],
        out_specs=(per_choice(), per_choice(), pl.BlockSpec((1, LANES), lambda i: (0, 0))),
        out_shape=(jax.ShapeDtypeStruct((TOP_K, n_tok), jnp.int32), jax.ShapeDtypeStruct((TOP_K, n_tok), jnp.int32),
                   jax.ShapeDtypeStruct((1, LANES), F32)),
        scratch_shapes=[pltpu.VMEM((1, LANES), F32)],
        compiler_params=_params(("arbitrary",)),
        name="route_ranks",
    )(idx2d)


def _expert_kernel(be_ref, xa_ref, xb_ref, wg_ref, wu_ref, wd_ref, oa_ref, ob_ref, wgb_ref, wub_ref, wdb_ref):
    q = xa_ref.shape[1]
    i = pl.program_id(0)
    n_live = be_ref[be_ref.shape[0] - 1]

    @pl.when((i == 0) | (be_ref[i] != be_ref[jnp.maximum(i - 1, 0)]))
    def _():
        wgb_ref[...] = wg_ref[0].astype(BF16)
        wub_ref[...] = wu_ref[0].astype(BF16)
        wdb_ref[...] = wd_ref[0].astype(BF16)

    @pl.when(i < n_live)
    def _():
        parts = [p.astype(BF16) for p in _unpack_rows(xa_ref[...], xb_ref[...])]

        def proj(w_ref):
            acc = _dot(parts[0], w_ref[0:q, :])
            for j in range(1, 4):
                acc = acc + _dot(parts[j], w_ref[j * q:(j + 1) * q, :])
            return acc

        hid = _silu(proj(wgb_ref)) * proj(wub_ref)
        oa_ref[...], ob_ref[...] = _pack_rows(_dot(hid.astype(BF16), wdb_ref[...]))

    @pl.when(i >= n_live)
    def _():
        oa_ref[...] = jnp.zeros_like(oa_ref)
        ob_ref[...] = jnp.zeros_like(ob_ref)


def _expert_blocks(xsa, xsb, block_expert, w_gate, w_up, w_down):
    n_rows, pw = xsa.shape
    bm = EXPERT_ROWS
    d, e_dim = w_gate.shape[1], w_gate.shape[2]
    assert block_expert.shape[0] == n_rows // bm + 1
    row = lambda: pl.BlockSpec((bm, pw), lambda i, be: (i, 0))
    grid_spec = pltpu.PrefetchScalarGridSpec(
        num_scalar_prefetch=1,
        grid=(n_rows // bm,),
        in_specs=[row(), row(),
                  pl.BlockSpec((1, d, e_dim), lambda i, be: (be[i], 0, 0)),
                  pl.BlockSpec((1, d, e_dim), lambda i, be: (be[i], 0, 0)),
                  pl.BlockSpec((1, e_dim, d), lambda i, be: (be[i], 0, 0))],
        out_specs=(row(), row()),
        scratch_shapes=[pltpu.VMEM((d, e_dim), BF16), pltpu.VMEM((d, e_dim), BF16), pltpu.VMEM((e_dim, d), BF16)],
    )
    return pl.pallas_call(
        _expert_kernel,
        grid_spec=grid_spec,
        out_shape=(jax.ShapeDtypeStruct((n_rows, pw), jnp.uint32), jax.ShapeDtypeStruct((n_rows, pw), jnp.uint32)),
        compiler_params=_params(("arbitrary",)),
        name="routed_experts",
    )(block_expert, xsa, xsb, w_gate, w_up, w_down)


def _combine_kernel(h_ref, ra_ref, rb_ref, gate_ref, gf_ref, xa_ref, xb_ref, sg_ref, su_ref, sd_ref, *rest):
    nw_ref, o_ref = (rest[0], rest[1]) if len(rest) == 2 else (None, rest[0])
    tm, q = ra_ref.shape[1], ra_ref.shape[2]
    gate = gate_ref[0]
    parts = [p.astype(BF16) for p in _unpack_rows(xa_ref[0], xb_ref[0])]

    def proj(w_ref):
        a = _dot(parts[0], w_ref[0:q, :])
        for j in range(1, 4):
            a = a + _dot(parts[j], w_ref[j * q:(j + 1) * q, :])
        return a

    shared = _dot((_silu(proj(sg_ref)) * proj(su_ref)).astype(BF16), sd_ref[...])
    acc = [shared[:, blk * q:(blk + 1) * q] for blk in range(4)]
    for j in range(TOP_K):
        gj = gate[:, j:j + 1]
        for blk, part in enumerate(_unpack_rows(ra_ref[j], rb_ref[j])):
            acc[blk] = acc[blk] + part * gj
    out = [h_ref[0, :, blk * q:(blk + 1) * q] + gf_ref[0, :, blk * q:(blk + 1) * q] * acc[blk] for blk in range(4)]
    if nw_ref is not None:
        ms = sum(jnp.sum(o * o, axis=-1, keepdims=True) for o in out) * (1.0 / (4 * q))
        scale = lax.rsqrt(ms + EPS)
        out = [o * scale * nw_ref[:, blk * q:(blk + 1) * q] for blk, o in enumerate(out)]
    for blk in range(4):
        o_ref[0, :, blk * q:(blk + 1) * q] = out[blk]


def _combine(h2, ra, rb, gate, mod, xpa, xpb, sg, su, sd, final_w, *, ctx):
    bsz, t_all, d = h2.shape
    tm = TOKEN_TILE
    tiles = t_all // tm
    ctx_tiles = ctx // tm
    rows = mod.shape[0] // 6
    pw = ra.shape[-1]
    skip = ctx_tiles if final_w is not None else 0
    per = lambda: pl.BlockSpec((TOP_K, tm, pw), lambda b, t: (0, b * tiles + t + skip, 0))
    in_specs = [pl.BlockSpec((1, tm, d), lambda b, t: (b, t + skip, 0)), per(), per(),
                pl.BlockSpec((1, tm, LANES), lambda b, t: (b, t + skip, 0)),
                pl.BlockSpec((1, 1, d), lambda b, t: _mod_row(5, ctx_tiles, rows)(b, t + skip))]
    full = lambda a: pl.BlockSpec(a.shape, lambda b, t: (0,) * a.ndim)
    in_specs += [pl.BlockSpec((1, tm, pw), lambda b, t: (b, t + skip, 0)),
                 pl.BlockSpec((1, tm, pw), lambda b, t: (b, t + skip, 0)), full(sg), full(su), full(sd)]
    args = [h2, ra, rb, gate, mod, xpa, xpb, sg, su, sd]
    if final_w is not None:
        in_specs.append(pl.BlockSpec((1, d), lambda b, t: (0, 0)))
        args.append(final_w)
    return pl.pallas_call(
        _combine_kernel,
        grid=(bsz, tiles - skip),
        in_specs=in_specs,
        out_specs=pl.BlockSpec((1, tm, d), lambda b, t: (b, t, 0)),
        out_shape=jax.ShapeDtypeStruct((bsz, t_all - skip * tm, d), F32),
        compiler_params=_params(("parallel", "parallel")),
        name="moe_combine_final" if final_w is not None else "moe_combine",
    )(*args)


def _gather_rows(xa, xb, indices):
    n = indices.shape[0]
    q = xa.shape[1]
    win = GATHER_WINDOW
    mesh = plsc.VectorSubcoreMesh(core_axis_name="core", subcore_axis_name="subcore")
    workers = mesh.num_cores * mesh.num_subcores
    per = n // (win * workers)
    assert per * win * workers == n, (n, win, workers)
    out = jax.ShapeDtypeStruct((n, q), xa.dtype)
    scratch = [pltpu.VMEM((per, win), jnp.int32), pltpu.VMEM((win, q), xa.dtype), pltpu.VMEM((win, q), xa.dtype),
               pltpu.SemaphoreType.DMA((4,))]

    @functools.partial(pl.kernel, out_type=(out, out), mesh=mesh, scratch_types=scratch)
    def gather(xa_hbm, xb_hbm, i_hbm, oa_hbm, ob_hbm, idx_vmem, buf_a, buf_b, sems):
        wid = lax.axis_index("core") * mesh.num_subcores + lax.axis_index("subcore")
        pltpu.sync_copy(i_hbm.at[wid], idx_vmem)

        @pl.loop(0, per)
        def _(s):
            rows = pl.ds(pl.multiple_of((wid * per + s) * win, win), win)
            ga = pltpu.async_copy(xa_hbm.at[idx_vmem.at[s]], buf_a, sems.at[0])
            gb = pltpu.async_copy(xb_hbm.at[idx_vmem.at[s]], buf_b, sems.at[1])
            ga.wait()
            wa = pltpu.async_copy(buf_a, oa_hbm.at[rows], sems.at[2])
            gb.wait()
            wb = pltpu.async_copy(buf_b, ob_hbm.at[rows], sems.at[3])
            wa.wait()
            wb.wait()

    return gather(xa, xb, indices.reshape(workers, per, win))


def _scatter_rows(xa, xb, indices, n_out):
    n = indices.shape[0]
    n_src, q = xa.shape
    win = GATHER_WINDOW
    mesh = plsc.VectorSubcoreMesh(core_axis_name="core", subcore_axis_name="subcore")
    workers = mesh.num_cores * mesh.num_subcores
    per = n // (win * workers)
    assert per * win * workers == n and n_src % win == 0, (n, n_src, win, workers)
    src_windows = n_src // win
    out = jax.ShapeDtypeStruct((n_out, q), xa.dtype)
    scratch = [pltpu.VMEM((per, win), jnp.int32), pltpu.VMEM((win, q), xa.dtype), pltpu.VMEM((win, q), xa.dtype),
               pltpu.SemaphoreType.DMA((4,))]

    @functools.partial(pl.kernel, out_type=(out, out), mesh=mesh, scratch_types=scratch)
    def scatter(xa_hbm, xb_hbm, i_hbm, oa_hbm, ob_hbm, idx_vmem, buf_a, buf_b, sems):
        wid = lax.axis_index("core") * mesh.num_subcores + lax.axis_index("subcore")
        pltpu.sync_copy(i_hbm.at[wid], idx_vmem)

        @pl.loop(0, per)
        def _(s):
            src = lax.rem(wid * per + s, src_windows)
            rows = pl.ds(pl.multiple_of(src * win, win), win)
            la = pltpu.async_copy(xa_hbm.at[rows], buf_a, sems.at[0])
            lb = pltpu.async_copy(xb_hbm.at[rows], buf_b, sems.at[1])
            la.wait()
            sa = pltpu.async_copy(buf_a, oa_hbm.at[idx_vmem.at[s]], sems.at[2])
            lb.wait()
            sb = pltpu.async_copy(buf_b, ob_hbm.at[idx_vmem.at[s]], sems.at[3])
            sa.wait()
            sb.wait()

    return scatter(xa, xb, indices.reshape(workers, per, win))


def _routed(xpa, xpb, idx, w_gate, w_up, w_down, layer):
    bsz, t_all, pw = xpa.shape
    n_tok = bsz * t_all
    n_assign = n_tok * TOP_K
    bm = EXPERT_ROWS
    n_blocks = -(-n_assign // bm) + N_EXPERTS
    n_slots = n_blocks * bm
    idx2d = idx.reshape(n_tok, LANES)
    rank_t, exp_t, cnt = _route_ranks(idx2d)
    counts = cnt[0, :N_EXPERTS].astype(jnp.int32)
    padded = (counts + bm - 1) // bm * bm
    pad_end = jnp.cumsum(padded)
    offset = pad_end - padded
    slot_t = jnp.take(offset, exp_t.reshape(-1)) + rank_t.reshape(-1)
    block_expert = jnp.minimum(
        jnp.sum(pad_end[None, :] <= (jnp.arange(n_blocks, dtype=jnp.int32) * bm)[:, None], axis=1),
        N_EXPERTS - 1).astype(jnp.int32)
    xsa, xsb = _scatter_rows(xpa.reshape(n_tok, pw), xpb.reshape(n_tok, pw), slot_t, n_slots)
    stack = lambda w: w.reshape((-1,) + w.shape[2:])
    table = jnp.concatenate([block_expert + layer * N_EXPERTS, (pad_end[-1:] // bm).astype(jnp.int32)])
    oa, ob = _expert_blocks(xsa, xsb, table, stack(w_gate), stack(w_up), stack(w_down))
    ra, rb = _gather_rows(oa, ob, slot_t)
    return ra.reshape(TOP_K, n_tok, pw), rb.reshape(TOP_K, n_tok, pw)


def kernel(x, c, ctx, c_ctx, w_mod, b_mod, norm_mix, w_in, ssd_conv_w, ssd_conv_b, ssd_dt_bias, ssd_a_log, ssd_d,
           ssd_norm, hgrn_lb, hgrn_norm, gdn_conv_w, gdn_dt_bias, gdn_a_log, gdn_norm, w_out, norm_ffn, router_w,
           router_bias, exp_gate, exp_up, exp_down, sh_gate, sh_up, sh_down, norm_final):
    bsz, seq, d = x.shape
    n_ctx_tok = ctx.shape[1]
    depth = w_in.shape[0]
    assert d == D_MODEL and n_ctx_tok == TOKEN_TILE and seq % TOKEN_TILE == 0 and TOKEN_TILE % SCAN_CHUNK == 0
    assert TOKEN_TILE % GRID_W == 0 and seq % GRID_W == 0
    n_ctx = n_ctx_tok // SCAN_CHUNK

    p_lb = jax.nn.softmax(hgrn_lb.astype(F32), axis=0)
    lower_all = jnp.cumsum(p_lb, axis=0) - p_lb[0]

    layers = []
    for l in range(depth):
        nrm = jnp.concatenate([ssd_norm[l], jnp.tile(hgrn_norm[l], HG_WIDTH // HEAD),
                               jnp.tile(gdn_norm[l], GD_WIDTH // HEAD)]).astype(F32).reshape(1, d)
        layers.append(dict(
            w_p=_permute_w_in(w_in[l]).astype(BF16), nrm=nrm,
            rwp=jnp.pad(router_w[l].astype(F32), ((0, 0), (0, LANES - N_EXPERTS))),
            rbp=jnp.pad(router_bias[l].astype(F32), (0, LANES - N_EXPERTS)).reshape(1, LANES),
            w_out=w_out[l].astype(BF16), sg=sh_gate[l].astype(BF16), su=sh_up[l].astype(BF16),
            sd=sh_down[l].astype(BF16)))

    def forward(xg, cg, ctxg):
        nb = xg.shape[0]
        rows = -(-(nb + 1) // 8) * 8
        cond = jnp.zeros((rows, d), F32).at[:nb].set(cg).at[rows - 1].set(c_ctx)
        h = jnp.concatenate([ctxg, xg], axis=1)
        for l in range(depth):
            p = layers[l]
            mod = _modulation(cond, w_mod[l], b_mod[l]).reshape(rows * 6, 1, d)
            u = _inproj(h, mod, norm_mix[l], p["w_p"], ssd_conv_w[l], ssd_conv_b[l], gdn_conv_w[l], ctx=n_ctx_tok)
            ys = [_ssd_scan(u, ssd_dt_bias[l], ssd_a_log[l], ssd_d[l], rev=r, n_ctx=n_ctx) for r in (False, True)]
            yh = [_hgrn_scan(u, lower_all[l], rev=r, n_ctx=n_ctx) for r in (False, True)]
            yg = [_gdn_scan(u, gdn_dt_bias[l], gdn_a_log[l], rev=r, n_ctx=n_ctx) for r in (False, True)]
            h1, xpa, xpb, idx, gate = _outproj(ys, yh, yg, u, h, mod, p["nrm"], p["w_out"],
                                               norm_ffn[l].reshape(1, d), p["rwp"], p["rbp"], ctx=n_ctx_tok)
            ra, rb = _routed(xpa, xpb, idx, exp_gate, exp_up, exp_down, l)
            last = l == depth - 1
            h = _combine(h1, ra, rb, gate, mod, xpa, xpb, p["sg"], p["su"], p["sd"],
                         norm_final.reshape(1, d) if last else None, ctx=n_ctx_tok)
        return h

    groups = SAMPLE_GROUPS if bsz % (SAMPLE_GROUPS * SCAN_BATCH) == 0 else 1
    gs = bsz // groups
    outs = [forward(x[g * gs:(g + 1) * gs], c[g * gs:(g + 1) * gs], ctx[g * gs:(g + 1) * gs]) for g in range(groups)]
    return outs[0] if groups == 1 else jnp.concatenate(outs, axis=0)
```
